```python
import jax, jax.numpy as jnp
from jax import lax
import numpy as np

D_MODEL = 1024
BATCH = 8
SEQ = 2048
DEPTH = 2
DEC_BATCH = 32
DEC_SEQ = 8
PAST_LEN = 16384
PAGE_SIZE = 128

EPS = 1e-6
D_FF = 2816
HG_HEADS = 4
HG_DK = 64
HG_DV = 64
HG_KW = HG_HEADS * HG_DK
HG_WIDTH = HG_HEADS * HG_DV
HG_CHUNK = 16
HG_MIN_F = 1e-30
RW_HEADS = 4
RW_DH = 64
RW_WIDTH = RW_HEADS * RW_DH
RW_DECAY_LORA = 32
RW_AAA_LORA = 32
RW_GATE_LORA = 64
RW_GN_EPS = 64e-5
MLA_HEADS = 8
MLA_NOPE = 64
MLA_ROPE = 32
MLA_VDIM = 64
MLA_Q_RANK = 384
MLA_KV_RANK = 256
MLA_WIDTH = MLA_HEADS * MLA_VDIM
MLA_SCALE = (MLA_NOPE + MLA_ROPE) ** -0.5
ROPE_THETA = 10000.0
Q_BLOCK = 128
MIX_WIDTH = HG_WIDTH + RW_WIDTH + MLA_WIDTH
HG_COLS = 2 * HG_KW + 2 * HG_WIDTH
RW_COLS = 3 * RW_WIDTH + RW_DECAY_LORA + RW_AAA_LORA + RW_GATE_LORA
MLA_COLS = MLA_Q_RANK + MLA_KV_RANK + MLA_ROPE
IN_COLS = HG_COLS + RW_COLS + MLA_COLS
RW_SPLITS = [RW_WIDTH, 2 * RW_WIDTH, 3 * RW_WIDTH, 3 * RW_WIDTH + RW_DECAY_LORA, 3 * RW_WIDTH + RW_DECAY_LORA + RW_AAA_LORA]

kernel_name = 'hymba_hgrn2_rwkv7_mla_macaron_step'


def rms_norm(x, g):
    xf = x.astype(jnp.float32)
    y = xf * lax.rsqrt(jnp.mean(xf * xf, axis=-1, keepdims=True) + EPS)
    return (y * g.astype(jnp.float32)).astype(x.dtype)


def swiglu(x, wi, wo):
    a, b = jnp.split(x @ wi, 2, axis=-1)
    return (jax.nn.silu(a) * b) @ wo


def rope(x, pos):
    half = x.shape[-1] // 2
    inv = jnp.exp(-jnp.log(ROPE_THETA) * jnp.arange(half, dtype=jnp.float32) / half)
    ang = pos[:, None] * inv[None, :]
    cos = jnp.cos(ang)[:, None, :]
    sin = jnp.sin(ang)[:, None, :]
    xf = x.astype(jnp.float32)
    x1, x2 = xf[..., :half], xf[..., half:]
    return jnp.concatenate([x1 * cos - x2 * sin, x2 * cos + x1 * sin], axis=-1).astype(x.dtype)


def hgrn_lower_bounds(logits):
    p = jax.nn.softmax(logits.astype(jnp.float32), axis=0)
    return jnp.clip(jnp.cumsum(p, axis=0) - p[0:1], 0.0, 1.0)


def gla_chunked(q, k, v, log_f, s0):
    B, T, H, _ = q.shape
    C = min(HG_CHUNK, T)
    n = -(-T // C)
    pad = n * C - T

    def prep(a):
        a = jnp.pad(a, ((0, 0), (0, pad), (0, 0), (0, 0)))
        return a.reshape(B, n, C, H, a.shape[-1]).transpose(1, 0, 3, 2, 4)

    qc, kc, vc, gc = prep(q), prep(k), prep(v), prep(log_f)
    mask = jnp.tril(jnp.ones((C, C), dtype=bool))[:, :, None]

    def step(S, inp):
        qb, kb, vb, gb = inp
        b = jnp.cumsum(gb, axis=-2)
        o_inter = jnp.einsum('bhck,bhkv->bhcv', qb * jnp.exp(b), S)
        diff = b[:, :, :, None, :] - b[:, :, None, :, :]
        decay = jnp.where(mask, jnp.exp(jnp.where(mask, diff, 0.0)), 0.0)
        att = jnp.einsum('bhtk,bhtsk,bhsk->bhts', qb, decay, kb)
        o = o_inter + jnp.einsum('bhts,bhsv->bhtv', att, vb)
        b_last = b[:, :, -1:, :]
        S_new = jnp.exp(b_last[:, :, 0, :])[..., None] * S + jnp.einsum('bhsk,bhsv->bhkv', kb * jnp.exp(b_last - b), vb)
        return S_new, o

    S, o = lax.scan(step, s0, (qc, kc, vc, gc))
    o = o.transpose(1, 0, 3, 2, 4).reshape(B, n * C, H, -1)[:, :T]
    return o, S


def hgrn2_mixer(p, lb, norm_g, s0):
    B, T, _ = p.shape
    q, fz, i, g = jnp.split(p, [HG_KW, 2 * HG_KW, 2 * HG_KW + HG_WIDTH], axis=-1)
    fz = fz.astype(jnp.float32)
    f = lb + (1.0 - lb) * jax.nn.sigmoid(fz)
    log_f = jnp.log(jnp.maximum(f, HG_MIN_F))
    k = (1.0 - lb) * jax.nn.sigmoid(-fz)
    heads = lambda t, d: t.astype(jnp.float32).reshape(B, T, HG_HEADS, d)
    o, s = gla_chunked(heads(q, HG_DK), heads(k, HG_DK), heads(i, HG_DV), heads(log_f, HG_DK), s0.astype(jnp.float32))
    o = rms_norm(o, norm_g).reshape(B, T, HG_WIDTH)
    return o * jax.nn.silu(g.astype(jnp.float32)), s


def rwkv7_mixer(p, shift0, s0, l, W):
    f32 = jnp.float32
    B, T, _ = p.shape
    prev = jnp.concatenate([shift0[:, None, :].astype(p.dtype), p[:, :-1]], axis=1)
    ps = p + (prev - p) * W['rw_mu'][l]
    r, k, v, wd, ad, gd = jnp.split(ps, RW_SPLITS, axis=-1)
    w = -jax.nn.softplus(-(W['rw_w0'][l] + jnp.tanh(wd) @ W['rw_w2'][l])) - 0.5
    log_w = -jnp.exp(w.astype(f32))
    a = jax.nn.sigmoid((W['rw_a0'][l] + ad @ W['rw_a2'][l]).astype(f32))
    g = jax.nn.sigmoid(gd) @ W['rw_g2'][l]
    hd = lambda t: t.astype(f32).reshape(B, T, RW_HEADS, RW_DH)
    r_, k_, v_, a_, lw_ = hd(r), hd(k), hd(v), hd(a), hd(log_w)
    kk = k_ * W['rw_kk'][l].astype(f32).reshape(RW_HEADS, RW_DH)
    kk = kk / jnp.maximum(jnp.sqrt(jnp.sum(kk * kk, axis=-1, keepdims=True)), 1e-12)
    k_ = k_ * (1.0 + (a_ - 1.0) * W['rw_ka'][l].astype(f32).reshape(RW_HEADS, RW_DH))

    def step(S, inp):
        r_t, lw_t, k_t, v_t, kk_t, a_t = inp
        S = (S * jnp.exp(lw_t)[:, :, None, :]
             - jnp.einsum('bhvk,bhk->bhv', S, kk_t)[..., None] * (kk_t * a_t)[:, :, None, :]
             + v_t[..., None] * k_t[:, :, None, :])
        return S, jnp.einsum('bhvk,bhk->bhv', S, r_t)

    tm = lambda t: jnp.moveaxis(t, 1, 0)
    S, y = lax.scan(step, s0.astype(f32), (tm(r_), tm(lw_), tm(k_), tm(v_), tm(kk), tm(a_)))
    y = jnp.moveaxis(y, 0, 1)
    mu = jnp.mean(y, axis=-1, keepdims=True)
    var = jnp.mean(jnp.square(y - mu), axis=-1, keepdims=True)
    y = ((y - mu) * lax.rsqrt(var + RW_GN_EPS)).reshape(B, T, RW_WIDTH)
    y = y * W['rw_ln_w'][l].astype(f32) + W['rw_ln_b'][l].astype(f32)
    bonus = jnp.sum(r_ * k_ * W['rw_rk'][l].astype(f32), axis=-1, keepdims=True) * v_
    y = (y + bonus.reshape(B, T, RW_WIDTH)) * g.astype(f32)
    return y, S, p[:, -1]


def mla_project(p, pos, l, W):
    B, T, _ = p.shape
    qa, kva, kpe = jnp.split(p, [MLA_Q_RANK, MLA_Q_RANK + MLA_KV_RANK], axis=-1)
    q = (rms_norm(qa, W['mla_q_norm'][l]) @ W['mla_wqb'][l]).reshape(B, T, MLA_HEADS, MLA_NOPE + MLA_ROPE)
    q_nope = q[..., :MLA_NOPE]
    q_pe = rope(q[..., MLA_NOPE:], pos)
    c = rms_norm(kva, W['mla_kv_norm'][l])
    k_pe = rope(kpe[:, :, None, :], pos)[:, :, 0]
    return q_nope, q_pe, c, k_pe


def mla_prompt_attn(q_nope, q_pe, c, k_pe, wuk, wuv):
    B, T, H, _ = q_nope.shape
    k_nope = jnp.einsum('btr,rhd->bthd', c, wuk)
    v = jnp.einsum('btr,rhd->bthd', c, wuv)
    qb = min(Q_BLOCK, T)
    nb = T // qb
    kpos = jnp.arange(T)

    def block(i):
        start = i * qb
        qn = lax.dynamic_slice_in_dim(q_nope, start, qb, axis=1)
        qp = lax.dynamic_slice_in_dim(q_pe, start, qb, axis=1)
        s = jnp.einsum('bqhd,bkhd->bhqk', qn, k_nope) + jnp.einsum('bqhe,bke->bhqk', qp, k_pe)
        s = s.astype(jnp.float32) * MLA_SCALE
        qpos = start + jnp.arange(qb)
        s = jnp.where(kpos[None, :] <= qpos[:, None], s, -jnp.inf)
        pr = jax.nn.softmax(s, axis=-1).astype(v.dtype)
        return jnp.einsum('bhqk,bkhd->bqhd', pr, v)

    o = lax.map(block, jnp.arange(nb))
    return o.transpose(1, 0, 2, 3, 4).reshape(B, T, H * MLA_VDIM)


def mla_sample_attn(q_nope, q_pe, c, k_pe, ckv_pages, kpe_pages, wuk, wuv):
    B, T, H, _ = q_nope.shape
    c_past = ckv_pages.reshape(B, -1, MLA_KV_RANK).astype(c.dtype)
    kpe_past = kpe_pages.reshape(B, -1, MLA_ROPE).astype(k_pe.dtype)
    P = c_past.shape[1]
    q_lat = jnp.einsum('bqhd,rhd->bqhr', q_nope, wuk)
    s_past = jnp.einsum('bqhr,bkr->bhqk', q_lat, c_past) + jnp.einsum('bqhe,bke->bhqk', q_pe, kpe_past)
    s_new = jnp.einsum('bqhr,bkr->bhqk', q_lat, c) + jnp.einsum('bqhe,bke->bhqk', q_pe, k_pe)
    causal = jnp.tril(jnp.ones((T, T), dtype=bool))
    s_new = jnp.where(causal, s_new.astype(jnp.float32) * MLA_SCALE, -jnp.inf)
    s = jnp.concatenate([s_past.astype(jnp.float32) * MLA_SCALE, s_new], axis=-1)
    pr = jax.nn.softmax(s, axis=-1).astype(c.dtype)
    o_lat = jnp.einsum('bhqk,bkr->bqhr', pr[..., :P], c_past) + jnp.einsum('bhqk,bkr->bqhr', pr[..., P:], c)
    o = jnp.einsum('bqhr,rhd->bqhd', o_lat, wuv)
    return o.reshape(B, T, H * MLA_VDIM)


def trunk(x, pos, hg_s0, rw_s0, rw_sh0, attend, W):
    lbs = hgrn_lower_bounds(W['hg_lb_logits'])
    hg_out, rw_out, sh_out, c_out, kpe_out = [], [], [], [], []
    for l in range(DEPTH):
        x = x + 0.5 * swiglu(rms_norm(x, W['norm_ffn1'][l]), W['ffn1_wi'][l], W['ffn1_wo'][l])
        p = rms_norm(x, W['norm_mix'][l]) @ W['w_in'][l]
        p_hg, p_rw, p_mla = jnp.split(p, [HG_COLS, HG_COLS + RW_COLS], axis=-1)
        o_hg, hg_s = hgrn2_mixer(p_hg, lbs[l], W['hg_norm'][l], hg_s0[l])
        o_rw, rw_s, rw_sh = rwkv7_mixer(p_rw, rw_sh0[l], rw_s0[l], l, W)
        q_nope, q_pe, c, k_pe = mla_project(p_mla, pos, l, W)
        o_mla = attend(l, q_nope, q_pe, c, k_pe)
        o = jnp.concatenate([o_hg.astype(x.dtype), o_rw.astype(x.dtype), o_mla.astype(x.dtype)], axis=-1)
        x = x + o @ W['w_out'][l]
        x = x + 0.5 * swiglu(rms_norm(x, W['norm_ffn2'][l]), W['ffn2_wi'][l], W['ffn2_wo'][l])
        hg_out.append(hg_s)
        rw_out.append(rw_s)
        sh_out.append(rw_sh)
        c_out.append(c)
        kpe_out.append(k_pe)
    y = rms_norm(x, W['norm_final'])
    return y, jnp.stack(hg_out), jnp.stack(rw_out), jnp.stack(sh_out), jnp.stack(c_out), jnp.stack(kpe_out)


def setup_inputs(seed: int = 0) -> dict:
    key = jax.random.key(seed)
    keys = iter(jax.random.split(key, 48))
    f32 = jnp.float32
    nrm = lambda shape, s=1.0: jax.random.normal(next(keys), shape, f32) * s
    lin = lambda shape, fan_in: nrm(shape, fan_in ** -0.5)
    gain = lambda shape: 1.0 + nrm(shape, 0.02)
    n_pages = PAST_LEN // PAGE_SIZE
    n_used = DEC_BATCH * n_pages
    n_pool = n_used + n_used // 4
    perm = jax.random.permutation(next(keys), n_pool)
    page_table = perm[:n_used].reshape(DEC_BATCH, n_pages).astype(jnp.int32)
    return {
        'x_prompt': nrm((BATCH, SEQ, D_MODEL)),
        'x_sample': nrm((DEC_BATCH, DEC_SEQ, D_MODEL)),
        'cache_mla_ckv': nrm((DEPTH, n_pool, PAGE_SIZE, MLA_KV_RANK)),
        'cache_mla_kpe': nrm((DEPTH, n_pool, PAGE_SIZE, MLA_ROPE)),
        'state_hgrn': nrm((DEPTH, DEC_BATCH, HG_HEADS, HG_DK, HG_DV), 0.5),
        'state_rwkv': nrm((DEPTH, DEC_BATCH, RW_HEADS, RW_DH, RW_DH), 0.5),
        'state_rwkv_shift': nrm((DEPTH, DEC_BATCH, RW_COLS)),
        'page_table': page_table,
        'norm_ffn1': gain((DEPTH, D_MODEL)),
        'ffn1_wi': lin((DEPTH, D_MODEL, 2 * D_FF), D_MODEL),
        'ffn1_wo': lin((DEPTH, D_FF, D_MODEL), D_FF),
        'norm_mix': gain((DEPTH, D_MODEL)),
        'w_in': lin((DEPTH, D_MODEL, IN_COLS), D_MODEL),
        'w_out': lin((DEPTH, MIX_WIDTH, D_MODEL), MIX_WIDTH),
        'hg_lb_logits': nrm((DEPTH, HG_KW)),
        'hg_norm': gain((DEPTH, HG_DV)),
        'rw_mu': jax.random.uniform(next(keys), (DEPTH, RW_COLS), f32),
        'rw_w0': jax.random.uniform(next(keys), (DEPTH, RW_WIDTH), f32, minval=-6.0, maxval=-1.0),
        'rw_w2': lin((DEPTH, RW_DECAY_LORA, RW_WIDTH), RW_DECAY_LORA),
        'rw_a0': nrm((DEPTH, RW_WIDTH), 0.1),
        'rw_a2': lin((DEPTH, RW_AAA_LORA, RW_WIDTH), RW_AAA_LORA),
        'rw_g2': lin((DEPTH, RW_GATE_LORA, RW_WIDTH), RW_GATE_LORA),
        'rw_kk': 0.85 + nrm((DEPTH, RW_WIDTH), 0.02),
        'rw_ka': gain((DEPTH, RW_WIDTH)),
        'rw_rk': nrm((DEPTH, RW_HEADS, RW_DH), 0.1),
        'rw_ln_w': gain((DEPTH, RW_WIDTH)),
        'rw_ln_b': nrm((DEPTH, RW_WIDTH), 0.02),
        'mla_q_norm': gain((DEPTH, MLA_Q_RANK)),
        'mla_wqb': lin((DEPTH, MLA_Q_RANK, MLA_HEADS * (MLA_NOPE + MLA_ROPE)), MLA_Q_RANK),
        'mla_kv_norm': gain((DEPTH, MLA_KV_RANK)),
        'mla_wuk': lin((DEPTH, MLA_KV_RANK, MLA_HEADS, MLA_NOPE), MLA_KV_RANK),
        'mla_wuv': lin((DEPTH, MLA_KV_RANK, MLA_HEADS, MLA_VDIM), MLA_KV_RANK),
        'norm_ffn2': gain((DEPTH, D_MODEL)),
        'ffn2_wi': lin((DEPTH, D_MODEL, 2 * D_FF), D_MODEL),
        'ffn2_wo': lin((DEPTH, D_FF, D_MODEL), D_FF),
        'norm_final': gain((D_MODEL,)),
    }


def reference(x_prompt, x_sample, cache_mla_ckv, cache_mla_kpe, state_hgrn, state_rwkv, state_rwkv_shift, page_table,
              norm_ffn1, ffn1_wi, ffn1_wo, norm_mix, w_in, w_out, hg_lb_logits, hg_norm,
              rw_mu, rw_w0, rw_w2, rw_a0, rw_a2, rw_g2, rw_kk, rw_ka, rw_rk, rw_ln_w, rw_ln_b,
              mla_q_norm, mla_wqb, mla_kv_norm, mla_wuk, mla_wuv, norm_ffn2, ffn2_wi, ffn2_wo, norm_final):
    W = dict(norm_ffn1=norm_ffn1, ffn1_wi=ffn1_wi, ffn1_wo=ffn1_wo, norm_mix=norm_mix, w_in=w_in, w_out=w_out,
             hg_lb_logits=hg_lb_logits, hg_norm=hg_norm, rw_mu=rw_mu, rw_w0=rw_w0, rw_w2=rw_w2, rw_a0=rw_a0,
             rw_a2=rw_a2, rw_g2=rw_g2, rw_kk=rw_kk, rw_ka=rw_ka, rw_rk=rw_rk, rw_ln_w=rw_ln_w, rw_ln_b=rw_ln_b,
             mla_q_norm=mla_q_norm, mla_wqb=mla_wqb, mla_kv_norm=mla_kv_norm, mla_wuk=mla_wuk, mla_wuv=mla_wuv,
             norm_ffn2=norm_ffn2, ffn2_wi=ffn2_wi, ffn2_wo=ffn2_wo, norm_final=norm_final)
    Bp, Tp, _ = x_prompt.shape
    pos_p = jnp.arange(Tp, dtype=jnp.float32)
    hg0 = jnp.zeros((DEPTH, Bp, HG_HEADS, HG_DK, HG_DV), jnp.float32)
    rw0 = jnp.zeros((DEPTH, Bp, RW_HEADS, RW_DH, RW_DH), jnp.float32)
    sh0 = jnp.zeros((DEPTH, Bp, RW_COLS), x_prompt.dtype)
    attend_p = lambda l, qn, qp, c, kp: mla_prompt_attn(qn, qp, c, kp, mla_wuk[l], mla_wuv[l])
    y_p, hg_p, rw_p, sh_p, ckv_p, kpe_p = trunk(x_prompt, pos_p, hg0, rw0, sh0, attend_p, W)
    past_len = page_table.shape[1] * cache_mla_ckv.shape[2]
    pos_s = past_len + jnp.arange(x_sample.shape[1], dtype=jnp.float32)
    attend_s = lambda l, qn, qp, c, kp: mla_sample_attn(qn, qp, c, kp, cache_mla_ckv[l, page_table],
                                                         cache_mla_kpe[l, page_table], mla_wuk[l], mla_wuv[l])
    y_s, hg_s, rw_s, sh_s, ckv_s, kpe_s = trunk(x_sample, pos_s, state_hgrn, state_rwkv, state_rwkv_shift, attend_s, W)
    return (y_p, y_s, hg_p, hg_s, rw_p, rw_s, sh_p, sh_s, ckv_p, ckv_s, kpe_p, kpe_s)
```

```python
import functools
import math

import jax
import jax.numpy as jnp
import numpy as np
from jax import lax
from jax.experimental import pallas as pl
from jax.experimental.pallas import tpu as pltpu

F32 = jnp.float32
BF16 = jnp.bfloat16

D_MODEL = 1024
DEPTH = 2
PAGE_SIZE = 128
EPS = 1e-6
D_FF = 2816
HG_HEADS = 4
HG_DK = 64
HG_KW = 256
HG_MIN_F = 1e-30
RW_HEADS = 4
RW_DH = 64
RW_WIDTH = 256
RW_COLS = 896
RW_GN_EPS = 64e-5
MLA_HEADS = 8
MLA_NOPE = 64
MLA_ROPE = 32
MLA_VDIM = 64
MLA_Q_RANK = 384
MLA_KV_RANK = 256
MLA_SCALE = (MLA_NOPE + MLA_ROPE) ** -0.5
ROPE_THETA = 10000.0

LANES = 128
HEAD_SLOT = 128
ROPE_LANE0 = 64
VMEM_LIMIT = 56 * 1024 * 1024
NEG_INF = -1e30


def _dot(a, b):
    return jnp.dot(a.astype(BF16), b.astype(BF16), preferred_element_type=F32)


def _dot_nt(a, b):
    return lax.dot_general(a.astype(BF16), b.astype(BF16), (((1,), (1,)), ((), ())), preferred_element_type=F32)


def _split(a):
    hi = a.astype(BF16)
    lo = (a - hi.astype(F32)).astype(BF16)
    return hi, lo


def _dot3(a, b):
    ah, al = _split(a)
    bh, bl = _split(b)
    d = functools.partial(jnp.dot, preferred_element_type=F32)
    return d(ah, bh) + d(ah, bl) + d(al, bh)


def _dot3_nt(a, b):
    ah, al = _split(a)
    bh, bl = _split(b)
    d = functools.partial(lax.dot_general, dimension_numbers=(((1,), (1,)), ((), ())), preferred_element_type=F32)
    return d(ah, bh) + d(ah, bl) + d(al, bh)


def _dot_xl(m, b, pieces=2):
    out = None
    rem = b
    for _ in range(pieces):
        part = rem.astype(BF16)
        term = jnp.dot(m, part, preferred_element_type=F32)
        out = term if out is None else out + term
        rem = rem - part.astype(F32)
    return out


def _dot_xr(a, m, pieces=2):
    out = None
    rem = a
    for _ in range(pieces):
        part = rem.astype(BF16)
        term = jnp.dot(part, m, preferred_element_type=F32)
        out = term if out is None else out + term
        rem = rem - part.astype(F32)
    return out


def _rms(x, g, eps=EPS):
    return x * lax.rsqrt(jnp.mean(x * x, axis=-1, keepdims=True) + eps) * g


def _sigmoid(x):
    return 1.0 / (1.0 + jnp.exp(-x))


def _iota(shape, dim):
    return lax.broadcasted_iota(jnp.int32, shape, dim)


def _head_block_ones(n, hd):
    return (_iota((n, n), 0) // hd == _iota((n, n), 1) // hd).astype(BF16)


def _full(shape):
    return pl.BlockSpec(shape, lambda *_: (0,) * len(shape), pipeline_mode=pl.Buffered(1))


def _params(sem):
    return pltpu.CompilerParams(dimension_semantics=sem, vmem_limit_bytes=VMEM_LIMIT)


FF_CHUNK = 256


def _ffn_kernel(*refs, has_mix, final):
    it = iter(refs)
    x_ref = next(it)
    if has_mix:
        ohg_ref, orw_ref, omla_ref, wout_ref = next(it), next(it), next(it), next(it)
    g_ref, wi_ref, wo_ref = next(it), next(it), next(it)
    gf_ref = next(it) if final else None
    o_ref = next(it)

    x = x_ref[...]
    if has_mix:
        o = jnp.concatenate([r[...].astype(BF16) for r in (ohg_ref, orw_ref, omla_ref)], axis=-1)
        x = x + jnp.dot(o, wout_ref[...], preferred_element_type=F32)
    xn = _rms(x, g_ref[...]).astype(BF16)
    acc = jnp.zeros(x.shape, F32)
    for c in range(D_FF // FF_CHUNK):
        lo = c * FF_CHUNK
        a = jnp.dot(xn, wi_ref[:, lo:lo + FF_CHUNK], preferred_element_type=F32)
        b = jnp.dot(xn, wi_ref[:, D_FF + lo:D_FF + lo + FF_CHUNK], preferred_element_type=F32)
        h = (a * _sigmoid(a) * b).astype(BF16)
        acc = acc + jnp.dot(h, wo_ref[lo:lo + FF_CHUNK, :], preferred_element_type=F32)
    y = x + 0.5 * acc
    if final:
        y = _rms(y, gf_ref[...])
    o_ref[...] = y


def _ffn(x, g, wi, wo, mix=None, wout=None, gf=None, tm=512):
    n = x.shape[0]
    tm = min(tm, n)
    row = lambda w: pl.BlockSpec((tm, w), lambda i: (i, 0))
    args, specs = [x], [row(D_MODEL)]
    if mix is not None:
        for m in mix:
            args.append(m)
            specs.append(row(m.shape[1]))
        args.append(wout)
        specs.append(_full(wout.shape))
    args += [g, wi, wo]
    specs += [_full(g.shape), _full(wi.shape), _full(wo.shape)]
    if gf is not None:
        args.append(gf)
        specs.append(_full(gf.shape))
    return pl.pallas_call(
        functools.partial(_ffn_kernel, has_mix=mix is not None, final=gf is not None),
        grid=(n // tm,), in_specs=specs, out_specs=row(D_MODEL),
        out_shape=jax.ShapeDtypeStruct((n, D_MODEL), F32),
        compiler_params=_params(("parallel",)), name="ffn")(*args)


MLA_PAD = 896


def _proj_kernel(*refs, seq_len, pos0, sample):
    it = iter(refs)
    x_ref, g_ref, whg_ref, wrw_ref, wmla_ref = (next(it) for _ in range(5))
    qn_ref, wq_ref, kvn_ref, rope_ref = (next(it) for _ in range(4))
    if sample:
        wukt_ref = next(it)
    else:
        wuk_ref, wuv_ref = next(it), next(it)
    phg_ref, prw_ref, q_ref = next(it), next(it), next(it)
    if sample:
        qlat_ref = next(it)
    else:
        k_ref, v_ref = next(it), next(it)
    c_ref, kpe_ref = next(it), next(it)

    tm = x_ref.shape[0]
    xn = _rms(x_ref[...], g_ref[...]).astype(BF16)
    phg_ref[...] = jnp.dot(xn, whg_ref[...], preferred_element_type=F32)
    prw_ref[...] = jnp.dot(xn, wrw_ref[...], preferred_element_type=F32)
    pm = jnp.dot(xn, wmla_ref[...], preferred_element_type=F32)

    row = pl.program_id(0) * tm + _iota((tm, 1), 0)
    pos = (pos0 + (row & (seq_len - 1))).astype(F32)
    rope = rope_ref[...]
    ang = pos * rope[0:1, :]
    cos_s = rope[1:2, :] + rope[3:4, :] * jnp.cos(ang)
    sin_s = rope[2:3, :] * jnp.sin(ang)

    c = _rms(pm[:, MLA_Q_RANK:MLA_Q_RANK + MLA_KV_RANK], kvn_ref[...])
    c_ref[...] = c
    k_slot = pm[:, 640:768] * cos_s + pm[:, 768:896] * sin_s
    kpe_ref[...] = k_slot[:, ROPE_LANE0:ROPE_LANE0 + MLA_ROPE]

    qn = _rms(pm[:, :MLA_Q_RANK], qn_ref[...]).astype(BF16)
    q2 = jnp.dot(qn, wq_ref[...], preferred_element_type=F32)
    hw = MLA_HEADS * HEAD_SLOT
    cos_q = jnp.concatenate([cos_s * MLA_SCALE] * MLA_HEADS, axis=-1)
    sin_q = jnp.concatenate([sin_s * MLA_SCALE] * MLA_HEADS, axis=-1)
    q = q2[:, :hw] * cos_q + q2[:, hw:] * sin_q
    if sample:
        q_ref[...] = q
        for h in range(MLA_HEADS):
            qh = q[:, h * HEAD_SLOT:h * HEAD_SLOT + MLA_NOPE]
            qlat_ref[:, h * MLA_KV_RANK:(h + 1) * MLA_KV_RANK] = _dot(qh, wukt_ref[h * MLA_NOPE:(h + 1) * MLA_NOPE, :])
    else:
        q_ref[...] = q.astype(BF16)
        cb = c.astype(BF16)
        kn = jnp.dot(cb, wuk_ref[...], preferred_element_type=F32)
        k_ref[...] = (kn + jnp.concatenate([k_slot] * MLA_HEADS, axis=-1)).astype(BF16)
        v_ref[...] = jnp.dot(cb, wuv_ref[...], preferred_element_type=F32).astype(BF16)


def _proj(x, pw, seq_len, pos0, sample, tm=512):
    n = x.shape[0]
    tm = min(tm, n)
    row = lambda w: pl.BlockSpec((tm, w), lambda i: (i, 0))
    hw = MLA_HEADS * HEAD_SLOT
    args = [x, pw["g"], pw["whg"], pw["wrw"], pw["wmla"], pw["qn"], pw["wq"], pw["kvn"], pw["rope"]]
    args += [pw["wukt"]] if sample else [pw["wuk"], pw["wuv"]]
    specs = [row(D_MODEL)] + [_full(a.shape) for a in args[1:]]
    qdt = F32 if sample else BF16
    outs = [((n, 1024), F32), ((n, RW_COLS), F32), ((n, hw), qdt)]
    outs += [((n, MLA_HEADS * MLA_KV_RANK), F32)] if sample else [((n, hw), BF16), ((n, MLA_HEADS * MLA_VDIM), BF16)]
    outs += [((n, MLA_KV_RANK), F32), ((n, MLA_ROPE), F32)]
    return pl.pallas_call(
        functools.partial(_proj_kernel, seq_len=seq_len, pos0=pos0, sample=sample),
        grid=(n // tm,), in_specs=specs, out_specs=[row(s[1]) for s, _ in outs],
        out_shape=[jax.ShapeDtypeStruct(s, d) for s, d in outs],
        compiler_params=_params(("parallel",)), name="proj")(*args)


MIX_W = 256
HEAD_D = 64


def _chunk_masks(L, C):
    r, c = _iota((L, L), 0), _iota((L, L), 1)
    same = (r // C) == (c // C)
    return same, same & (c <= r), same & (c < r)


def _expand_state(s_kv):
    t = jnp.concatenate([s_kv] * (MIX_W // HEAD_D), axis=0)
    bd = (_iota((MIX_W, MIX_W), 0) // HEAD_D) == (_iota((MIX_W, MIX_W), 1) // HEAD_D)
    return jnp.where(bd, t, 0.0)


def _fold_state(hbd):
    out = hbd[0:HEAD_D]
    for h in range(1, MIX_W // HEAD_D):
        out = out + hbd[h * HEAD_D:(h + 1) * HEAD_D]
    return out


def _hgrn_kernel(*refs, L, C, layer, chained):
    it = iter(refs)
    p_ref, lbl_ref, gn_ref = next(it), next(it), next(it)
    s0_ref = None if chained else next(it)
    o_ref, sout_ref = next(it), next(it)
    h_scr = next(it) if chained else None

    nck = L // C
    p = p_ref[...]
    q, fz, vi, gate = p[:, 0:256], p[:, 256:512], p[:, 512:768], p[:, 768:1024]

    lg = lbl_ref[...]
    e = jnp.exp(lg - jnp.max(lg, axis=0, keepdims=True))
    pr = e / jnp.sum(e, axis=0, keepdims=True)
    cs = pr[0:1]
    for i in range(1, layer + 1):
        cs = cs + pr[i:i + 1]
    lb = jnp.clip(cs - pr[0:1], 0.0, 1.0)

    f = lb + (1.0 - lb) * _sigmoid(fz)
    log_f = jnp.log(jnp.maximum(f, HG_MIN_F))
    k = (1.0 - lb) * _sigmoid(-fz)

    same, incl, _ = _chunk_masks(L, C)
    b = _dot_xl(incl.astype(BF16), log_f, pieces=3)
    b_last = _dot_xl(same.astype(BF16), log_f, pieces=3)
    ones_h = _head_block_ones(MIX_W, HEAD_D)

    b3, q3, k3, v3 = (a.reshape(nck, C, MIX_W) for a in (b, q, k, vi))
    t_idx = _iota((nck, C, MIX_W), 1)
    o_intra = jnp.zeros((nck, C, MIX_W), F32)
    for s in range(C):
        w = jnp.exp(jnp.minimum(b3 - b3[:, s:s + 1, :], 0.0))
        x = jnp.where(t_idx >= s, q3 * w * k3[:, s:s + 1, :], 0.0)
        a_s = _dot_xr(x.reshape(L, MIX_W), ones_h).reshape(nck, C, MIX_W)
        o_intra = o_intra + a_s * v3[:, s:s + 1, :]
    o_intra = o_intra.reshape(L, MIX_W)

    qe = q * jnp.exp(b)
    kd_t = (k * jnp.exp(b_last - b)).T
    e_t = jnp.exp(b_last).T
    bd = (_iota((MIX_W, MIX_W), 0) // HEAD_D) == (_iota((MIX_W, MIX_W), 1) // HEAD_D)
    col = _iota((1, L), 1)

    if chained:
        @pl.when(pl.program_id(1) == 0)
        def _():
            h_scr[...] = jnp.zeros((MIX_W, MIX_W), F32)
        hbd = h_scr[...]
    o_parts = []
    for ck in range(nck):
        if not chained:
            s_kv = jnp.concatenate([s0_ref[ck, h] for h in range(HG_HEADS)], axis=-1)
            hbd = _expand_state(s_kv)
        o_parts.append(_dot3(qe[ck * C:(ck + 1) * C], hbd))
        cm = (col // C) == ck
        gk = jnp.where(bd, _dot3(jnp.where(cm, kd_t, 0.0), vi), 0.0)
        dcol = jnp.sum(jnp.where(col == ck * C, e_t, 0.0), axis=1, keepdims=True)
        hbd = dcol * hbd + gk
        if not chained:
            s_new = _fold_state(hbd)
            for h in range(HG_HEADS):
                sout_ref[ck, h] = s_new[:, h * HEAD_D:(h + 1) * HEAD_D]
    if chained:
        h_scr[...] = hbd

        @pl.when(pl.program_id(1) == pl.num_programs(1) - 1)
        def _():
            s_new = _fold_state(hbd)
            for h in range(HG_HEADS):
                sout_ref[h] = s_new[:, h * HEAD_D:(h + 1) * HEAD_D]

    o = o_intra + jnp.concatenate(o_parts, axis=0)
    ms = _dot_xr(o * o, ones_h) * (1.0 / HEAD_D)
    on = o * lax.rsqrt(ms + EPS) * gn_ref[...]
    o_ref[...] = (on * (gate * _sigmoid(gate))).astype(o_ref.dtype)


def _hgrn(p_hg, lb_logits, gn, s0, layer, B, T, chained):
    n = p_hg.shape[0]
    if chained:
        L, C = min(256, T), min(16, T)
        nblk = T // L
        grid = (B, nblk)
        rowmap = lambda b, j: (b * nblk + j, 0)
        in_specs = [pl.BlockSpec((L, 1024), rowmap), _full(lb_logits.shape), _full(gn.shape)]
        args = [p_hg, lb_logits, gn]
        out_specs = [pl.BlockSpec((L, MIX_W), rowmap),
                     pl.BlockSpec((None, HG_HEADS, HEAD_D, HEAD_D), lambda b, j: (b, 0, 0, 0))]
        scratch = [pltpu.VMEM((MIX_W, MIX_W), F32)]
        sem = ("parallel", "arbitrary")
    else:
        L, C = n, T
        grid = (1,)
        in_specs = [_full((L, 1024)), _full(lb_logits.shape), _full(gn.shape), _full(s0.shape)]
        args = [p_hg, lb_logits, gn, s0]
        out_specs = [_full((L, MIX_W)), _full((B, HG_HEADS, HEAD_D, HEAD_D))]
        scratch = []
        sem = ("arbitrary",)
    return pl.pallas_call(
        functools.partial(_hgrn_kernel, L=L, C=C, layer=layer, chained=chained),
        grid=grid, in_specs=in_specs, out_specs=out_specs,
        out_shape=[jax.ShapeDtypeStruct((n, MIX_W), BF16),
                   jax.ShapeDtypeStruct((B, HG_HEADS, HEAD_D, HEAD_D), F32)],
        scratch_shapes=scratch, compiler_params=_params(sem), name="hgrn")(*args)


def _rwkv_kernel(*refs, L, C, chained):
    it = iter(refs)
    p_ref = next(it)
    if not chained:
        shrow_ref, s0_ref = next(it), next(it)
    (mu_ref, w0_ref, w2_ref, a0_ref, a2_ref, g2_ref, kkw_ref, ka_ref, rk_ref, lnw_ref, lnb_ref) = (next(it) for _ in range(11))
    y_ref, sout_ref = next(it), next(it)
    if chained:
        h_scr, carry_scr = next(it), next(it)

    nck = L // C
    p = p_ref[...]
    rowi = _iota((L, 1), 0)
    rolled = pltpu.roll(p, 1, 0)
    if chained:
        first = pl.program_id(1) == 0

        @pl.when(first)
        def _():
            h_scr[...] = jnp.zeros((MIX_W, MIX_W), F32)
            carry_scr[...] = jnp.zeros(carry_scr.shape, F32)
        prev = jnp.where(rowi == 0, carry_scr[0:1, :], rolled)
        carry_scr[0:1, :] = p[L - 1:L, :]
    else:
        prev = jnp.where((rowi & (C - 1)) == 0, shrow_ref[...], rolled)
    ps = p + (prev - p) * mu_ref[...]
    r, k, v, tail = ps[:, 0:256], ps[:, 256:512], ps[:, 512:768], ps[:, 768:896]

    w_raw = -jax.nn.softplus(-(w0_ref[...] + _dot(jnp.tanh(tail), w2_ref[...]))) - 0.5
    lw = -jnp.exp(w_raw)
    a = _sigmoid(a0_ref[...] + _dot(tail, a2_ref[...]))
    g = _dot(_sigmoid(tail), g2_ref[...])

    ones_h = _head_block_ones(MIX_W, HEAD_D)
    kk = k * kkw_ref[...]
    kk = kk / jnp.maximum(jnp.sqrt(_dot_xr(kk * kk, ones_h)), 1e-12)
    k2 = k * (1.0 + (a - 1.0) * ka_ref[...])
    bb = kk * a

    same, incl, strict = _chunk_masks(L, C)
    gc = _dot_xl(incl.astype(BF16), lw)
    g_last = _dot_xl(same.astype(BF16), lw)
    eg, eng = jnp.exp(gc), jnp.exp(-gc)
    rg, kg, bg, kkg = r * eg, k2 * eng, bb * eng, kk * jnp.exp(gc - lw)
    to_end = jnp.exp(g_last - gc)
    kd_t, bd_t, e_t = (k2 * to_end).T, (bb * to_end).T, jnp.exp(g_last).T

    lane_h = _iota((1, MIX_W), 1) // HEAD_D
    eye = (_iota((L, L), 0) == _iota((L, L), 1)).astype(F32)
    u0 = jnp.zeros((L, MIX_W), F32)
    kkt = jnp.zeros((L, MIX_W), F32)
    qp = rg
    o0 = jnp.zeros((L, MIX_W), F32)
    for h in range(RW_HEADS):
        mh = lane_h == h
        kkg_h, rg_h = jnp.where(mh, kkg, 0.0), jnp.where(mh, rg, 0.0)
        a_kb = jnp.where(strict, _dot3_nt(kkg_h, bg), 0.0)
        a_kk = jnp.where(strict, _dot3_nt(kkg_h, kg), 0.0)
        a_rk = jnp.where(incl, _dot3_nt(rg_h, kg), 0.0)
        a_rb = jnp.where(incl, _dot3_nt(rg_h, bg), 0.0)
        npow = -a_kb
        tinv = eye + npow
        for _ in range(int(math.log2(C)) - 1):
            npow = _dot3(npow, npow)
            tinv = tinv + _dot3(tinv, npow)
        u0_h = _dot3(tinv, _dot3(a_kk, v))
        kkt_h = _dot3(tinv, kkg)
        u0 = jnp.where(mh, u0_h, u0)
        kkt = jnp.where(mh, kkt_h, kkt)
        qp = qp - jnp.where(mh, _dot3(a_rb, kkt_h), 0.0)
        o0 = o0 + jnp.where(mh, _dot3(a_rk, v) - _dot3(a_rb, u0_h), 0.0)

    bd = (_iota((MIX_W, MIX_W), 0) // HEAD_D) == (_iota((MIX_W, MIX_W), 1) // HEAD_D)
    col = _iota((1, L), 1)
    if chained:
        hbd = h_scr[...]
    y_parts = []
    for ck in range(nck):
        if not chained:
            hbd = _expand_state(s0_ref[ck].T)
        y_parts.append(_dot3(qp[ck * C:(ck + 1) * C], hbd))
        cm = (col // C) == ck
        bd_c = jnp.where(cm, bd_t, 0.0)
        mb = jnp.where(bd, _dot3(bd_c, kkt), 0.0)
        gb = jnp.where(bd, _dot3(jnp.where(cm, kd_t, 0.0), v) - _dot3(bd_c, u0), 0.0)
        dcol = jnp.sum(jnp.where(col == ck * C, e_t, 0.0), axis=1, keepdims=True)
        hbd = dcol * hbd - _dot3(mb, hbd) + gb
        if not chained:
            sout_ref[ck] = _fold_state(hbd).T
    if chained:
        h_scr[...] = hbd

        @pl.when(pl.program_id(1) == pl.num_programs(1) - 1)
        def _():
            sout_ref[...] = _fold_state(hbd).T

    y = o0 + jnp.concatenate(y_parts, axis=0)
    mean = _dot_xr(y, ones_h) * (1.0 / HEAD_D)
    yc = y - mean
    var = _dot_xr(yc * yc, ones_h) * (1.0 / HEAD_D)
    yn = yc * lax.rsqrt(var + RW_GN_EPS) * lnw_ref[...] + lnb_ref[...]
    bonus = _dot_xr(r * k2 * rk_ref[...], ones_h) * v
    y_ref[...] = ((yn + bonus) * g).astype(y_ref.dtype)


def _rwkv(p_rw, shift_rows, s0, rw, B, T, chained):
    n = p_rw.shape[0]
    names = ("mu", "w0", "w2", "a0", "a2", "g2", "kk", "ka", "rk", "lnw", "lnb")
    wargs = [rw[k] for k in names]
    wspecs = [_full(a.shape) for a in wargs]
    if chained:
        L, C = min(256, T), min(64, T)
        nblk = T // L
        grid = (B, nblk)
        rowmap = lambda b, j: (b * nblk + j, 0)
        in_specs = [pl.BlockSpec((L, RW_COLS), rowmap)] + wspecs
        args = [p_rw] + wargs
        out_specs = [pl.BlockSpec((L, MIX_W), rowmap), pl.BlockSpec((None, MIX_W, HEAD_D), lambda b, j: (b, 0, 0))]
        scratch = [pltpu.VMEM((MIX_W, MIX_W), F32), pltpu.VMEM((8, RW_COLS), F32)]
        sem = ("parallel", "arbitrary")
    else:
        L, C = n, T
        grid = (1,)
        in_specs = [_full((L, RW_COLS)), _full(shift_rows.shape), _full(s0.shape)] + wspecs
        args = [p_rw, shift_rows, s0] + wargs
        out_specs = [_full((L, MIX_W)), _full((B, MIX_W, HEAD_D))]
        scratch = []
        sem = ("arbitrary",)
    return pl.pallas_call(
        functools.partial(_rwkv_kernel, L=L, C=C, chained=chained),
        grid=grid, in_specs=in_specs, out_specs=out_specs,
        out_shape=[jax.ShapeDtypeStruct((n, MIX_W), BF16), jax.ShapeDtypeStruct((B, MIX_W, HEAD_D), F32)],
        scratch_shapes=scratch, compiler_params=_params(sem), name="rwkv")(*args)


def _attn_prompt_kernel(q_ref, k_ref, v_ref, o_ref, *, tq):
    i = pl.program_id(2)
    rows = i * tq + _iota((tq, tq), 0)
    outs = []
    for hh in range(2):
        q = q_ref[:, hh * HEAD_SLOT:(hh + 1) * HEAD_SLOT]

        def body(j, carry, hh=hh, q=q):
            m, l, acc = carry
            kb = k_ref[pl.ds(j * tq, tq), hh * HEAD_SLOT:(hh + 1) * HEAD_SLOT]
            s = lax.dot_general(q, kb, (((1,), (1,)), ((), ())), preferred_element_type=F32)
            s = jnp.where(j * tq + _iota((tq, tq), 1) <= rows, s, NEG_INF)
            m_new = jnp.maximum(m, jnp.max(s, axis=-1, keepdims=True))
            pexp = jnp.exp(s - m_new)
            alpha = jnp.exp(m - m_new)
            l = alpha * l + jnp.sum(pexp, axis=-1, keepdims=True)
            acc = alpha * acc + jnp.dot(pexp.astype(BF16), v_ref[pl.ds(j * tq, tq), :], preferred_element_type=F32)
            return m_new, l, acc

        m0 = jnp.full((tq, 1), NEG_INF, F32)
        m, l, acc = lax.fori_loop(0, i + 1, body, (m0, jnp.zeros((tq, 1), F32), jnp.zeros((tq, 2 * MLA_VDIM), F32)))
        outs.append(acc / l)
    lane = _iota((tq, 2 * MLA_VDIM), 1)
    o_ref[...] = jnp.where(lane < MLA_VDIM, outs[0], outs[1]).astype(o_ref.dtype)


def _attn_prompt(q, k, v, B, T, tq=512):
    n = q.shape[0]
    tq = min(tq, T)
    nq = T // tq
    return pl.pallas_call(
        functools.partial(_attn_prompt_kernel, tq=tq),
        grid=(B, MLA_HEADS // 2, nq),
        in_specs=[pl.BlockSpec((tq, 2 * HEAD_SLOT), lambda b, hp, i: (b * nq + i, hp)),
                  pl.BlockSpec((T, 2 * HEAD_SLOT), lambda b, hp, i: (b, hp)),
                  pl.BlockSpec((T, 2 * MLA_VDIM), lambda b, hp, i: (b, hp))],
        out_specs=pl.BlockSpec((tq, 2 * MLA_VDIM), lambda b, hp, i: (b * nq + i, hp)),
        out_shape=jax.ShapeDtypeStruct((n, MLA_HEADS * MLA_VDIM), BF16),
        compiler_params=_params(("parallel", "parallel", "arbitrary")), name="attn_prompt")(q, k, v)


PAGES_PER_STEP = 8


def _attn_sample_kernel(pt_ref, *refs, T, npp):
    it = iter(refs)
    qlat_ref, q_ref, c_ref, kpe_ref, wuv_ref = (next(it) for _ in range(5))
    ckv_refs = [next(it) for _ in range(npp)]
    kpp_refs = [next(it) for _ in range(npp)]
    o_ref = next(it)
    ql_scr, qp_scr, m_scr, l_scr, acc_scr = (next(it) for _ in range(5))
    j = pl.program_id(1)
    R = MLA_HEADS * T

    @pl.when(j == 0)
    def _():
        ql = jnp.concatenate([qlat_ref[:, h * MLA_KV_RANK:(h + 1) * MLA_KV_RANK] for h in range(MLA_HEADS)], axis=0)
        qs = jnp.concatenate([q_ref[:, h * HEAD_SLOT:(h + 1) * HEAD_SLOT] for h in range(MLA_HEADS)], axis=0)
        ql_scr[...] = ql
        qp_scr[...] = qs[:, ROPE_LANE0:ROPE_LANE0 + MLA_ROPE]
        m_scr[...] = jnp.full(m_scr.shape, NEG_INF, F32)
        l_scr[...] = jnp.zeros(l_scr.shape, F32)
        acc_scr[...] = jnp.zeros(acc_scr.shape, F32)

    ql = ql_scr[...].astype(BF16)
    qp = qp_scr[...].astype(BF16)

    def scores(cb, kb):
        d = functools.partial(lax.dot_general, dimension_numbers=(((1,), (1,)), ((), ())), preferred_element_type=F32)
        return d(ql, cb) + d(qp, kb)

    def update(s, cbs):
        m = m_scr[...]
        m_new = jnp.maximum(m, jnp.max(s, axis=-1, keepdims=True))
        pexp = jnp.exp(s - m_new)
        alpha = jnp.exp(m - m_new)
        l_scr[...] = alpha * l_scr[...] + jnp.sum(pexp, axis=-1, keepdims=True)
        acc = alpha * acc_scr[...]
        for i, cb in enumerate(cbs):
            n = cb.shape[0]
            acc = acc + jnp.dot(pexp[:, i * n:(i + 1) * n].astype(BF16), cb, preferred_element_type=F32)
        acc_scr[...] = acc
        m_scr[...] = m_new

    cbs = [r[...].astype(BF16) for r in ckv_refs]
    kbs = [r[...].astype(BF16) for r in kpp_refs]
    update(jnp.concatenate([scores(cb, kb) for cb, kb in zip(cbs, kbs)], axis=-1), cbs)

    @pl.when(j == pl.num_programs(1) - 1)
    def _():
        pad = PAGE_SIZE - T
        cn = jnp.concatenate([c_ref[...], jnp.zeros((pad, MLA_KV_RANK), F32)], axis=0).astype(BF16)
        kn = jnp.concatenate([kpe_ref[...], jnp.zeros((pad, MLA_ROPE), F32)], axis=0).astype(BF16)
        tq = _iota((R, PAGE_SIZE), 0) & (T - 1)
        s = jnp.where(_iota((R, PAGE_SIZE), 1) <= tq, scores(cn, kn), NEG_INF)
        update(s, [cn])
        o_lat = acc_scr[...] / l_scr[...]
        z = _dot(o_lat, wuv_ref[...])
        lane_h = _iota((T, MLA_HEADS * MLA_VDIM), 1) // MLA_VDIM
        out = jnp.zeros((T, MLA_HEADS * MLA_VDIM), F32)
        for h in range(MLA_HEADS):
            out = jnp.where(lane_h == h, z[h * T:(h + 1) * T], out)
        o_ref[...] = out.astype(o_ref.dtype)


def _attn_sample(qlat, q, c, kpe, wuv, cache_ckv, cache_kpe, page_table, layer, B, T):
    n = qlat.shape[0]
    n_pages = page_table.shape[1]
    npp = PAGES_PER_STEP
    R = MLA_HEADS * T
    row = lambda w: pl.BlockSpec((T, w), lambda b, j, pt: (b, 0))

    def page_spec(width, i):
        return pl.BlockSpec((None, None, PAGE_SIZE, width), lambda b, j, pt: (layer, pt[b, j * npp + i], 0, 0))

    in_specs = [row(qlat.shape[1]), row(q.shape[1]), row(MLA_KV_RANK), row(MLA_ROPE),
                pl.BlockSpec(wuv.shape, lambda b, j, pt: (0, 0))]
    in_specs += [page_spec(MLA_KV_RANK, i) for i in range(npp)] + [page_spec(MLA_ROPE, i) for i in range(npp)]
    grid_spec = pltpu.PrefetchScalarGridSpec(
        num_scalar_prefetch=1, grid=(B, n_pages // npp), in_specs=in_specs,
        out_specs=row(MLA_HEADS * MLA_VDIM),
        scratch_shapes=[pltpu.VMEM((R, MLA_KV_RANK), F32), pltpu.VMEM((R, MLA_ROPE), F32),
                        pltpu.VMEM((R, 1), F32), pltpu.VMEM((R, 1), F32), pltpu.VMEM((R, MLA_KV_RANK), F32)])
    return pl.pallas_call(
        functools.partial(_attn_sample_kernel, T=T, npp=npp), grid_spec=grid_spec,
        out_shape=jax.ShapeDtypeStruct((n, MLA_HEADS * MLA_VDIM), F32),
        compiler_params=_params(("parallel", "arbitrary")), name="attn_sample")(
            page_table, qlat, q, c, kpe, wuv, *([cache_ckv] * npp), *([cache_kpe] * npp))


def _rope_rows():
    half = MLA_ROPE // 2
    inv = jnp.exp(-jnp.log(ROPE_THETA) * jnp.arange(half, dtype=F32) / half)
    lane = np.arange(LANES)
    on = (lane >= ROPE_LANE0) & (lane < ROPE_LANE0 + MLA_ROPE)
    inv_l = jnp.where(on, inv[(lane - ROPE_LANE0) % half], 0.0)
    sign = np.where(on, np.where(lane < ROPE_LANE0 + half, -1.0, 1.0), 0.0)
    rows = [inv_l, (lane < ROPE_LANE0).astype(np.float32), sign, on.astype(np.float32)]
    rows += [np.zeros(LANES, np.float32)] * 4
    return jnp.stack([jnp.asarray(r, F32) for r in rows])


def _layer_weights(l, W):
    half = MLA_ROPE // 2
    r2 = lambda a: a.reshape(1, -1)
    w_in = W["w_in"][l]
    whg, wrw, wmla = w_in[:, :1024], w_in[:, 1024:1920], w_in[:, 1920:]
    wkpe = wmla[:, MLA_Q_RANK + MLA_KV_RANK:]
    z = lambda n: jnp.zeros((D_MODEL, n), F32)
    wmla_pad = jnp.concatenate([wmla[:, :MLA_Q_RANK + MLA_KV_RANK], z(ROPE_LANE0), wkpe, z(HEAD_SLOT - ROPE_LANE0 - MLA_ROPE),
                                z(ROPE_LANE0), wkpe[:, half:], wkpe[:, :half], z(HEAD_SLOT - ROPE_LANE0 - MLA_ROPE)], axis=1)
    wqb = W["mla_wqb"][l].reshape(MLA_Q_RANK, MLA_HEADS, MLA_NOPE + MLA_ROPE)
    zq = lambda n: jnp.zeros((MLA_Q_RANK, MLA_HEADS, n), F32)
    rest = HEAD_SLOT - MLA_NOPE - MLA_ROPE
    wq_pad = jnp.concatenate([wqb, zq(rest)], axis=-1)
    wq_rot = jnp.concatenate([zq(MLA_NOPE), wqb[..., MLA_NOPE + half:], wqb[..., MLA_NOPE:MLA_NOPE + half], zq(rest)], axis=-1)
    hw = MLA_HEADS * HEAD_SLOT
    wq = jnp.concatenate([wq_pad.reshape(MLA_Q_RANK, hw), wq_rot.reshape(MLA_Q_RANK, hw)], axis=1)
    wuk = W["mla_wuk"][l]
    wuk_pad = jnp.concatenate([wuk, jnp.zeros((MLA_KV_RANK, MLA_HEADS, HEAD_SLOT - MLA_NOPE), F32)], axis=-1).reshape(MLA_KV_RANK, hw)
    wukt = wuk.transpose(1, 2, 0).reshape(MLA_HEADS * MLA_NOPE, MLA_KV_RANK)
    wuv = W["mla_wuv"][l].reshape(MLA_KV_RANK, MLA_HEADS * MLA_VDIM)
    proj = dict(g=r2(W["norm_mix"][l]), whg=whg.astype(BF16), wrw=wrw.astype(BF16), wmla=wmla_pad.astype(BF16),
                qn=r2(W["mla_q_norm"][l]), wq=wq.astype(BF16), kvn=r2(W["mla_kv_norm"][l]), rope=_rope_rows(),
                wuk=wuk_pad.astype(BF16), wuv=wuv.astype(BF16), wukt=wukt.astype(BF16))
    zr = lambda n: jnp.zeros((n, RW_WIDTH), F32)
    rwk = dict(mu=r2(W["rw_mu"][l]), w0=r2(W["rw_w0"][l]),
               w2=jnp.concatenate([W["rw_w2"][l], zr(96)], axis=0).astype(BF16),
               a0=r2(W["rw_a0"][l]),
               a2=jnp.concatenate([zr(32), W["rw_a2"][l], zr(64)], axis=0).astype(BF16),
               g2=jnp.concatenate([zr(64), W["rw_g2"][l]], axis=0).astype(BF16),
               kk=r2(W["rw_kk"][l]), ka=r2(W["rw_ka"][l]), rk=r2(W["rw_rk"][l]),
               lnw=r2(W["rw_ln_w"][l]), lnb=r2(W["rw_ln_b"][l]))
    return dict(
        proj=proj, rwkv=rwk,
        ffn1=(r2(W["norm_ffn1"][l]), W["ffn1_wi"][l].astype(BF16), W["ffn1_wo"][l].astype(BF16)),
        ffn2=(r2(W["norm_ffn2"][l]), W["ffn2_wi"][l].astype(BF16), W["ffn2_wo"][l].astype(BF16)),
        wout=W["w_out"][l].astype(BF16), hgn=r2(jnp.tile(W["hg_norm"][l], HG_HEADS)))


def _trunk(x, B, T, pos0, LW, W, sample, st=None):
    n = B * T
    x = x.reshape(n, D_MODEL)
    hg_out, rw_out, sh_out, c_out, kpe_out = [], [], [], [], []
    for l in range(DEPTH):
        lw = LW[l]
        x = _ffn(x, *lw["ffn1"])
        if sample:
            p_hg, p_rw, q, qlat, c, kpe = _proj(x, lw["proj"], T, pos0, True)
            o_hg, hg_s = _hgrn(p_hg, W["hg_lb_logits"], lw["hgn"], st["hg"][l], l, B, T, False)
            shift_rows = jnp.repeat(st["sh"][l], T, axis=0)
            o_rw, rw_s = _rwkv(p_rw, shift_rows, st["rw"][l].reshape(B, MIX_W, HEAD_D), lw["rwkv"], B, T, False)
            o_mla = _attn_sample(qlat, q, c, kpe, lw["proj"]["wuv"], st["ckv"], st["kpe"], st["pt"], l, B, T)
        else:
            p_hg, p_rw, q, k, v, c, kpe = _proj(x, lw["proj"], T, pos0, False)
            o_hg, hg_s = _hgrn(p_hg, W["hg_lb_logits"], lw["hgn"], None, l, B, T, True)
            o_rw, rw_s = _rwkv(p_rw, None, None, lw["rwkv"], B, T, True)
            o_mla = _attn_prompt(q, k, v, B, T)
        gf = W["norm_final"].reshape(1, -1) if l == DEPTH - 1 else None
        x = _ffn(x, *lw["ffn2"], mix=(o_hg, o_rw, o_mla), wout=lw["wout"], gf=gf)
        hg_out.append(hg_s)
        rw_out.append(rw_s.reshape(B, RW_HEADS, RW_DH, RW_DH))
        sh_out.append(p_rw.reshape(B, T, RW_COLS)[:, -1])
        c_out.append(c.reshape(B, T, MLA_KV_RANK))
        kpe_out.append(kpe.reshape(B, T, MLA_ROPE))
    st_ = jnp.stack
    return x.reshape(B, T, D_MODEL), st_(hg_out), st_(rw_out), st_(sh_out), st_(c_out), st_(kpe_out)


def kernel(x_prompt, x_sample, cache_mla_ckv, cache_mla_kpe, state_hgrn, state_rwkv, state_rwkv_shift, page_table, norm_ffn1, ffn1_wi, ffn1_wo, norm_mix, w_in, w_out, hg_lb_logits, hg_norm, rw_mu, rw_w0, rw_w2, rw_a0, rw_a2, rw_g2, rw_kk, rw_ka, rw_rk, rw_ln_w, rw_ln_b, mla_q_norm, mla_wqb, mla_kv_norm, mla_wuk, mla_wuv, norm_ffn2, ffn2_wi, ffn2_wo, norm_final):
    W = dict(norm_ffn1=norm_ffn1, ffn1_wi=ffn1_wi, ffn1_wo=ffn1_wo, norm_mix=norm_mix, w_in=w_in, w_out=w_out,
             hg_lb_logits=hg_lb_logits, hg_norm=hg_norm, rw_mu=rw_mu, rw_w0=rw_w0, rw_w2=rw_w2, rw_a0=rw_a0,
             rw_a2=rw_a2, rw_g2=rw_g2, rw_kk=rw_kk, rw_ka=rw_ka, rw_rk=rw_rk, rw_ln_w=rw_ln_w, rw_ln_b=rw_ln_b,
             mla_q_norm=mla_q_norm, mla_wqb=mla_wqb, mla_kv_norm=mla_kv_norm, mla_wuk=mla_wuk, mla_wuv=mla_wuv,
             norm_ffn2=norm_ffn2, ffn2_wi=ffn2_wi, ffn2_wo=ffn2_wo, norm_final=norm_final)
    LW = [_layer_weights(l, W) for l in range(DEPTH)]
    Bp, Tp, _ = x_prompt.shape
    y_p, hg_p, rw_p, sh_p, ckv_p, kpe_p = _trunk(x_prompt, Bp, Tp, 0, LW, W, False)
    Bs, Ts, _ = x_sample.shape
    past_len = page_table.shape[1] * cache_mla_ckv.shape[2]
    st = dict(hg=state_hgrn, rw=state_rwkv, sh=state_rwkv_shift, ckv=cache_mla_ckv, kpe=cache_mla_kpe, pt=page_table)
    y_s, hg_s, rw_s, sh_s, ckv_s, kpe_s = _trunk(x_sample, Bs, Ts, past_len, LW, W, True, st)
    return (y_p, y_s, hg_p, hg_s, rw_p, rw_s, sh_p, sh_s, ckv_p, ckv_s, kpe_p, kpe_s)
```

```python
import functools
import math

import jax
import jax.numpy as jnp
import numpy as np
from jax import lax
from jax.experimental import pallas as pl
from jax.experimental.pallas import tpu as pltpu

F32 = jnp.float32
BF16 = jnp.bfloat16

D_MODEL = 1024
DEPTH = 2
PAGE_SIZE = 128
EPS = 1e-6
D_FF = 2816
HG_HEADS = 4
HG_DK = 64
HG_KW = 256
HG_MIN_F = 1e-30
RW_HEADS = 4
RW_DH = 64
RW_WIDTH = 256
RW_COLS = 896
RW_GN_EPS = 64e-5
MLA_HEADS = 8
MLA_NOPE = 64
MLA_ROPE = 32
MLA_VDIM = 64
MLA_Q_RANK = 384
MLA_KV_RANK = 256
MLA_SCALE = (MLA_NOPE + MLA_ROPE) ** -0.5
ROPE_THETA = 10000.0

LANES = 128
HEAD_SLOT = 128
ROPE_LANE0 = 64
VMEM_LIMIT = 56 * 1024 * 1024
NEG_INF = -1e30


def _dot(a, b):
    return jnp.dot(a.astype(BF16), b.astype(BF16), preferred_element_type=F32)


def _dot_nt(a, b):
    return lax.dot_general(a.astype(BF16), b.astype(BF16), (((1,), (1,)), ((), ())), preferred_element_type=F32)


def _dot_xl(m, b, pieces=2):
    out = None
    rem = b
    for _ in range(pieces):
        part = rem.astype(BF16)
        term = jnp.dot(m, part, preferred_element_type=F32)
        out = term if out is None else out + term
        rem = rem - part.astype(F32)
    return out


def _dot_xr(a, m, pieces=2):
    out = None
    rem = a
    for _ in range(pieces):
        part = rem.astype(BF16)
        term = jnp.dot(part, m, preferred_element_type=F32)
        out = term if out is None else out + term
        rem = rem - part.astype(F32)
    return out


def _rms(x, g, eps=EPS):
    return x * lax.rsqrt(jnp.mean(x * x, axis=-1, keepdims=True) + eps) * g


def _sigmoid(x):
    return 1.0 / (1.0 + jnp.exp(-x))


def _iota(shape, dim):
    return lax.broadcasted_iota(jnp.int32, shape, dim)


def _head_block_ones(n, hd):
    return (_iota((n, n), 0) // hd == _iota((n, n), 1) // hd).astype(BF16)


def _full(shape):
    return pl.BlockSpec(shape, lambda *_: (0,) * len(shape), pipeline_mode=pl.Buffered(1))


def _params(sem):
    return pltpu.CompilerParams(dimension_semantics=sem, vmem_limit_bytes=VMEM_LIMIT)


FF_CHUNK = 256


def _ffn_kernel(*refs, has_mix, final):
    it = iter(refs)
    x_ref = next(it)
    if has_mix:
        ohg_ref, orw_ref, omla_ref, wout_ref = next(it), next(it), next(it), next(it)
    g_ref, wi_ref, wo_ref = next(it), next(it), next(it)
    gf_ref = next(it) if final else None
    o_ref = next(it)

    x = x_ref[...]
    if has_mix:
        o = jnp.concatenate([r[...].astype(BF16) for r in (ohg_ref, orw_ref, omla_ref)], axis=-1)
        x = x + jnp.dot(o, wout_ref[...], preferred_element_type=F32)
    xn = _rms(x, g_ref[...]).astype(BF16)
    acc = jnp.zeros(x.shape, F32)
    for c in range(D_FF // FF_CHUNK):
        lo = c * FF_CHUNK
        a = jnp.dot(xn, wi_ref[:, lo:lo + FF_CHUNK], preferred_element_type=F32)
        b = jnp.dot(xn, wi_ref[:, D_FF + lo:D_FF + lo + FF_CHUNK], preferred_element_type=F32)
        h = (a * _sigmoid(a) * b).astype(BF16)
        acc = acc + jnp.dot(h, wo_ref[lo:lo + FF_CHUNK, :], preferred_element_type=F32)
    y = x + 0.5 * acc
    if final:
        y = _rms(y, gf_ref[...])
    o_ref[...] = y


def _ffn(x, g, wi, wo, mix=None, wout=None, gf=None, tm=512):
    n = x.shape[0]
    tm = min(tm, n)
    row = lambda w: pl.BlockSpec((tm, w), lambda i: (i, 0))
    args, specs = [x], [row(D_MODEL)]
    if mix is not None:
        for m in mix:
            args.append(m)
            specs.append(row(m.shape[1]))
        args.append(wout)
        specs.append(_full(wout.shape))
    args += [g, wi, wo]
    specs += [_full(g.shape), _full(wi.shape), _full(wo.shape)]
    if gf is not None:
        args.append(gf)
        specs.append(_full(gf.shape))
    return pl.pallas_call(
        functools.partial(_ffn_kernel, has_mix=mix is not None, final=gf is not None),
        grid=(n // tm,), in_specs=specs, out_specs=row(D_MODEL),
        out_shape=jax.ShapeDtypeStruct((n, D_MODEL), F32),
        compiler_params=_params(("parallel",)), name="ffn")(*args)


MLA_PAD = 896


def _proj_kernel(*refs, seq_len, pos0, sample):
    it = iter(refs)
    x_ref, g_ref, whg_ref, wrw_ref, wmla_ref = (next(it) for _ in range(5))
    qn_ref, wq_ref, kvn_ref, rope_ref = (next(it) for _ in range(4))
    if sample:
        wukt_ref = next(it)
    else:
        wuk_ref, wuv_ref = next(it), next(it)
    phg_ref, prw_ref, q_ref = next(it), next(it), next(it)
    if sample:
        qlat_ref = next(it)
    else:
        k_ref, v_ref = next(it), next(it)
    c_ref, kpe_ref = next(it), next(it)

    tm = x_ref.shape[0]
    xn = _rms(x_ref[...], g_ref[...]).astype(BF16)
    phg_ref[...] = jnp.dot(xn, whg_ref[...], preferred_element_type=F32)
    prw_ref[...] = jnp.dot(xn, wrw_ref[...], preferred_element_type=F32)
    pm = jnp.dot(xn, wmla_ref[...], preferred_element_type=F32)

    row = pl.program_id(0) * tm + _iota((tm, 1), 0)
    pos = (pos0 + (row & (seq_len - 1))).astype(F32)
    rope = rope_ref[...]
    ang = pos * rope[0:1, :]
    cos_s = rope[1:2, :] + rope[3:4, :] * jnp.cos(ang)
    sin_s = rope[2:3, :] * jnp.sin(ang)

    c = _rms(pm[:, MLA_Q_RANK:MLA_Q_RANK + MLA_KV_RANK], kvn_ref[...])
    c_ref[...] = c
    k_slot = pm[:, 640:768] * cos_s + pm[:, 768:896] * sin_s
    kpe_ref[...] = k_slot[:, ROPE_LANE0:ROPE_LANE0 + MLA_ROPE]

    qn = _rms(pm[:, :MLA_Q_RANK], qn_ref[...]).astype(BF16)
    q2 = jnp.dot(qn, wq_ref[...], preferred_element_type=F32)
    hw = MLA_HEADS * HEAD_SLOT
    cos_q = jnp.concatenate([cos_s * MLA_SCALE] * MLA_HEADS, axis=-1)
    sin_q = jnp.concatenate([sin_s * MLA_SCALE] * MLA_HEADS, axis=-1)
    q = q2[:, :hw] * cos_q + q2[:, hw:] * sin_q
    if sample:
        q_ref[...] = q
        for h in range(MLA_HEADS):
            qh = q[:, h * HEAD_SLOT:h * HEAD_SLOT + MLA_NOPE]
            qlat_ref[:, h * MLA_KV_RANK:(h + 1) * MLA_KV_RANK] = _dot(qh, wukt_ref[h * MLA_NOPE:(h + 1) * MLA_NOPE, :])
    else:
        q_ref[...] = q.astype(BF16)
        cb = c.astype(BF16)
        kn = jnp.dot(cb, wuk_ref[...], preferred_element_type=F32)
        k_ref[...] = (kn + jnp.concatenate([k_slot] * MLA_HEADS, axis=-1)).astype(BF16)
        v_ref[...] = jnp.dot(cb, wuv_ref[...], preferred_element_type=F32).astype(BF16)


def _proj(x, pw, seq_len, pos0, sample, tm=512):
    n = x.shape[0]
    tm = min(tm, n)
    row = lambda w: pl.BlockSpec((tm, w), lambda i: (i, 0))
    hw = MLA_HEADS * HEAD_SLOT
    args = [x, pw["g"], pw["whg"], pw["wrw"], pw["wmla"], pw["qn"], pw["wq"], pw["kvn"], pw["rope"]]
    args += [pw["wukt"]] if sample else [pw["wuk"], pw["wuv"]]
    specs = [row(D_MODEL)] + [_full(a.shape) for a in args[1:]]
    qdt = F32 if sample else BF16
    outs = [((n, 1024), F32), ((n, RW_COLS), F32), ((n, hw), qdt)]
    outs += [((n, MLA_HEADS * MLA_KV_RANK), F32)] if sample else [((n, hw), BF16), ((n, MLA_HEADS * MLA_VDIM), BF16)]
    outs += [((n, MLA_KV_RANK), F32), ((n, MLA_ROPE), F32)]
    return pl.pallas_call(
        functools.partial(_proj_kernel, seq_len=seq_len, pos0=pos0, sample=sample),
        grid=(n // tm,), in_specs=specs, out_specs=[row(s[1]) for s, _ in outs],
        out_shape=[jax.ShapeDtypeStruct(s, d) for s, d in outs],
        compiler_params=_params(("parallel",)), name="proj")(*args)


MIX_W = 256
HEAD_D = 64


def _chunk_masks(L, C):
    r, c = _iota((L, L), 0), _iota((L, L), 1)
    same = (r // C) == (c // C)
    return same, same & (c <= r), same & (c < r)


def _expand_state(s_kv):
    t = jnp.concatenate([s_kv] * (MIX_W // HEAD_D), axis=0)
    bd = (_iota((MIX_W, MIX_W), 0) // HEAD_D) == (_iota((MIX_W, MIX_W), 1) // HEAD_D)
    return jnp.where(bd, t, 0.0)


def _fold_state(hbd):
    out = hbd[0:HEAD_D]
    for h in range(1, MIX_W // HEAD_D):
        out = out + hbd[h * HEAD_D:(h + 1) * HEAD_D]
    return out


def _hgrn_kernel(*refs, L, C, layer, chained):
    it = iter(refs)
    p_ref, lbl_ref, gn_ref = next(it), next(it), next(it)
    s0_ref = None if chained else next(it)
    o_ref, sout_ref = next(it), next(it)
    h_scr = next(it) if chained else None

    nck = L // C
    p = p_ref[...]
    q, fz, vi, gate = p[:, 0:256], p[:, 256:512], p[:, 512:768], p[:, 768:1024]

    lg = lbl_ref[...]
    e = jnp.exp(lg - jnp.max(lg, axis=0, keepdims=True))
    pr = e / jnp.sum(e, axis=0, keepdims=True)
    cs = pr[0:1]
    for i in range(1, layer + 1):
        cs = cs + pr[i:i + 1]
    lb = jnp.clip(cs - pr[0:1], 0.0, 1.0)

    f = lb + (1.0 - lb) * _sigmoid(fz)
    log_f = jnp.log(jnp.maximum(f, HG_MIN_F))
    k = (1.0 - lb) * _sigmoid(-fz)

    same, incl, _ = _chunk_masks(L, C)
    b = _dot_xl(incl.astype(BF16), log_f, pieces=3)
    b_last = _dot_xl(same.astype(BF16), log_f, pieces=3)
    ones_h = _head_block_ones(MIX_W, HEAD_D)

    b3, q3, k3, v3 = (a.reshape(nck, C, MIX_W) for a in (b, q, k, vi))
    t_idx = _iota((nck, C, MIX_W), 1)
    o_intra = jnp.zeros((nck, C, MIX_W), F32)
    for s in range(C):
        w = jnp.exp(jnp.minimum(b3 - b3[:, s:s + 1, :], 0.0))
        x = jnp.where(t_idx >= s, q3 * w * k3[:, s:s + 1, :], 0.0)
        a_s = _dot(x.reshape(L, MIX_W), ones_h).reshape(nck, C, MIX_W)
        o_intra = o_intra + a_s * v3[:, s:s + 1, :]
    o_intra = o_intra.reshape(L, MIX_W)

    qe = q * jnp.exp(b)
    kd_t = (k * jnp.exp(b_last - b)).T
    e_t = jnp.exp(b_last).T
    bd = (_iota((MIX_W, MIX_W), 0) // HEAD_D) == (_iota((MIX_W, MIX_W), 1) // HEAD_D)
    col = _iota((1, L), 1)

    if chained:
        @pl.when(pl.program_id(1) == 0)
        def _():
            h_scr[...] = jnp.zeros((MIX_W, MIX_W), F32)
        hbd = h_scr[...]
    o_parts = []
    for ck in range(nck):
        if not chained:
            s_kv = jnp.concatenate([s0_ref[ck, h] for h in range(HG_HEADS)], axis=-1)
            hbd = _expand_state(s_kv)
        o_parts.append(_dot(qe[ck * C:(ck + 1) * C], hbd))
        cm = (col // C) == ck
        gk = jnp.where(bd, _dot(jnp.where(cm, kd_t, 0.0), vi), 0.0)
        dcol = jnp.sum(jnp.where(col == ck * C, e_t, 0.0), axis=1, keepdims=True)
        hbd = dcol * hbd + gk
        if not chained:
            s_new = _fold_state(hbd)
            for h in range(HG_HEADS):
                sout_ref[ck, h] = s_new[:, h * HEAD_D:(h + 1) * HEAD_D]
    if chained:
        h_scr[...] = hbd

        @pl.when(pl.program_id(1) == pl.num_programs(1) - 1)
        def _():
            s_new = _fold_state(hbd)
            for h in range(HG_HEADS):
                sout_ref[h] = s_new[:, h * HEAD_D:(h + 1) * HEAD_D]

    o = o_intra + jnp.concatenate(o_parts, axis=0)
    ms = _dot_xr(o * o, ones_h) * (1.0 / HEAD_D)
    on = o * lax.rsqrt(ms + EPS) * gn_ref[...]
    o_ref[...] = (on * (gate * _sigmoid(gate))).astype(o_ref.dtype)


def _hgrn(p_hg, lb_logits, gn, s0, layer, B, T, chained):
    n = p_hg.shape[0]
    if chained:
        L, C = min(256, T), min(16, T)
        nblk = T // L
        grid = (B, nblk)
        rowmap = lambda b, j: (b * nblk + j, 0)
        in_specs = [pl.BlockSpec((L, 1024), rowmap), _full(lb_logits.shape), _full(gn.shape)]
        args = [p_hg, lb_logits, gn]
        out_specs = [pl.BlockSpec((L, MIX_W), rowmap),
                     pl.BlockSpec((None, HG_HEADS, HEAD_D, HEAD_D), lambda b, j: (b, 0, 0, 0))]
        scratch = [pltpu.VMEM((MIX_W, MIX_W), F32)]
        sem = ("parallel", "arbitrary")
    else:
        L, C = n, T
        grid = (1,)
        in_specs = [_full((L, 1024)), _full(lb_logits.shape), _full(gn.shape), _full(s0.shape)]
        args = [p_hg, lb_logits, gn, s0]
        out_specs = [_full((L, MIX_W)), _full((B, HG_HEADS, HEAD_D, HEAD_D))]
        scratch = []
        sem = ("arbitrary",)
    return pl.pallas_call(
        functools.partial(_hgrn_kernel, L=L, C=C, layer=layer, chained=chained),
        grid=grid, in_specs=in_specs, out_specs=out_specs,
        out_shape=[jax.ShapeDtypeStruct((n, MIX_W), BF16),
                   jax.ShapeDtypeStruct((B, HG_HEADS, HEAD_D, HEAD_D), F32)],
        scratch_shapes=scratch, compiler_params=_params(sem), name="hgrn")(*args)


def _rwkv_kernel(*refs, L, C, chained):
    it = iter(refs)
    p_ref = next(it)
    if not chained:
        shrow_ref, s0_ref = next(it), next(it)
    (mu_ref, w0_ref, w2_ref, a0_ref, a2_ref, g2_ref, kkw_ref, ka_ref, rk_ref, lnw_ref, lnb_ref) = (next(it) for _ in range(11))
    y_ref, sout_ref = next(it), next(it)
    if chained:
        h_scr, carry_scr = next(it), next(it)

    nck = L // C
    p = p_ref[...]
    rowi = _iota((L, 1), 0)
    rolled = pltpu.roll(p, 1, 0)
    if chained:
        first = pl.program_id(1) == 0

        @pl.when(first)
        def _():
            h_scr[...] = jnp.zeros((MIX_W, MIX_W), F32)
            carry_scr[...] = jnp.zeros(carry_scr.shape, F32)
        prev = jnp.where(rowi == 0, carry_scr[0:1, :], rolled)
        carry_scr[0:1, :] = p[L - 1:L, :]
    else:
        prev = jnp.where((rowi & (C - 1)) == 0, shrow_ref[...], rolled)
    ps = p + (prev - p) * mu_ref[...]
    r, k, v, tail = ps[:, 0:256], ps[:, 256:512], ps[:, 512:768], ps[:, 768:896]

    w_raw = -jax.nn.softplus(-(w0_ref[...] + _dot(jnp.tanh(tail), w2_ref[...]))) - 0.5
    lw = -jnp.exp(w_raw)
    a = _sigmoid(a0_ref[...] + _dot(tail, a2_ref[...]))
    g = _dot(_sigmoid(tail), g2_ref[...])

    ones_h = _head_block_ones(MIX_W, HEAD_D)
    kk = k * kkw_ref[...]
    kk = kk / jnp.maximum(jnp.sqrt(_dot_xr(kk * kk, ones_h)), 1e-12)
    k2 = k * (1.0 + (a - 1.0) * ka_ref[...])
    bb = kk * a

    same, incl, strict = _chunk_masks(L, C)
    gc = _dot_xl(incl.astype(BF16), lw)
    g_last = _dot_xl(same.astype(BF16), lw)
    eg, eng = jnp.exp(gc), jnp.exp(-gc)
    rg, kg, bg, kkg = r * eg, k2 * eng, bb * eng, kk * jnp.exp(gc - lw)
    to_end = jnp.exp(g_last - gc)
    kd_t, bd_t, e_t = (k2 * to_end).T, (bb * to_end).T, jnp.exp(g_last).T

    lane_h = _iota((1, MIX_W), 1) // HEAD_D
    eye = (_iota((L, L), 0) == _iota((L, L), 1)).astype(F32)
    u0 = jnp.zeros((L, MIX_W), F32)
    kkt = jnp.zeros((L, MIX_W), F32)
    qp = rg
    o0 = jnp.zeros((L, MIX_W), F32)
    for h in range(RW_HEADS):
        mh = lane_h == h
        kkg_h, rg_h = jnp.where(mh, kkg, 0.0), jnp.where(mh, rg, 0.0)
        a_kb = jnp.where(strict, _dot_nt(kkg_h, bg), 0.0)
        a_kk = jnp.where(strict, _dot_nt(kkg_h, kg), 0.0)
        a_rk = jnp.where(incl, _dot_nt(rg_h, kg), 0.0)
        a_rb = jnp.where(incl, _dot_nt(rg_h, bg), 0.0)
        npow = -a_kb
        tinv = eye + npow
        for _ in range(int(math.log2(C)) - 1):
            npow = _dot(npow, npow)
            tinv = tinv + _dot(tinv, npow)
        u0_h = _dot(tinv, _dot(a_kk, v))
        kkt_h = _dot(tinv, kkg)
        u0 = jnp.where(mh, u0_h, u0)
        kkt = jnp.where(mh, kkt_h, kkt)
        qp = qp - jnp.where(mh, _dot(a_rb, kkt_h), 0.0)
        o0 = o0 + jnp.where(mh, _dot(a_rk, v) - _dot(a_rb, u0_h), 0.0)

    bd = (_iota((MIX_W, MIX_W), 0) // HEAD_D) == (_iota((MIX_W, MIX_W), 1) // HEAD_D)
    col = _iota((1, L), 1)
    if chained:
        hbd = h_scr[...]
    y_parts = []
    for ck in range(nck):
        if not chained:
            hbd = _expand_state(s0_ref[ck].T)
        y_parts.append(_dot(qp[ck * C:(ck + 1) * C], hbd))
        cm = (col // C) == ck
        bd_c = jnp.where(cm, bd_t, 0.0)
        mb = jnp.where(bd, _dot(bd_c, kkt), 0.0)
        gb = jnp.where(bd, _dot(jnp.where(cm, kd_t, 0.0), v) - _dot(bd_c, u0), 0.0)
        dcol = jnp.sum(jnp.where(col == ck * C, e_t, 0.0), axis=1, keepdims=True)
        hbd = dcol * hbd - _dot(mb, hbd) + gb
        if not chained:
            sout_ref[ck] = _fold_state(hbd).T
    if chained:
        h_scr[...] = hbd

        @pl.when(pl.program_id(1) == pl.num_programs(1) - 1)
        def _():
            sout_ref[...] = _fold_state(hbd).T

    y = o0 + jnp.concatenate(y_parts, axis=0)
    mean = _dot_xr(y, ones_h) * (1.0 / HEAD_D)
    yc = y - mean
    var = _dot_xr(yc * yc, ones_h) * (1.0 / HEAD_D)
    yn = yc * lax.rsqrt(var + RW_GN_EPS) * lnw_ref[...] + lnb_ref[...]
    bonus = _dot_xr(r * k2 * rk_ref[...], ones_h) * v
    y_ref[...] = ((yn + bonus) * g).astype(y_ref.dtype)


def _rwkv(p_rw, shift_rows, s0, rw, B, T, chained):
    n = p_rw.shape[0]
    names = ("mu", "w0", "w2", "a0", "a2", "g2", "kk", "ka", "rk", "lnw", "lnb")
    wargs = [rw[k] for k in names]
    wspecs = [_full(a.shape) for a in wargs]
    if chained:
        L, C = min(256, T), min(64, T)
        nblk = T // L
        grid = (B, nblk)
        rowmap = lambda b, j: (b * nblk + j, 0)
        in_specs = [pl.BlockSpec((L, RW_COLS), rowmap)] + wspecs
        args = [p_rw] + wargs
        out_specs = [pl.BlockSpec((L, MIX_W), rowmap), pl.BlockSpec((None, MIX_W, HEAD_D), lambda b, j: (b, 0, 0))]
        scratch = [pltpu.VMEM((MIX_W, MIX_W), F32), pltpu.VMEM((8, RW_COLS), F32)]
        sem = ("parallel", "arbitrary")
    else:
        L, C = n, T
        grid = (1,)
        in_specs = [_full((L, RW_COLS)), _full(shift_rows.shape), _full(s0.shape)] + wspecs
        args = [p_rw, shift_rows, s0] + wargs
        out_specs = [_full((L, MIX_W)), _full((B, MIX_W, HEAD_D))]
        scratch = []
        sem = ("arbitrary",)
    return pl.pallas_call(
        functools.partial(_rwkv_kernel, L=L, C=C, chained=chained),
        grid=grid, in_specs=in_specs, out_specs=out_specs,
        out_shape=[jax.ShapeDtypeStruct((n, MIX_W), BF16), jax.ShapeDtypeStruct((B, MIX_W, HEAD_D), F32)],
        scratch_shapes=scratch, compiler_params=_params(sem), name="rwkv")(*args)


def _attn_prompt_kernel(q_ref, k_ref, v_ref, o_ref, *, tq):
    i = pl.program_id(2)
    rows = i * tq + _iota((tq, tq), 0)
    outs = []
    for hh in range(2):
        q = q_ref[:, hh * HEAD_SLOT:(hh + 1) * HEAD_SLOT]

        def body(j, carry, hh=hh, q=q):
            m, l, acc = carry
            kb = k_ref[pl.ds(j * tq, tq), hh * HEAD_SLOT:(hh + 1) * HEAD_SLOT]
            s = lax.dot_general(q, kb, (((1,), (1,)), ((), ())), preferred_element_type=F32)
            s = jnp.where(j * tq + _iota((tq, tq), 1) <= rows, s, NEG_INF)
            m_new = jnp.maximum(m, jnp.max(s, axis=-1, keepdims=True))
            pexp = jnp.exp(s - m_new)
            alpha = jnp.exp(m - m_new)
            l = alpha * l + jnp.sum(pexp, axis=-1, keepdims=True)
            acc = alpha * acc + jnp.dot(pexp.astype(BF16), v_ref[pl.ds(j * tq, tq), :], preferred_element_type=F32)
            return m_new, l, acc

        m0 = jnp.full((tq, 1), NEG_INF, F32)
        m, l, acc = lax.fori_loop(0, i + 1, body, (m0, jnp.zeros((tq, 1), F32), jnp.zeros((tq, 2 * MLA_VDIM), F32)))
        outs.append(acc / l)
    lane = _iota((tq, 2 * MLA_VDIM), 1)
    o_ref[...] = jnp.where(lane < MLA_VDIM, outs[0], outs[1]).astype(o_ref.dtype)


def _attn_prompt(q, k, v, B, T, tq=512):
    n = q.shape[0]
    tq = min(tq, T)
    nq = T // tq
    return pl.pallas_call(
        functools.partial(_attn_prompt_kernel, tq=tq),
        grid=(B, MLA_HEADS // 2, nq),
        in_specs=[pl.BlockSpec((tq, 2 * HEAD_SLOT), lambda b, hp, i: (b * nq + i, hp)),
                  pl.BlockSpec((T, 2 * HEAD_SLOT), lambda b, hp, i: (b, hp)),
                  pl.BlockSpec((T, 2 * MLA_VDIM), lambda b, hp, i: (b, hp))],
        out_specs=pl.BlockSpec((tq, 2 * MLA_VDIM), lambda b, hp, i: (b * nq + i, hp)),
        out_shape=jax.ShapeDtypeStruct((n, MLA_HEADS * MLA_VDIM), BF16),
        compiler_params=_params(("parallel", "parallel", "arbitrary")), name="attn_prompt")(q, k, v)


PAGES_PER_STEP = 8


def _attn_sample_kernel(pt_ref, *refs, T, npp):
    it = iter(refs)
    qlat_ref, q_ref, c_ref, kpe_ref, wuv_ref = (next(it) for _ in range(5))
    ckv_refs = [next(it) for _ in range(npp)]
    kpp_refs = [next(it) for _ in range(npp)]
    o_ref = next(it)
    ql_scr, qp_scr, m_scr, l_scr, acc_scr = (next(it) for _ in range(5))
    j = pl.program_id(1)
    R = MLA_HEADS * T

    @pl.when(j == 0)
    def _():
        ql = jnp.concatenate([qlat_ref[:, h * MLA_KV_RANK:(h + 1) * MLA_KV_RANK] for h in range(MLA_HEADS)], axis=0)
        qs = jnp.concatenate([q_ref[:, h * HEAD_SLOT:(h + 1) * HEAD_SLOT] for h in range(MLA_HEADS)], axis=0)
        ql_scr[...] = ql
        qp_scr[...] = qs[:, ROPE_LANE0:ROPE_LANE0 + MLA_ROPE]
        m_scr[...] = jnp.full(m_scr.shape, NEG_INF, F32)
        l_scr[...] = jnp.zeros(l_scr.shape, F32)
        acc_scr[...] = jnp.zeros(acc_scr.shape, F32)

    ql = ql_scr[...].astype(BF16)
    qp = qp_scr[...].astype(BF16)

    def scores(cb, kb_t):
        return _dot_nt(ql, cb) + jnp.dot(qp, kb_t, preferred_element_type=F32)

    def update(s, cbs):
        m = m_scr[...]
        m_new = jnp.maximum(m, jnp.max(s, axis=-1, keepdims=True))
        pexp = jnp.exp(s - m_new)
        alpha = jnp.exp(m - m_new)
        l_scr[...] = alpha * l_scr[...] + jnp.sum(pexp, axis=-1, keepdims=True)
        acc = alpha * acc_scr[...]
        for i, cb in enumerate(cbs):
            n = cb.shape[0]
            acc = acc + jnp.dot(pexp[:, i * n:(i + 1) * n].astype(BF16), cb, preferred_element_type=F32)
        acc_scr[...] = acc
        m_scr[...] = m_new

    cbs = [r[...].astype(BF16) for r in ckv_refs]
    kbs = [r[...].astype(BF16) for r in kpp_refs]
    update(jnp.concatenate([scores(cb, kb) for cb, kb in zip(cbs, kbs)], axis=-1), cbs)

    @pl.when(j == pl.num_programs(1) - 1)
    def _():
        pad = PAGE_SIZE - T
        cn = jnp.concatenate([c_ref[...], jnp.zeros((pad, MLA_KV_RANK), F32)], axis=0).astype(BF16)
        kn = jnp.concatenate([kpe_ref[...], jnp.zeros((pad, MLA_ROPE), F32)], axis=0).astype(BF16)
        tq = _iota((R, PAGE_SIZE), 0) & (T - 1)
        s = jnp.where(_iota((R, PAGE_SIZE), 1) <= tq, _dot_nt(ql, cn) + _dot_nt(qp, kn), NEG_INF)
        update(s, [cn])
        o_lat = acc_scr[...] / l_scr[...]
        z = _dot(o_lat, wuv_ref[...])
        lane_h = _iota((T, MLA_HEADS * MLA_VDIM), 1) // MLA_VDIM
        out = jnp.zeros((T, MLA_HEADS * MLA_VDIM), F32)
        for h in range(MLA_HEADS):
            out = jnp.where(lane_h == h, z[h * T:(h + 1) * T], out)
        o_ref[...] = out.astype(o_ref.dtype)


def _attn_sample(qlat, q, c, kpe, wuv, cache_ckv, cache_kpe, page_table, layer, B, T):
    n = qlat.shape[0]
    n_pages = page_table.shape[1]
    npp = PAGES_PER_STEP
    R = MLA_HEADS * T
    row = lambda w: pl.BlockSpec((T, w), lambda b, j, pt: (b, 0))

    def page_spec(shape, i):
        return pl.BlockSpec((None, None) + shape, lambda b, j, pt: (layer, pt[b, j * npp + i], 0, 0))

    in_specs = [row(qlat.shape[1]), row(q.shape[1]), row(MLA_KV_RANK), row(MLA_ROPE),
                pl.BlockSpec(wuv.shape, lambda b, j, pt: (0, 0))]
    in_specs += [page_spec((PAGE_SIZE, MLA_KV_RANK), i) for i in range(npp)]
    in_specs += [page_spec((MLA_ROPE, PAGE_SIZE), i) for i in range(npp)]
    grid_spec = pltpu.PrefetchScalarGridSpec(
        num_scalar_prefetch=1, grid=(B, n_pages // npp), in_specs=in_specs,
        out_specs=row(MLA_HEADS * MLA_VDIM),
        scratch_shapes=[pltpu.VMEM((R, MLA_KV_RANK), F32), pltpu.VMEM((R, MLA_ROPE), F32),
                        pltpu.VMEM((R, 1), F32), pltpu.VMEM((R, 1), F32), pltpu.VMEM((R, MLA_KV_RANK), F32)])
    return pl.pallas_call(
        functools.partial(_attn_sample_kernel, T=T, npp=npp), grid_spec=grid_spec,
        out_shape=jax.ShapeDtypeStruct((n, MLA_HEADS * MLA_VDIM), F32),
        compiler_params=_params(("parallel", "arbitrary")), name="attn_sample")(
            page_table, qlat, q, c, kpe, wuv, *([cache_ckv] * npp), *([cache_kpe] * npp))


def _rope_rows():
    half = MLA_ROPE // 2
    inv = jnp.exp(-jnp.log(ROPE_THETA) * jnp.arange(half, dtype=F32) / half)
    lane = np.arange(LANES)
    on = (lane >= ROPE_LANE0) & (lane < ROPE_LANE0 + MLA_ROPE)
    inv_l = jnp.where(on, inv[(lane - ROPE_LANE0) % half], 0.0)
    sign = np.where(on, np.where(lane < ROPE_LANE0 + half, -1.0, 1.0), 0.0)
    rows = [inv_l, (lane < ROPE_LANE0).astype(np.float32), sign, on.astype(np.float32)]
    rows += [np.zeros(LANES, np.float32)] * 4
    return jnp.stack([jnp.asarray(r, F32) for r in rows])


def _layer_weights(l, W):
    half = MLA_ROPE // 2
    r2 = lambda a: a.reshape(1, -1)
    w_in = W["w_in"][l]
    whg, wrw, wmla = w_in[:, :1024], w_in[:, 1024:1920], w_in[:, 1920:]
    wkpe = wmla[:, MLA_Q_RANK + MLA_KV_RANK:]
    z = lambda n: jnp.zeros((D_MODEL, n), F32)
    wmla_pad = jnp.concatenate([wmla[:, :MLA_Q_RANK + MLA_KV_RANK], z(ROPE_LANE0), wkpe, z(HEAD_SLOT - ROPE_LANE0 - MLA_ROPE),
                                z(ROPE_LANE0), wkpe[:, half:], wkpe[:, :half], z(HEAD_SLOT - ROPE_LANE0 - MLA_ROPE)], axis=1)
    wqb = W["mla_wqb"][l].reshape(MLA_Q_RANK, MLA_HEADS, MLA_NOPE + MLA_ROPE)
    zq = lambda n: jnp.zeros((MLA_Q_RANK, MLA_HEADS, n), F32)
    rest = HEAD_SLOT - MLA_NOPE - MLA_ROPE
    wq_pad = jnp.concatenate([wqb, zq(rest)], axis=-1)
    wq_rot = jnp.concatenate([zq(MLA_NOPE), wqb[..., MLA_NOPE + half:], wqb[..., MLA_NOPE:MLA_NOPE + half], zq(rest)], axis=-1)
    hw = MLA_HEADS * HEAD_SLOT
    wq = jnp.concatenate([wq_pad.reshape(MLA_Q_RANK, hw), wq_rot.reshape(MLA_Q_RANK, hw)], axis=1)
    wuk = W["mla_wuk"][l]
    wuk_pad = jnp.concatenate([wuk, jnp.zeros((MLA_KV_RANK, MLA_HEADS, HEAD_SLOT - MLA_NOPE), F32)], axis=-1).reshape(MLA_KV_RANK, hw)
    wukt = wuk.transpose(1, 2, 0).reshape(MLA_HEADS * MLA_NOPE, MLA_KV_RANK)
    wuv = W["mla_wuv"][l].reshape(MLA_KV_RANK, MLA_HEADS * MLA_VDIM)
    proj = dict(g=r2(W["norm_mix"][l]), whg=whg.astype(BF16), wrw=wrw.astype(BF16), wmla=wmla_pad.astype(BF16),
                qn=r2(W["mla_q_norm"][l]), wq=wq.astype(BF16), kvn=r2(W["mla_kv_norm"][l]), rope=_rope_rows(),
                wuk=wuk_pad.astype(BF16), wuv=wuv.astype(BF16), wukt=wukt.astype(BF16))
    zr = lambda n: jnp.zeros((n, RW_WIDTH), F32)
    rwk = dict(mu=r2(W["rw_mu"][l]), w0=r2(W["rw_w0"][l]),
               w2=jnp.concatenate([W["rw_w2"][l], zr(96)], axis=0).astype(BF16),
               a0=r2(W["rw_a0"][l]),
               a2=jnp.concatenate([zr(32), W["rw_a2"][l], zr(64)], axis=0).astype(BF16),
               g2=jnp.concatenate([zr(64), W["rw_g2"][l]], axis=0).astype(BF16),
               kk=r2(W["rw_kk"][l]), ka=r2(W["rw_ka"][l]), rk=r2(W["rw_rk"][l]),
               lnw=r2(W["rw_ln_w"][l]), lnb=r2(W["rw_ln_b"][l]))
    return dict(
        proj=proj, rwkv=rwk,
        ffn1=(r2(W["norm_ffn1"][l]), W["ffn1_wi"][l].astype(BF16), W["ffn1_wo"][l].astype(BF16)),
        ffn2=(r2(W["norm_ffn2"][l]), W["ffn2_wi"][l].astype(BF16), W["ffn2_wo"][l].astype(BF16)),
        wout=W["w_out"][l].astype(BF16), hgn=r2(jnp.tile(W["hg_norm"][l], HG_HEADS)))


def _trunk(x, B, T, pos0, LW, W, sample, st=None):
    n = B * T
    x = x.reshape(n, D_MODEL)
    hg_out, rw_out, sh_out, c_out, kpe_out = [], [], [], [], []
    for l in range(DEPTH):
        lw = LW[l]
        x = _ffn(x, *lw["ffn1"])
        if sample:
            p_hg, p_rw, q, qlat, c, kpe = _proj(x, lw["proj"], T, pos0, True)
            o_hg, hg_s = _hgrn(p_hg, W["hg_lb_logits"], lw["hgn"], st["hg"][l], l, B, T, False)
            shift_rows = jnp.repeat(st["sh"][l], T, axis=0)
            o_rw, rw_s = _rwkv(p_rw, shift_rows, st["rw"][l].reshape(B, MIX_W, HEAD_D), lw["rwkv"], B, T, False)
            o_mla = _attn_sample(qlat, q, c, kpe, lw["proj"]["wuv"], st["ckv"], st["kpe"], st["pt"], l, B, T)
        else:
            p_hg, p_rw, q, k, v, c, kpe = _proj(x, lw["proj"], T, pos0, False)
            o_hg, hg_s = _hgrn(p_hg, W["hg_lb_logits"], lw["hgn"], None, l, B, T, True)
            o_rw, rw_s = _rwkv(p_rw, None, None, lw["rwkv"], B, T, True)
            o_mla = _attn_prompt(q, k, v, B, T)
        gf = W["norm_final"].reshape(1, -1) if l == DEPTH - 1 else None
        x = _ffn(x, *lw["ffn2"], mix=(o_hg, o_rw, o_mla), wout=lw["wout"], gf=gf)
        hg_out.append(hg_s)
        rw_out.append(rw_s.reshape(B, RW_HEADS, RW_DH, RW_DH))
        sh_out.append(p_rw.reshape(B, T, RW_COLS)[:, -1])
        c_out.append(c.reshape(B, T, MLA_KV_RANK))
        kpe_out.append(kpe.reshape(B, T, MLA_ROPE))
    st_ = jnp.stack
    return x.reshape(B, T, D_MODEL), st_(hg_out), st_(rw_out), st_(sh_out), st_(c_out), st_(kpe_out)


def kernel(x_prompt, x_sample, cache_mla_ckv, cache_mla_kpe, state_hgrn, state_rwkv, state_rwkv_shift, page_table, norm_ffn1, ffn1_wi, ffn1_wo, norm_mix, w_in, w_out, hg_lb_logits, hg_norm, rw_mu, rw_w0, rw_w2, rw_a0, rw_a2, rw_g2, rw_kk, rw_ka, rw_rk, rw_ln_w, rw_ln_b, mla_q_norm, mla_wqb, mla_kv_norm, mla_wuk, mla_wuv, norm_ffn2, ffn2_wi, ffn2_wo, norm_final):
    W = dict(norm_ffn1=norm_ffn1, ffn1_wi=ffn1_wi, ffn1_wo=ffn1_wo, norm_mix=norm_mix, w_in=w_in, w_out=w_out,
             hg_lb_logits=hg_lb_logits, hg_norm=hg_norm, rw_mu=rw_mu, rw_w0=rw_w0, rw_w2=rw_w2, rw_a0=rw_a0,
             rw_a2=rw_a2, rw_g2=rw_g2, rw_kk=rw_kk, rw_ka=rw_ka, rw_rk=rw_rk, rw_ln_w=rw_ln_w, rw_ln_b=rw_ln_b,
             mla_q_norm=mla_q_norm, mla_wqb=mla_wqb, mla_kv_norm=mla_kv_norm, mla_wuk=mla_wuk, mla_wuv=mla_wuv,
             norm_ffn2=norm_ffn2, ffn2_wi=ffn2_wi, ffn2_wo=ffn2_wo, norm_final=norm_final)
    LW = [_layer_weights(l, W) for l in range(DEPTH)]
    Bp, Tp, _ = x_prompt.shape
    y_p, hg_p, rw_p, sh_p, ckv_p, kpe_p = _trunk(x_prompt, Bp, Tp, 0, LW, W, False)
    Bs, Ts, _ = x_sample.shape
    past_len = page_table.shape[1] * cache_mla_ckv.shape[2]
    st = dict(hg=state_hgrn, rw=state_rwkv, sh=state_rwkv_shift, ckv=cache_mla_ckv,
              kpe=jnp.swapaxes(cache_mla_kpe, 2, 3), pt=page_table)
    y_s, hg_s, rw_s, sh_s, ckv_s, kpe_s = _trunk(x_sample, Bs, Ts, past_len, LW, W, True, st)
    return (y_p, y_s, hg_p, hg_s, rw_p, rw_s, sh_p, sh_s, ckv_p, ckv_s, kpe_p, kpe_s)
```

```python
import functools
import math

import jax
import jax.numpy as jnp
import numpy as np
from jax import lax
from jax.experimental import pallas as pl
from jax.experimental.pallas import tpu as pltpu

F32 = jnp.float32
BF16 = jnp.bfloat16

D_MODEL = 1024
DEPTH = 2
PAGE_SIZE = 128
EPS = 1e-6
D_FF = 2816
HG_HEADS = 4
HG_DK = 64
HG_KW = 256
HG_MIN_F = 1e-30
RW_HEADS = 4
RW_DH = 64
RW_WIDTH = 256
RW_COLS = 896
RW_GN_EPS = 64e-5
MLA_HEADS = 8
MLA_NOPE = 64
MLA_ROPE = 32
MLA_VDIM = 64
MLA_Q_RANK = 384
MLA_KV_RANK = 256
MLA_SCALE = (MLA_NOPE + MLA_ROPE) ** -0.5
ROPE_THETA = 10000.0

LANES = 128
HEAD_SLOT = 128
ROPE_LANE0 = 64
VMEM_LIMIT = 56 * 1024 * 1024
NEG_INF = -1e30


def _dot(a, b):
    return jnp.dot(a.astype(BF16), b.astype(BF16), preferred_element_type=F32)


def _dot_nt(a, b):
    return lax.dot_general(a.astype(BF16), b.astype(BF16), (((1,), (1,)), ((), ())), preferred_element_type=F32)


def _dot_xl(m, b, pieces=2):
    out = None
    rem = b
    for _ in range(pieces):
        part = rem.astype(BF16)
        term = jnp.dot(m, part, preferred_element_type=F32)
        out = term if out is None else out + term
        rem = rem - part.astype(F32)
    return out


def _dot_xr(a, m, pieces=2):
    out = None
    rem = a
    for _ in range(pieces):
        part = rem.astype(BF16)
        term = jnp.dot(part, m, preferred_element_type=F32)
        out = term if out is None else out + term
        rem = rem - part.astype(F32)
    return out


def _rms(x, g, eps=EPS):
    return x * lax.rsqrt(jnp.mean(x * x, axis=-1, keepdims=True) + eps) * g


def _sigmoid(x):
    return 1.0 / (1.0 + jnp.exp(-x))


def _iota(shape, dim):
    return lax.broadcasted_iota(jnp.int32, shape, dim)


def _head_block_ones(n, hd):
    return (_iota((n, n), 0) // hd == _iota((n, n), 1) // hd).astype(BF16)


def _full(shape):
    return pl.BlockSpec(shape, lambda *_: (0,) * len(shape), pipeline_mode=pl.Buffered(1))


def _params(sem):
    return pltpu.CompilerParams(dimension_semantics=sem, vmem_limit_bytes=VMEM_LIMIT)


FF_CHUNK = 256


def _ffn_kernel(*refs, has_mix, final):
    it = iter(refs)
    x_ref = next(it)
    if has_mix:
        ohg_ref, orw_ref, omla_ref, wout_ref = next(it), next(it), next(it), next(it)
    g_ref, wi_ref, wo_ref = next(it), next(it), next(it)
    gf_ref = next(it) if final else None
    o_ref = next(it)

    x = x_ref[...]
    if has_mix:
        o = jnp.concatenate([r[...].astype(BF16) for r in (ohg_ref, orw_ref, omla_ref)], axis=-1)
        x = x + jnp.dot(o, wout_ref[...], preferred_element_type=F32)
    xn = _rms(x, g_ref[...]).astype(BF16)
    acc = jnp.zeros(x.shape, F32)
    for c in range(D_FF // FF_CHUNK):
        lo = c * FF_CHUNK
        a = jnp.dot(xn, wi_ref[:, lo:lo + FF_CHUNK], preferred_element_type=F32)
        b = jnp.dot(xn, wi_ref[:, D_FF + lo:D_FF + lo + FF_CHUNK], preferred_element_type=F32)
        h = (a * _sigmoid(a) * b).astype(BF16)
        acc = acc + jnp.dot(h, wo_ref[lo:lo + FF_CHUNK, :], preferred_element_type=F32)
    y = x + 0.5 * acc
    if final:
        y = _rms(y, gf_ref[...])
    o_ref[...] = y


def _ffn(x, g, wi, wo, mix=None, wout=None, gf=None, tm=512):
    n = x.shape[0]
    tm = min(tm, n)
    row = lambda w: pl.BlockSpec((tm, w), lambda i: (i, 0))
    args, specs = [x], [row(D_MODEL)]
    if mix is not None:
        for m in mix:
            args.append(m)
            specs.append(row(m.shape[1]))
        args.append(wout)
        specs.append(_full(wout.shape))
    args += [g, wi, wo]
    specs += [_full(g.shape), _full(wi.shape), _full(wo.shape)]
    if gf is not None:
        args.append(gf)
        specs.append(_full(gf.shape))
    return pl.pallas_call(
        functools.partial(_ffn_kernel, has_mix=mix is not None, final=gf is not None),
        grid=(n // tm,), in_specs=specs, out_specs=row(D_MODEL),
        out_shape=jax.ShapeDtypeStruct((n, D_MODEL), F32),
        compiler_params=_params(("parallel",)), name="ffn")(*args)


MLA_PAD = 896


def _proj_kernel(*refs, seq_len, pos0, sample):
    it = iter(refs)
    x_ref, g_ref, whg_ref, wrw_ref, wmla_ref = (next(it) for _ in range(5))
    qn_ref, wq_ref, kvn_ref, rope_ref = (next(it) for _ in range(4))
    if sample:
        wukt_ref = next(it)
    else:
        wuk_ref, wuv_ref = next(it), next(it)
    phg_ref, prw_ref, q_ref = next(it), next(it), next(it)
    if sample:
        qlat_ref = next(it)
    else:
        k_ref, v_ref = next(it), next(it)
    c_ref, kpe_ref = next(it), next(it)

    tm = x_ref.shape[0]
    xn = _rms(x_ref[...], g_ref[...]).astype(BF16)
    phg_ref[...] = jnp.dot(xn, whg_ref[...], preferred_element_type=F32)
    prw_ref[...] = jnp.dot(xn, wrw_ref[...], preferred_element_type=F32)
    pm = jnp.dot(xn, wmla_ref[...], preferred_element_type=F32)

    row = pl.program_id(0) * tm + _iota((tm, 1), 0)
    pos = (pos0 + (row & (seq_len - 1))).astype(F32)
    rope = rope_ref[...]
    ang = pos * rope[0:1, :]
    cos_s = rope[1:2, :] + rope[3:4, :] * jnp.cos(ang)
    sin_s = rope[2:3, :] * jnp.sin(ang)

    c = _rms(pm[:, MLA_Q_RANK:MLA_Q_RANK + MLA_KV_RANK], kvn_ref[...])
    c_ref[...] = c
    k_slot = pm[:, 640:768] * cos_s + pm[:, 768:896] * sin_s
    kpe_ref[...] = k_slot[:, ROPE_LANE0:ROPE_LANE0 + MLA_ROPE]

    qn = _rms(pm[:, :MLA_Q_RANK], qn_ref[...]).astype(BF16)
    q2 = jnp.dot(qn, wq_ref[...], preferred_element_type=F32)
    hw = MLA_HEADS * HEAD_SLOT
    q_scale = MLA_SCALE if sample else MLA_SCALE * math.log2(math.e)
    cos_q = jnp.concatenate([cos_s * q_scale] * MLA_HEADS, axis=-1)
    sin_q = jnp.concatenate([sin_s * q_scale] * MLA_HEADS, axis=-1)
    q = q2[:, :hw] * cos_q + q2[:, hw:] * sin_q
    if sample:
        q_ref[...] = q
        for h in range(MLA_HEADS):
            qh = q[:, h * HEAD_SLOT:h * HEAD_SLOT + MLA_NOPE]
            qlat_ref[:, h * MLA_KV_RANK:(h + 1) * MLA_KV_RANK] = _dot(qh, wukt_ref[h * MLA_NOPE:(h + 1) * MLA_NOPE, :])
    else:
        q_ref[...] = q.astype(BF16)
        cb = c.astype(BF16)
        kn = jnp.dot(cb, wuk_ref[...], preferred_element_type=F32)
        k_ref[...] = (kn + jnp.concatenate([k_slot] * MLA_HEADS, axis=-1)).astype(BF16)
        v_ref[...] = jnp.dot(cb, wuv_ref[...], preferred_element_type=F32).astype(BF16)


def _proj(x, pw, seq_len, pos0, sample, tm=512):
    n = x.shape[0]
    tm = min(tm, n)
    row = lambda w: pl.BlockSpec((tm, w), lambda i: (i, 0))
    hw = MLA_HEADS * HEAD_SLOT
    args = [x, pw["g"], pw["whg"], pw["wrw"], pw["wmla"], pw["qn"], pw["wq"], pw["kvn"], pw["rope"]]
    args += [pw["wukt"]] if sample else [pw["wuk"], pw["wuv"]]
    specs = [row(D_MODEL)] + [_full(a.shape) for a in args[1:]]
    qdt = F32 if sample else BF16
    outs = [((n, 1024), F32), ((n, RW_COLS), F32), ((n, hw), qdt)]
    outs += [((n, MLA_HEADS * MLA_KV_RANK), F32)] if sample else [((n, hw), BF16), ((n, MLA_HEADS * MLA_VDIM), BF16)]
    outs += [((n, MLA_KV_RANK), F32), ((n, MLA_ROPE), F32)]
    return pl.pallas_call(
        functools.partial(_proj_kernel, seq_len=seq_len, pos0=pos0, sample=sample),
        grid=(n // tm,), in_specs=specs, out_specs=[row(s[1]) for s, _ in outs],
        out_shape=[jax.ShapeDtypeStruct(s, d) for s, d in outs],
        compiler_params=_params(("parallel",)), name="proj")(*args)


MIX_W = 256
HEAD_D = 64


def _chunk_masks(L, C):
    r, c = _iota((L, L), 0), _iota((L, L), 1)
    same = (r // C) == (c // C)
    return same, same & (c <= r), same & (c < r)


def _expand_state(s_kv):
    t = jnp.concatenate([s_kv] * (MIX_W // HEAD_D), axis=0)
    bd = (_iota((MIX_W, MIX_W), 0) // HEAD_D) == (_iota((MIX_W, MIX_W), 1) // HEAD_D)
    return jnp.where(bd, t, 0.0)


def _fold_state(hbd):
    out = hbd[0:HEAD_D]
    for h in range(1, MIX_W // HEAD_D):
        out = out + hbd[h * HEAD_D:(h + 1) * HEAD_D]
    return out


def _hgrn_kernel(*refs, L, C, layer, chained):
    it = iter(refs)
    p_ref, lbl_ref, gn_ref = next(it), next(it), next(it)
    s0_ref = None if chained else next(it)
    o_ref, sout_ref = next(it), next(it)
    h_scr = next(it) if chained else None

    nck = L // C
    p = p_ref[...]
    q, fz, vi, gate = p[:, 0:256], p[:, 256:512], p[:, 512:768], p[:, 768:1024]

    lg = lbl_ref[...]
    e = jnp.exp(lg - jnp.max(lg, axis=0, keepdims=True))
    pr = e / jnp.sum(e, axis=0, keepdims=True)
    cs = pr[0:1]
    for i in range(1, layer + 1):
        cs = cs + pr[i:i + 1]
    lb = jnp.clip(cs - pr[0:1], 0.0, 1.0)

    f = lb + (1.0 - lb) * _sigmoid(fz)
    log_f = jnp.log(jnp.maximum(f, HG_MIN_F))
    k = (1.0 - lb) * _sigmoid(-fz)

    same, incl, _ = _chunk_masks(L, C)
    b = _dot_xl(incl.astype(BF16), log_f, pieces=3)
    b_last = _dot_xl(same.astype(BF16), log_f, pieces=3)
    ones_h = _head_block_ones(MIX_W, HEAD_D)

    b3, q3, k3, v3 = (a.reshape(nck, C, MIX_W) for a in (b, q, k, vi))
    SUB = 8
    tiles = [jnp.zeros((nck, min(SUB, C), MIX_W), F32) for _ in range(max(C // SUB, 1))]
    for s in range(C):
        t0 = (s // SUB) * SUB
        bt, qt = b3[:, t0:, :], q3[:, t0:, :]
        rows = C - t0
        w = jnp.exp(jnp.minimum(bt - b3[:, s:s + 1, :], 0.0))
        x = jnp.where(t0 + _iota((nck, rows, MIX_W), 1) >= s, qt * w * k3[:, s:s + 1, :], 0.0)
        a_s = _dot(x.reshape(nck * rows, MIX_W), ones_h).reshape(nck, rows, MIX_W)
        upd = a_s * v3[:, s:s + 1, :]
        for ti in range(t0 // SUB, len(tiles)):
            lo = ti * SUB - t0
            tiles[ti] = tiles[ti] + upd[:, lo:lo + SUB, :]
    o_intra = jnp.concatenate(tiles, axis=1).reshape(L, MIX_W)

    qe = q * jnp.exp(b)
    kd_t = (k * jnp.exp(b_last - b)).T
    e_t = jnp.exp(b_last).T
    bd = (_iota((MIX_W, MIX_W), 0) // HEAD_D) == (_iota((MIX_W, MIX_W), 1) // HEAD_D)
    col = _iota((1, L), 1)

    if chained:
        @pl.when(pl.program_id(1) == 0)
        def _():
            h_scr[...] = jnp.zeros((MIX_W, MIX_W), F32)
        hbd = h_scr[...]
    o_parts = []
    for ck in range(nck):
        if not chained:
            s_kv = jnp.concatenate([s0_ref[ck, h] for h in range(HG_HEADS)], axis=-1)
            hbd = _expand_state(s_kv)
        o_parts.append(_dot(qe[ck * C:(ck + 1) * C], hbd))
        cm = (col // C) == ck
        gk = jnp.where(bd, _dot(jnp.where(cm, kd_t, 0.0), vi), 0.0)
        dcol = jnp.sum(jnp.where(col == ck * C, e_t, 0.0), axis=1, keepdims=True)
        hbd = dcol * hbd + gk
        if not chained:
            s_new = _fold_state(hbd)
            for h in range(HG_HEADS):
                sout_ref[ck, h] = s_new[:, h * HEAD_D:(h + 1) * HEAD_D]
    if chained:
        h_scr[...] = hbd

        @pl.when(pl.program_id(1) == pl.num_programs(1) - 1)
        def _():
            s_new = _fold_state(hbd)
            for h in range(HG_HEADS):
                sout_ref[h] = s_new[:, h * HEAD_D:(h + 1) * HEAD_D]

    o = o_intra + jnp.concatenate(o_parts, axis=0)
    ms = _dot_xr(o * o, ones_h) * (1.0 / HEAD_D)
    on = o * lax.rsqrt(ms + EPS) * gn_ref[...]
    o_ref[...] = (on * (gate * _sigmoid(gate))).astype(o_ref.dtype)


def _hgrn(p_hg, lb_logits, gn, s0, layer, B, T, chained):
    n = p_hg.shape[0]
    if chained:
        L, C = min(256, T), min(16, T)
        nblk = T // L
        grid = (B, nblk)
        rowmap = lambda b, j: (b * nblk + j, 0)
        in_specs = [pl.BlockSpec((L, 1024), rowmap), _full(lb_logits.shape), _full(gn.shape)]
        args = [p_hg, lb_logits, gn]
        out_specs = [pl.BlockSpec((L, MIX_W), rowmap),
                     pl.BlockSpec((None, HG_HEADS, HEAD_D, HEAD_D), lambda b, j: (b, 0, 0, 0))]
        scratch = [pltpu.VMEM((MIX_W, MIX_W), F32)]
        sem = ("parallel", "arbitrary")
    else:
        L, C = n, T
        grid = (1,)
        in_specs = [_full((L, 1024)), _full(lb_logits.shape), _full(gn.shape), _full(s0.shape)]
        args = [p_hg, lb_logits, gn, s0]
        out_specs = [_full((L, MIX_W)), _full((B, HG_HEADS, HEAD_D, HEAD_D))]
        scratch = []
        sem = ("arbitrary",)
    return pl.pallas_call(
        functools.partial(_hgrn_kernel, L=L, C=C, layer=layer, chained=chained),
        grid=grid, in_specs=in_specs, out_specs=out_specs,
        out_shape=[jax.ShapeDtypeStruct((n, MIX_W), BF16),
                   jax.ShapeDtypeStruct((B, HG_HEADS, HEAD_D, HEAD_D), F32)],
        scratch_shapes=scratch, compiler_params=_params(sem), name="hgrn")(*args)


def _rwkv_kernel(*refs, L, C, chained):
    it = iter(refs)
    p_ref = next(it)
    if not chained:
        shrow_ref, s0_ref = next(it), next(it)
    (mu_ref, w0_ref, w2_ref, a0_ref, a2_ref, g2_ref, kkw_ref, ka_ref, rk_ref, lnw_ref, lnb_ref) = (next(it) for _ in range(11))
    y_ref, sout_ref = next(it), next(it)
    if chained:
        h_scr, carry_scr = next(it), next(it)

    nck = L // C
    p = p_ref[...]
    rowi = _iota((L, 1), 0)
    rolled = pltpu.roll(p, 1, 0)
    if chained:
        first = pl.program_id(1) == 0

        @pl.when(first)
        def _():
            h_scr[...] = jnp.zeros((MIX_W, MIX_W), F32)
            carry_scr[...] = jnp.zeros(carry_scr.shape, F32)
        prev = jnp.where(rowi == 0, carry_scr[0:1, :], rolled)
        carry_scr[0:1, :] = p[L - 1:L, :]
    else:
        prev = jnp.where((rowi & (C - 1)) == 0, shrow_ref[...], rolled)
    ps = p + (prev - p) * mu_ref[...]
    r, k, v, tail = ps[:, 0:256], ps[:, 256:512], ps[:, 512:768], ps[:, 768:896]

    w_raw = -jax.nn.softplus(-(w0_ref[...] + _dot(jnp.tanh(tail), w2_ref[...]))) - 0.5
    lw = -jnp.exp(w_raw)
    a = _sigmoid(a0_ref[...] + _dot(tail, a2_ref[...]))
    g = _dot(_sigmoid(tail), g2_ref[...])

    ones_h = _head_block_ones(MIX_W, HEAD_D)
    kk = k * kkw_ref[...]
    kk = kk / jnp.maximum(jnp.sqrt(_dot_xr(kk * kk, ones_h)), 1e-12)
    k2 = k * (1.0 + (a - 1.0) * ka_ref[...])
    bb = kk * a

    same, incl, strict = _chunk_masks(L, C)
    gc = _dot_xl(incl.astype(BF16), lw)
    g_last = _dot_xl(same.astype(BF16), lw)
    eg, eng = jnp.exp(gc), jnp.exp(-gc)
    rg, kg, bg, kkg = r * eg, k2 * eng, bb * eng, kk * jnp.exp(gc - lw)
    to_end = jnp.exp(g_last - gc)
    kd_t, bd_t, e_t = (k2 * to_end).T, (bb * to_end).T, jnp.exp(g_last).T

    lane_h = _iota((1, MIX_W), 1) // HEAD_D
    eye = (_iota((L, L), 0) == _iota((L, L), 1)).astype(F32)
    u0 = jnp.zeros((L, MIX_W), F32)
    kkt = jnp.zeros((L, MIX_W), F32)
    qp = rg
    o0 = jnp.zeros((L, MIX_W), F32)
    for h in range(RW_HEADS):
        mh = lane_h == h
        kkg_h, rg_h = jnp.where(mh, kkg, 0.0), jnp.where(mh, rg, 0.0)
        a_kb = jnp.where(strict, _dot_nt(kkg_h, bg), 0.0)
        a_kk = jnp.where(strict, _dot_nt(kkg_h, kg), 0.0)
        a_rk = jnp.where(incl, _dot_nt(rg_h, kg), 0.0)
        a_rb = jnp.where(incl, _dot_nt(rg_h, bg), 0.0)
        npow = -a_kb
        tinv = eye + npow
        for _ in range(int(math.log2(C)) - 1):
            npow = _dot(npow, npow)
            tinv = tinv + _dot(tinv, npow)
        u0_h = _dot(tinv, _dot(a_kk, v))
        kkt_h = _dot(tinv, kkg)
        u0 = jnp.where(mh, u0_h, u0)
        kkt = jnp.where(mh, kkt_h, kkt)
        qp = qp - jnp.where(mh, _dot(a_rb, kkt_h), 0.0)
        o0 = o0 + jnp.where(mh, _dot(a_rk, v) - _dot(a_rb, u0_h), 0.0)

    bd = (_iota((MIX_W, MIX_W), 0) // HEAD_D) == (_iota((MIX_W, MIX_W), 1) // HEAD_D)
    col = _iota((1, L), 1)
    if chained:
        hbd = h_scr[...]
    y_parts = []
    for ck in range(nck):
        if not chained:
            hbd = _expand_state(s0_ref[ck].T)
        y_parts.append(_dot(qp[ck * C:(ck + 1) * C], hbd))
        cm = (col // C) == ck
        bd_c = jnp.where(cm, bd_t, 0.0)
        mb = jnp.where(bd, _dot(bd_c, kkt), 0.0)
        gb = jnp.where(bd, _dot(jnp.where(cm, kd_t, 0.0), v) - _dot(bd_c, u0), 0.0)
        dcol = jnp.sum(jnp.where(col == ck * C, e_t, 0.0), axis=1, keepdims=True)
        hbd = dcol * hbd - _dot(mb, hbd) + gb
        if not chained:
            sout_ref[ck] = _fold_state(hbd).T
    if chained:
        h_scr[...] = hbd

        @pl.when(pl.program_id(1) == pl.num_programs(1) - 1)
        def _():
            sout_ref[...] = _fold_state(hbd).T

    y = o0 + jnp.concatenate(y_parts, axis=0)
    mean = _dot_xr(y, ones_h) * (1.0 / HEAD_D)
    yc = y - mean
    var = _dot_xr(yc * yc, ones_h) * (1.0 / HEAD_D)
    yn = yc * lax.rsqrt(var + RW_GN_EPS) * lnw_ref[...] + lnb_ref[...]
    bonus = _dot_xr(r * k2 * rk_ref[...], ones_h) * v
    y_ref[...] = ((yn + bonus) * g).astype(y_ref.dtype)


def _rwkv(p_rw, shift_rows, s0, rw, B, T, chained):
    n = p_rw.shape[0]
    names = ("mu", "w0", "w2", "a0", "a2", "g2", "kk", "ka", "rk", "lnw", "lnb")
    wargs = [rw[k] for k in names]
    wspecs = [_full(a.shape) for a in wargs]
    if chained:
        L, C = min(256, T), min(64, T)
        nblk = T // L
        grid = (B, nblk)
        rowmap = lambda b, j: (b * nblk + j, 0)
        in_specs = [pl.BlockSpec((L, RW_COLS), rowmap)] + wspecs
        args = [p_rw] + wargs
        out_specs = [pl.BlockSpec((L, MIX_W), rowmap), pl.BlockSpec((None, MIX_W, HEAD_D), lambda b, j: (b, 0, 0))]
        scratch = [pltpu.VMEM((MIX_W, MIX_W), F32), pltpu.VMEM((8, RW_COLS), F32)]
        sem = ("parallel", "arbitrary")
    else:
        L, C = n, T
        grid = (1,)
        in_specs = [_full((L, RW_COLS)), _full(shift_rows.shape), _full(s0.shape)] + wspecs
        args = [p_rw, shift_rows, s0] + wargs
        out_specs = [_full((L, MIX_W)), _full((B, MIX_W, HEAD_D))]
        scratch = []
        sem = ("arbitrary",)
    return pl.pallas_call(
        functools.partial(_rwkv_kernel, L=L, C=C, chained=chained),
        grid=grid, in_specs=in_specs, out_specs=out_specs,
        out_shape=[jax.ShapeDtypeStruct((n, MIX_W), BF16), jax.ShapeDtypeStruct((B, MIX_W, HEAD_D), F32)],
        scratch_shapes=scratch, compiler_params=_params(sem), name="rwkv")(*args)


def _attn_prompt_kernel(q_ref, k_ref, v_ref, o_ref, *, tq):
    i = pl.program_id(2)
    qs = [q_ref[:, hh * HEAD_SLOT:(hh + 1) * HEAD_SLOT] for hh in range(2)]

    def step(j, carry, diagonal):
        start = pl.multiple_of(j * tq, tq)
        vb = v_ref[pl.ds(start, tq), :]
        new = []
        for hh in range(2):
            m, l, acc = carry[hh]
            kb = k_ref[pl.ds(start, tq), hh * HEAD_SLOT:(hh + 1) * HEAD_SLOT]
            s = lax.dot_general(qs[hh], kb, (((1,), (1,)), ((), ())), preferred_element_type=F32)
            if diagonal:
                s = jnp.where(_iota((tq, tq), 1) <= _iota((tq, tq), 0), s, NEG_INF)
            m_new = jnp.maximum(m, jnp.max(s, axis=-1, keepdims=True))
            pexp = jnp.exp2(s - m_new)
            alpha = jnp.exp2(m - m_new)
            l = alpha * l + jnp.sum(pexp, axis=-1, keepdims=True)
            acc = alpha * acc + jnp.dot(pexp.astype(BF16), vb, preferred_element_type=F32)
            new.append((m_new, l, acc))
        return tuple(new)

    one = (jnp.full((tq, 1), NEG_INF, F32), jnp.zeros((tq, 1), F32), jnp.zeros((tq, 2 * MLA_VDIM), F32))
    carry = lax.fori_loop(0, i, lambda j, c: step(j, c, False), (one, one))
    (_, l0, acc0), (_, l1, acc1) = step(i, carry, True)
    lane = _iota((tq, 2 * MLA_VDIM), 1)
    o_ref[...] = jnp.where(lane < MLA_VDIM, acc0 / l0, acc1 / l1).astype(o_ref.dtype)


def _attn_prompt(q, k, v, B, T, tq=1024):
    n = q.shape[0]
    tq = min(tq, T)
    nq = T // tq
    return pl.pallas_call(
        functools.partial(_attn_prompt_kernel, tq=tq),
        grid=(B, MLA_HEADS // 2, nq),
        in_specs=[pl.BlockSpec((tq, 2 * HEAD_SLOT), lambda b, hp, i: (b * nq + i, hp)),
                  pl.BlockSpec((T, 2 * HEAD_SLOT), lambda b, hp, i: (b, hp)),
                  pl.BlockSpec((T, 2 * MLA_VDIM), lambda b, hp, i: (b, hp))],
        out_specs=pl.BlockSpec((tq, 2 * MLA_VDIM), lambda b, hp, i: (b * nq + i, hp)),
        out_shape=jax.ShapeDtypeStruct((n, MLA_HEADS * MLA_VDIM), BF16),
        compiler_params=_params(("parallel", "parallel", "arbitrary")), name="attn_prompt")(q, k, v)


PAGE_GROUP = 16


def _attn_sample_kernel(pt_ref, qlat_ref, q_ref, c_ref, kpe_ref, wuv_ref, ckv_hbm, kpe_hbm, o_ref,
                        cbuf, kbuf, sem, *, T, layer, n_pages):
    b = pl.program_id(0)
    G = PAGE_GROUP
    n_groups = n_pages // G
    R = MLA_HEADS * T

    def page_copies(g, slot):
        out = []
        for i in range(G):
            pg = pt_ref[b, g * G + i]
            out.append(pltpu.make_async_copy(ckv_hbm.at[layer, pg], cbuf.at[slot, i], sem.at[0, slot]))
            out.append(pltpu.make_async_copy(kpe_hbm.at[layer, pg], kbuf.at[slot, i], sem.at[1, slot]))
        return out

    for cp in page_copies(0, 0):
        cp.start()

    ql = jnp.concatenate([qlat_ref[:, h * MLA_KV_RANK:(h + 1) * MLA_KV_RANK] for h in range(MLA_HEADS)], axis=0).astype(BF16)
    qs = jnp.concatenate([q_ref[:, h * HEAD_SLOT:(h + 1) * HEAD_SLOT] for h in range(MLA_HEADS)], axis=0)
    qp = qs[:, ROPE_LANE0:ROPE_LANE0 + MLA_ROPE].astype(BF16)

    def update(carry, s, cb):
        m, l, acc = carry
        m_new = jnp.maximum(m, jnp.max(s, axis=-1, keepdims=True))
        pexp = jnp.exp(s - m_new)
        alpha = jnp.exp(m - m_new)
        l = alpha * l + jnp.sum(pexp, axis=-1, keepdims=True)
        acc = alpha * acc + jnp.dot(pexp.astype(BF16), cb, preferred_element_type=F32)
        return m_new, l, acc

    def group(g, carry):
        slot = g & 1

        @pl.when(g + 1 < n_groups)
        def _():
            for cp in page_copies(g + 1, 1 - slot):
                cp.start()
        for cp in page_copies(g, slot):
            cp.wait()
        cb = cbuf[slot].reshape(G * PAGE_SIZE, MLA_KV_RANK).astype(BF16)
        kt = kbuf[slot].astype(BF16)
        s = _dot_nt(ql, cb) + jnp.concatenate(
            [jnp.dot(qp, kt[i], preferred_element_type=F32) for i in range(G)], axis=-1)
        return update(carry, s, cb)

    init = (jnp.full((R, 1), NEG_INF, F32), jnp.zeros((R, 1), F32), jnp.zeros((R, MLA_KV_RANK), F32))
    carry = lax.fori_loop(0, n_groups, group, init)

    pad = PAGE_SIZE - T
    cn = jnp.concatenate([c_ref[...], jnp.zeros((pad, MLA_KV_RANK), F32)], axis=0).astype(BF16)
    kn = jnp.concatenate([kpe_ref[...], jnp.zeros((pad, MLA_ROPE), F32)], axis=0).astype(BF16)
    tq = _iota((R, PAGE_SIZE), 0) & (T - 1)
    s = jnp.where(_iota((R, PAGE_SIZE), 1) <= tq, _dot_nt(ql, cn) + _dot_nt(qp, kn), NEG_INF)
    _, l, acc = update(carry, s, cn)
    z = _dot(acc / l, wuv_ref[...])
    lane_h = _iota((T, MLA_HEADS * MLA_VDIM), 1) // MLA_VDIM
    out = jnp.zeros((T, MLA_HEADS * MLA_VDIM), F32)
    for h in range(MLA_HEADS):
        out = jnp.where(lane_h == h, z[h * T:(h + 1) * T], out)
    o_ref[...] = out.astype(o_ref.dtype)


def _attn_sample(qlat, q, c, kpe, wuv, cache_ckv, cache_kpe_t, page_table, layer, B, T):
    n = qlat.shape[0]
    n_pages = page_table.shape[1]
    row = lambda w: pl.BlockSpec((T, w), lambda b, pt: (b, 0))
    hbm = pl.BlockSpec(memory_space=pl.ANY)
    grid_spec = pltpu.PrefetchScalarGridSpec(
        num_scalar_prefetch=1, grid=(B,),
        in_specs=[row(qlat.shape[1]), row(q.shape[1]), row(MLA_KV_RANK), row(MLA_ROPE),
                  pl.BlockSpec(wuv.shape, lambda b, pt: (0, 0)), hbm, hbm],
        out_specs=row(MLA_HEADS * MLA_VDIM),
        scratch_shapes=[pltpu.VMEM((2, PAGE_GROUP, PAGE_SIZE, MLA_KV_RANK), F32),
                        pltpu.VMEM((2, PAGE_GROUP, MLA_ROPE, PAGE_SIZE), F32),
                        pltpu.SemaphoreType.DMA((2, 2))])
    return pl.pallas_call(
        functools.partial(_attn_sample_kernel, T=T, layer=layer, n_pages=n_pages), grid_spec=grid_spec,
        out_shape=jax.ShapeDtypeStruct((n, MLA_HEADS * MLA_VDIM), F32),
        compiler_params=_params(("arbitrary",)), name="attn_sample")(
            page_table, qlat, q, c, kpe, wuv, cache_ckv, cache_kpe_t)


def _rope_rows():
    half = MLA_ROPE // 2
    inv = jnp.exp(-jnp.log(ROPE_THETA) * jnp.arange(half, dtype=F32) / half)
    lane = np.arange(LANES)
    on = (lane >= ROPE_LANE0) & (lane < ROPE_LANE0 + MLA_ROPE)
    inv_l = jnp.where(on, inv[(lane - ROPE_LANE0) % half], 0.0)
    sign = np.where(on, np.where(lane < ROPE_LANE0 + half, -1.0, 1.0), 0.0)
    rows = [inv_l, (lane < ROPE_LANE0).astype(np.float32), sign, on.astype(np.float32)]
    rows += [np.zeros(LANES, np.float32)] * 4
    return jnp.stack([jnp.asarray(r, F32) for r in rows])


def _layer_weights(l, W):
    half = MLA_ROPE // 2
    r2 = lambda a: a.reshape(1, -1)
    w_in = W["w_in"][l]
    whg, wrw, wmla = w_in[:, :1024], w_in[:, 1024:1920], w_in[:, 1920:]
    wkpe = wmla[:, MLA_Q_RANK + MLA_KV_RANK:]
    z = lambda n: jnp.zeros((D_MODEL, n), F32)
    wmla_pad = jnp.concatenate([wmla[:, :MLA_Q_RANK + MLA_KV_RANK], z(ROPE_LANE0), wkpe, z(HEAD_SLOT - ROPE_LANE0 - MLA_ROPE),
                                z(ROPE_LANE0), wkpe[:, half:], wkpe[:, :half], z(HEAD_SLOT - ROPE_LANE0 - MLA_ROPE)], axis=1)
    wqb = W["mla_wqb"][l].reshape(MLA_Q_RANK, MLA_HEADS, MLA_NOPE + MLA_ROPE)
    zq = lambda n: jnp.zeros((MLA_Q_RANK, MLA_HEADS, n), F32)
    rest = HEAD_SLOT - MLA_NOPE - MLA_ROPE
    wq_pad = jnp.concatenate([wqb, zq(rest)], axis=-1)
    wq_rot = jnp.concatenate([zq(MLA_NOPE), wqb[..., MLA_NOPE + half:], wqb[..., MLA_NOPE:MLA_NOPE + half], zq(rest)], axis=-1)
    hw = MLA_HEADS * HEAD_SLOT
    wq = jnp.concatenate([wq_pad.reshape(MLA_Q_RANK, hw), wq_rot.reshape(MLA_Q_RANK, hw)], axis=1)
    wuk = W["mla_wuk"][l]
    wuk_pad = jnp.concatenate([wuk, jnp.zeros((MLA_KV_RANK, MLA_HEADS, HEAD_SLOT - MLA_NOPE), F32)], axis=-1).reshape(MLA_KV_RANK, hw)
    wukt = wuk.transpose(1, 2, 0).reshape(MLA_HEADS * MLA_NOPE, MLA_KV_RANK)
    wuv = W["mla_wuv"][l].reshape(MLA_KV_RANK, MLA_HEADS * MLA_VDIM)
    proj = dict(g=r2(W["norm_mix"][l]), whg=whg.astype(BF16), wrw=wrw.astype(BF16), wmla=wmla_pad.astype(BF16),
                qn=r2(W["mla_q_norm"][l]), wq=wq.astype(BF16), kvn=r2(W["mla_kv_norm"][l]), rope=_rope_rows(),
                wuk=wuk_pad.astype(BF16), wuv=wuv.astype(BF16), wukt=wukt.astype(BF16))
    zr = lambda n: jnp.zeros((n, RW_WIDTH), F32)
    rwk = dict(mu=r2(W["rw_mu"][l]), w0=r2(W["rw_w0"][l]),
               w2=jnp.concatenate([W["rw_w2"][l], zr(96)], axis=0).astype(BF16),
               a0=r2(W["rw_a0"][l]),
               a2=jnp.concatenate([zr(32), W["rw_a2"][l], zr(64)], axis=0).astype(BF16),
               g2=jnp.concatenate([zr(64), W["rw_g2"][l]], axis=0).astype(BF16),
               kk=r2(W["rw_kk"][l]), ka=r2(W["rw_ka"][l]), rk=r2(W["rw_rk"][l]),
               lnw=r2(W["rw_ln_w"][l]), lnb=r2(W["rw_ln_b"][l]))
    return dict(
        proj=proj, rwkv=rwk,
        ffn1=(r2(W["norm_ffn1"][l]), W["ffn1_wi"][l].astype(BF16), W["ffn1_wo"][l].astype(BF16)),
        ffn2=(r2(W["norm_ffn2"][l]), W["ffn2_wi"][l].astype(BF16), W["ffn2_wo"][l].astype(BF16)),
        wout=W["w_out"][l].astype(BF16), hgn=r2(jnp.tile(W["hg_norm"][l], HG_HEADS)))


def _trunk(x, B, T, pos0, LW, W, sample, st=None):
    n = B * T
    x = x.reshape(n, D_MODEL)
    hg_out, rw_out, sh_out, c_out, kpe_out = [], [], [], [], []
    for l in range(DEPTH):
        lw = LW[l]
        x = _ffn(x, *lw["ffn1"])
        if sample:
            p_hg, p_rw, q, qlat, c, kpe = _proj(x, lw["proj"], T, pos0, True)
            o_hg, hg_s = _hgrn(p_hg, W["hg_lb_logits"], lw["hgn"], st["hg"][l], l, B, T, False)
            shift_rows = jnp.repeat(st["sh"][l], T, axis=0)
            o_rw, rw_s = _rwkv(p_rw, shift_rows, st["rw"][l].reshape(B, MIX_W, HEAD_D), lw["rwkv"], B, T, False)
            o_mla = _attn_sample(qlat, q, c, kpe, lw["proj"]["wuv"], st["ckv"], st["kpe"], st["pt"], l, B, T)
        else:
            p_hg, p_rw, q, k, v, c, kpe = _proj(x, lw["proj"], T, pos0, False)
            o_hg, hg_s = _hgrn(p_hg, W["hg_lb_logits"], lw["hgn"], None, l, B, T, True)
            o_rw, rw_s = _rwkv(p_rw, None, None, lw["rwkv"], B, T, True)
            o_mla = _attn_prompt(q, k, v, B, T)
        gf = W["norm_final"].reshape(1, -1) if l == DEPTH - 1 else None
        x = _ffn(x, *lw["ffn2"], mix=(o_hg, o_rw, o_mla), wout=lw["wout"], gf=gf)
        hg_out.append(hg_s)
        rw_out.append(rw_s.reshape(B, RW_HEADS, RW_DH, RW_DH))
        sh_out.append(p_rw.reshape(B, T, RW_COLS)[:, -1])
        c_out.append(c.reshape(B, T, MLA_KV_RANK))
        kpe_out.append(kpe.reshape(B, T, MLA_ROPE))
    st_ = jnp.stack
    return x.reshape(B, T, D_MODEL), st_(hg_out), st_(rw_out), st_(sh_out), st_(c_out), st_(kpe_out)


def kernel(x_prompt, x_sample, cache_mla_ckv, cache_mla_kpe, state_hgrn, state_rwkv, state_rwkv_shift, page_table, norm_ffn1, ffn1_wi, ffn1_wo, norm_mix, w_in, w_out, hg_lb_logits, hg_norm, rw_mu, rw_w0, rw_w2, rw_a0, rw_a2, rw_g2, rw_kk, rw_ka, rw_rk, rw_ln_w, rw_ln_b, mla_q_norm, mla_wqb, mla_kv_norm, mla_wuk, mla_wuv, norm_ffn2, ffn2_wi, ffn2_wo, norm_final):
    W = dict(norm_ffn1=norm_ffn1, ffn1_wi=ffn1_wi, ffn1_wo=ffn1_wo, norm_mix=norm_mix, w_in=w_in, w_out=w_out,
             hg_lb_logits=hg_lb_logits, hg_norm=hg_norm, rw_mu=rw_mu, rw_w0=rw_w0, rw_w2=rw_w2, rw_a0=rw_a0,
             rw_a2=rw_a2, rw_g2=rw_g2, rw_kk=rw_kk, rw_ka=rw_ka, rw_rk=rw_rk, rw_ln_w=rw_ln_w, rw_ln_b=rw_ln_b,
             mla_q_norm=mla_q_norm, mla_wqb=mla_wqb, mla_kv_norm=mla_kv_norm, mla_wuk=mla_wuk, mla_wuv=mla_wuv,
             norm_ffn2=norm_ffn2, ffn2_wi=ffn2_wi, ffn2_wo=ffn2_wo, norm_final=norm_final)
    LW = [_layer_weights(l, W) for l in range(DEPTH)]
    Bp, Tp, _ = x_prompt.shape
    y_p, hg_p, rw_p, sh_p, ckv_p, kpe_p = _trunk(x_prompt, Bp, Tp, 0, LW, W, False)
    Bs, Ts, _ = x_sample.shape
    past_len = page_table.shape[1] * cache_mla_ckv.shape[2]
    st = dict(hg=state_hgrn, rw=state_rwkv, sh=state_rwkv_shift, ckv=cache_mla_ckv,
              kpe=jnp.swapaxes(cache_mla_kpe, 2, 3), pt=page_table)
    y_s, hg_s, rw_s, sh_s, ckv_s, kpe_s = _trunk(x_sample, Bs, Ts, past_len, LW, W, True, st)
    return (y_p, y_s, hg_p, hg_s, rw_p, rw_s, sh_p, sh_s, ckv_p, ckv_s, kpe_p, kpe_s)
```

```python
import functools
import math

import jax
import jax.numpy as jnp
import numpy as np
from jax import lax
from jax.experimental import pallas as pl
from jax.experimental.pallas import tpu as pltpu

F32 = jnp.float32
BF16 = jnp.bfloat16

D_MODEL = 1024
DEPTH = 2
PAGE_SIZE = 128
EPS = 1e-6
D_FF = 2816
HG_HEADS = 4
HG_DK = 64
HG_KW = 256
HG_MIN_F = 1e-30
RW_HEADS = 4
RW_DH = 64
RW_WIDTH = 256
RW_COLS = 896
RW_GN_EPS = 64e-5
MLA_HEADS = 8
MLA_NOPE = 64
MLA_ROPE = 32
MLA_VDIM = 64
MLA_Q_RANK = 384
MLA_KV_RANK = 256
MLA_SCALE = (MLA_NOPE + MLA_ROPE) ** -0.5
ROPE_THETA = 10000.0

LANES = 128
HEAD_SLOT = 128
ROPE_LANE0 = 64
VMEM_LIMIT = 56 * 1024 * 1024
NEG_INF = -1e30


def _dot(a, b):
    return jnp.dot(a.astype(BF16), b.astype(BF16), preferred_element_type=F32)


def _dot_nt(a, b):
    return lax.dot_general(a.astype(BF16), b.astype(BF16), (((1,), (1,)), ((), ())), preferred_element_type=F32)


def _dot_xl(m, b, pieces=2):
    out = None
    rem = b
    for _ in range(pieces):
        part = rem.astype(BF16)
        term = jnp.dot(m, part, preferred_element_type=F32)
        out = term if out is None else out + term
        rem = rem - part.astype(F32)
    return out


def _dot_xr(a, m, pieces=2):
    out = None
    rem = a
    for _ in range(pieces):
        part = rem.astype(BF16)
        term = jnp.dot(part, m, preferred_element_type=F32)
        out = term if out is None else out + term
        rem = rem - part.astype(F32)
    return out


def _rms(x, g, eps=EPS):
    return x * lax.rsqrt(jnp.mean(x * x, axis=-1, keepdims=True) + eps) * g


def _sigmoid(x):
    return 1.0 / (1.0 + jnp.exp(-x))


def _iota(shape, dim):
    return lax.broadcasted_iota(jnp.int32, shape, dim)


def _head_block_ones(n, hd):
    return (_iota((n, n), 0) // hd == _iota((n, n), 1) // hd).astype(BF16)


def _full(shape):
    return pl.BlockSpec(shape, lambda *_: (0,) * len(shape), pipeline_mode=pl.Buffered(1))


def _params(sem):
    return pltpu.CompilerParams(dimension_semantics=sem, vmem_limit_bytes=VMEM_LIMIT)


FF_CHUNK = 256


def _ffn_kernel(*refs, has_mix, final):
    it = iter(refs)
    x_ref = next(it)
    if has_mix:
        ohg_ref, orw_ref, omla_ref, wout_ref = next(it), next(it), next(it), next(it)
    g_ref, wi_ref, wo_ref = next(it), next(it), next(it)
    gf_ref = next(it) if final else None
    o_ref = next(it)

    x = x_ref[...]
    if has_mix:
        o = jnp.concatenate([r[...].astype(BF16) for r in (ohg_ref, orw_ref, omla_ref)], axis=-1)
        x = x + jnp.dot(o, wout_ref[...], preferred_element_type=F32)
    xn = _rms(x, g_ref[...]).astype(BF16)
    acc = jnp.zeros(x.shape, F32)
    for c in range(D_FF // FF_CHUNK):
        lo = c * FF_CHUNK
        a = jnp.dot(xn, wi_ref[:, lo:lo + FF_CHUNK], preferred_element_type=F32)
        b = jnp.dot(xn, wi_ref[:, D_FF + lo:D_FF + lo + FF_CHUNK], preferred_element_type=F32)
        h = (a * _sigmoid(a) * b).astype(BF16)
        acc = acc + jnp.dot(h, wo_ref[lo:lo + FF_CHUNK, :], preferred_element_type=F32)
    y = x + 0.5 * acc
    if final:
        y = _rms(y, gf_ref[...])
    o_ref[...] = y


def _ffn(x, g, wi, wo, mix=None, wout=None, gf=None, tm=512):
    n = x.shape[0]
    tm = min(tm, n)
    row = lambda w: pl.BlockSpec((tm, w), lambda i: (i, 0))
    args, specs = [x], [row(D_MODEL)]
    if mix is not None:
        for m in mix:
            args.append(m)
            specs.append(row(m.shape[1]))
        args.append(wout)
        specs.append(_full(wout.shape))
    args += [g, wi, wo]
    specs += [_full(g.shape), _full(wi.shape), _full(wo.shape)]
    if gf is not None:
        args.append(gf)
        specs.append(_full(gf.shape))
    return pl.pallas_call(
        functools.partial(_ffn_kernel, has_mix=mix is not None, final=gf is not None),
        grid=(n // tm,), in_specs=specs, out_specs=row(D_MODEL),
        out_shape=jax.ShapeDtypeStruct((n, D_MODEL), F32),
        compiler_params=_params(("parallel",)), name="ffn")(*args)


MLA_PAD = 896


def _proj_kernel(*refs, seq_len, pos0, sample):
    it = iter(refs)
    x_ref, g_ref, whg_ref, wrw_ref, wmla_ref = (next(it) for _ in range(5))
    qn_ref, wq_ref, kvn_ref, rope_ref = (next(it) for _ in range(4))
    if sample:
        wukt_ref = next(it)
    else:
        wuk_ref, wuv_ref = next(it), next(it)
    phg_ref, prw_ref, q_ref = next(it), next(it), next(it)
    if sample:
        qlat_ref = next(it)
    else:
        k_ref, v_ref = next(it), next(it)
    c_ref, kpe_ref = next(it), next(it)

    tm = x_ref.shape[0]
    xn = _rms(x_ref[...], g_ref[...]).astype(BF16)
    phg_ref[...] = jnp.dot(xn, whg_ref[...], preferred_element_type=F32)
    prw_ref[...] = jnp.dot(xn, wrw_ref[...], preferred_element_type=F32)
    pm = jnp.dot(xn, wmla_ref[...], preferred_element_type=F32)

    row = pl.program_id(0) * tm + _iota((tm, 1), 0)
    pos = (pos0 + (row & (seq_len - 1))).astype(F32)
    rope = rope_ref[...]
    ang = pos * rope[0:1, :]
    cos_s = rope[1:2, :] + rope[3:4, :] * jnp.cos(ang)
    sin_s = rope[2:3, :] * jnp.sin(ang)

    c = _rms(pm[:, MLA_Q_RANK:MLA_Q_RANK + MLA_KV_RANK], kvn_ref[...])
    c_ref[...] = c
    k_slot = pm[:, 640:768] * cos_s + pm[:, 768:896] * sin_s
    kpe_ref[...] = k_slot[:, ROPE_LANE0:ROPE_LANE0 + MLA_ROPE]

    qn = _rms(pm[:, :MLA_Q_RANK], qn_ref[...]).astype(BF16)
    q2 = jnp.dot(qn, wq_ref[...], preferred_element_type=F32)
    hw = MLA_HEADS * HEAD_SLOT
    q_scale = MLA_SCALE if sample else MLA_SCALE * math.log2(math.e)
    cos_q = jnp.concatenate([cos_s * q_scale] * MLA_HEADS, axis=-1)
    sin_q = jnp.concatenate([sin_s * q_scale] * MLA_HEADS, axis=-1)
    q = q2[:, :hw] * cos_q + q2[:, hw:] * sin_q
    if sample:
        q_ref[...] = q
        for h in range(MLA_HEADS):
            qh = q[:, h * HEAD_SLOT:h * HEAD_SLOT + MLA_NOPE]
            qlat_ref[:, h * MLA_KV_RANK:(h + 1) * MLA_KV_RANK] = _dot(qh, wukt_ref[h * MLA_NOPE:(h + 1) * MLA_NOPE, :])
    else:
        q_ref[...] = q.astype(BF16)
        cb = c.astype(BF16)
        kn = jnp.dot(cb, wuk_ref[...], preferred_element_type=F32)
        k_ref[...] = (kn + jnp.concatenate([k_slot] * MLA_HEADS, axis=-1)).astype(BF16)
        v_ref[...] = jnp.dot(cb, wuv_ref[...], preferred_element_type=F32).astype(BF16)


def _proj(x, pw, seq_len, pos0, sample, tm=512):
    n = x.shape[0]
    tm = min(tm, n)
    row = lambda w: pl.BlockSpec((tm, w), lambda i: (i, 0))
    hw = MLA_HEADS * HEAD_SLOT
    args = [x, pw["g"], pw["whg"], pw["wrw"], pw["wmla"], pw["qn"], pw["wq"], pw["kvn"], pw["rope"]]
    args += [pw["wukt"]] if sample else [pw["wuk"], pw["wuv"]]
    specs = [row(D_MODEL)] + [_full(a.shape) for a in args[1:]]
    qdt = F32 if sample else BF16
    outs = [((n, 1024), F32), ((n, RW_COLS), F32), ((n, hw), qdt)]
    outs += [((n, MLA_HEADS * MLA_KV_RANK), F32)] if sample else [((n, hw), BF16), ((n, MLA_HEADS * MLA_VDIM), BF16)]
    outs += [((n, MLA_KV_RANK), F32), ((n, MLA_ROPE), F32)]
    return pl.pallas_call(
        functools.partial(_proj_kernel, seq_len=seq_len, pos0=pos0, sample=sample),
        grid=(n // tm,), in_specs=specs, out_specs=[row(s[1]) for s, _ in outs],
        out_shape=[jax.ShapeDtypeStruct(s, d) for s, d in outs],
        compiler_params=_params(("parallel",)), name="proj")(*args)


MIX_W = 256
HEAD_D = 64


def _chunk_masks(L, C):
    r, c = _iota((L, L), 0), _iota((L, L), 1)
    same = (r // C) == (c // C)
    return same, same & (c <= r), same & (c < r)


def _expand_state(s_kv):
    t = jnp.concatenate([s_kv] * (MIX_W // HEAD_D), axis=0)
    bd = (_iota((MIX_W, MIX_W), 0) // HEAD_D) == (_iota((MIX_W, MIX_W), 1) // HEAD_D)
    return jnp.where(bd, t, 0.0)


def _fold_state(hbd):
    out = hbd[0:HEAD_D]
    for h in range(1, MIX_W // HEAD_D):
        out = out + hbd[h * HEAD_D:(h + 1) * HEAD_D]
    return out


def _hgrn_steps(*refs, L, C, layer, chained, own_init=True):
    it = iter(refs)
    p_ref, lbl_ref, gn_ref = next(it), next(it), next(it)
    s0_ref = None if chained else next(it)
    o_ref, sout_ref = next(it), next(it)
    h_scr = next(it) if chained else None

    nck = L // C
    p = p_ref[...]
    q, fz, vi, gate = p[:, 0:256], p[:, 256:512], p[:, 512:768], p[:, 768:1024]

    lg = lbl_ref[...]
    e = jnp.exp(lg - jnp.max(lg, axis=0, keepdims=True))
    pr = e / jnp.sum(e, axis=0, keepdims=True)
    cs = pr[0:1]
    for i in range(1, layer + 1):
        cs = cs + pr[i:i + 1]
    lb = jnp.clip(cs - pr[0:1], 0.0, 1.0)

    f = lb + (1.0 - lb) * _sigmoid(fz)
    log_f = jnp.log(jnp.maximum(f, HG_MIN_F))
    k = (1.0 - lb) * _sigmoid(-fz)

    same, incl, _ = _chunk_masks(L, C)
    b = _dot_xl(incl.astype(BF16), log_f, pieces=3)
    b_last = _dot_xl(same.astype(BF16), log_f, pieces=3)
    ones_h = _head_block_ones(MIX_W, HEAD_D)

    b3, q3, k3, v3 = (a.reshape(nck, C, MIX_W) for a in (b, q, k, vi))
    SUB = 8
    tiles = [jnp.zeros((nck, min(SUB, C), MIX_W), F32) for _ in range(max(C // SUB, 1))]
    for s in range(C):
        t0 = (s // SUB) * SUB
        bt, qt = b3[:, t0:, :], q3[:, t0:, :]
        rows = C - t0
        w = jnp.exp(jnp.minimum(bt - b3[:, s:s + 1, :], 0.0))
        x = jnp.where(t0 + _iota((nck, rows, MIX_W), 1) >= s, qt * w * k3[:, s:s + 1, :], 0.0)
        a_s = _dot(x.reshape(nck * rows, MIX_W), ones_h).reshape(nck, rows, MIX_W)
        upd = a_s * v3[:, s:s + 1, :]
        for ti in range(t0 // SUB, len(tiles)):
            lo = ti * SUB - t0
            tiles[ti] = tiles[ti] + upd[:, lo:lo + SUB, :]
        yield
    o_intra = jnp.concatenate(tiles, axis=1).reshape(L, MIX_W)

    qe = q * jnp.exp(b)
    kd_t = (k * jnp.exp(b_last - b)).T
    e_t = jnp.exp(b_last).T
    bd = (_iota((MIX_W, MIX_W), 0) // HEAD_D) == (_iota((MIX_W, MIX_W), 1) // HEAD_D)
    col = _iota((1, L), 1)

    if chained:
        if own_init:
            @pl.when(pl.program_id(1) == 0)
            def _():
                h_scr[...] = jnp.zeros((MIX_W, MIX_W), F32)
        hbd = h_scr[...]
    gks = []
    for ck in range(nck):
        gks.append(jnp.where(bd, _dot(jnp.where((col // C) == ck, kd_t, 0.0), vi), 0.0))
        yield
    dcols = [jnp.sum(jnp.where(col == ck * C, e_t, 0.0), axis=1, keepdims=True) for ck in range(nck)]
    o_parts = []
    for ck in range(nck):
        if not chained:
            s_kv = jnp.concatenate([s0_ref[ck, h] for h in range(HG_HEADS)], axis=-1)
            hbd = _expand_state(s_kv)
        o_parts.append(_dot(qe[ck * C:(ck + 1) * C], hbd))
        hbd = dcols[ck] * hbd + gks[ck]
        if not chained:
            s_new = _fold_state(hbd)
            for h in range(HG_HEADS):
                sout_ref[ck, h] = s_new[:, h * HEAD_D:(h + 1) * HEAD_D]
    if chained:
        h_scr[...] = hbd
        s_new = _fold_state(hbd)
        for h in range(HG_HEADS):
            sout_ref[h] = s_new[:, h * HEAD_D:(h + 1) * HEAD_D]

    o = o_intra + jnp.concatenate(o_parts, axis=0)
    ms = _dot_xr(o * o, ones_h) * (1.0 / HEAD_D)
    on = o * lax.rsqrt(ms + EPS) * gn_ref[...]
    o_ref[...] = (on * (gate * _sigmoid(gate))).astype(o_ref.dtype)


def _drain(*gens):
    alive = list(gens)
    while alive:
        for g in list(alive):
            try:
                next(g)
            except StopIteration:
                alive.remove(g)


def _hgrn_kernel(*refs, **kw):
    _drain(_hgrn_steps(*refs, **kw))


def _hgrn(p_hg, lb_logits, gn, s0, layer, B, T, chained):
    n = p_hg.shape[0]
    if chained:
        L, C = min(256, T), min(16, T)
        nblk = T // L
        grid = (B, nblk)
        rowmap = lambda b, j: (b * nblk + j, 0)
        in_specs = [pl.BlockSpec((L, 1024), rowmap), _full(lb_logits.shape), _full(gn.shape)]
        args = [p_hg, lb_logits, gn]
        out_specs = [pl.BlockSpec((L, MIX_W), rowmap),
                     pl.BlockSpec((None, HG_HEADS, HEAD_D, HEAD_D), lambda b, j: (b, 0, 0, 0))]
        scratch = [pltpu.VMEM((MIX_W, MIX_W), F32)]
        sem = ("parallel", "arbitrary")
    else:
        L, C = n, T
        grid = (1,)
        in_specs = [_full((L, 1024)), _full(lb_logits.shape), _full(gn.shape), _full(s0.shape)]
        args = [p_hg, lb_logits, gn, s0]
        out_specs = [_full((L, MIX_W)), _full((B, HG_HEADS, HEAD_D, HEAD_D))]
        scratch = []
        sem = ("arbitrary",)
    return pl.pallas_call(
        functools.partial(_hgrn_kernel, L=L, C=C, layer=layer, chained=chained),
        grid=grid, in_specs=in_specs, out_specs=out_specs,
        out_shape=[jax.ShapeDtypeStruct((n, MIX_W), BF16),
                   jax.ShapeDtypeStruct((B, HG_HEADS, HEAD_D, HEAD_D), F32)],
        scratch_shapes=scratch, compiler_params=_params(sem), name="hgrn")(*args)


def _rwkv_steps(*refs, L, C, chained, own_init=True):
    it = iter(refs)
    p_ref = next(it)
    if not chained:
        shrow_ref, s0_ref = next(it), next(it)
    (mu_ref, w0_ref, w2_ref, a0_ref, a2_ref, g2_ref, kkw_ref, ka_ref, rk_ref, lnw_ref, lnb_ref) = (next(it) for _ in range(11))
    y_ref, sout_ref = next(it), next(it)
    if chained:
        h_scr, carry_scr = next(it), next(it)

    nck = L // C
    p = p_ref[...]
    rowi = _iota((L, 1), 0)
    rolled = pltpu.roll(p, 1, 0)
    if chained:
        if own_init:
            @pl.when(pl.program_id(1) == 0)
            def _():
                h_scr[...] = jnp.zeros((MIX_W, MIX_W), F32)
                carry_scr[...] = jnp.zeros(carry_scr.shape, F32)
        prev = jnp.where(rowi == 0, carry_scr[0:1, :], rolled)
        carry_scr[0:1, :] = p[L - 1:L, :]
    else:
        prev = jnp.where((rowi & (C - 1)) == 0, shrow_ref[...], rolled)
    ps = p + (prev - p) * mu_ref[...]
    r, k, v, tail = ps[:, 0:256], ps[:, 256:512], ps[:, 512:768], ps[:, 768:896]

    w_raw = -jax.nn.softplus(-(w0_ref[...] + _dot(jnp.tanh(tail), w2_ref[...]))) - 0.5
    lw = -jnp.exp(w_raw)
    a = _sigmoid(a0_ref[...] + _dot(tail, a2_ref[...]))
    g = _dot(_sigmoid(tail), g2_ref[...])

    ones_h = _head_block_ones(MIX_W, HEAD_D)
    kk = k * kkw_ref[...]
    kk = kk / jnp.maximum(jnp.sqrt(_dot_xr(kk * kk, ones_h)), 1e-12)
    k2 = k * (1.0 + (a - 1.0) * ka_ref[...])
    bb = kk * a

    same, incl, strict = _chunk_masks(L, C)
    gc = _dot_xl(incl.astype(BF16), lw)
    g_last = _dot_xl(same.astype(BF16), lw)
    eg, eng = jnp.exp(gc), jnp.exp(-gc)
    rg, kg, bg, kkg = r * eg, k2 * eng, bb * eng, kk * jnp.exp(gc - lw)
    to_end = jnp.exp(g_last - gc)
    kd_t, bd_t, e_t = (k2 * to_end).T, (bb * to_end).T, jnp.exp(g_last).T

    lane_h = _iota((1, MIX_W), 1) // HEAD_D
    eye = (_iota((L, L), 0) == _iota((L, L), 1)).astype(F32)
    u0 = jnp.zeros((L, MIX_W), F32)
    kkt = jnp.zeros((L, MIX_W), F32)
    qp = rg
    o0 = jnp.zeros((L, MIX_W), F32)
    heads = range(RW_HEADS)
    mhs = [lane_h == h for h in heads]
    kkg_hs = [jnp.where(mh, kkg, 0.0) for mh in mhs]
    rg_hs = [jnp.where(mh, rg, 0.0) for mh in mhs]
    a_kb, a_kk, a_rk, a_rb = [], [], [], []
    for h in heads:
        a_kb.append(jnp.where(strict, _dot_nt(kkg_hs[h], bg), 0.0))
        a_kk.append(jnp.where(strict, _dot_nt(kkg_hs[h], kg), 0.0))
        yield
    for h in heads:
        a_rk.append(jnp.where(incl, _dot_nt(rg_hs[h], kg), 0.0))
        a_rb.append(jnp.where(incl, _dot_nt(rg_hs[h], bg), 0.0))
        yield
    npow = [-a for a in a_kb]
    tinv = [eye + n_ for n_ in npow]
    for _ in range(int(math.log2(C)) - 1):
        npow = [_dot(n_, n_) for n_ in npow]
        yield
        tinv = [t_ + _dot(t_, n_) for t_, n_ in zip(tinv, npow)]
        yield
    x1 = [_dot(a_kk[h], v) for h in heads]
    yield
    u0_hs = [_dot(tinv[h], x1[h]) for h in heads]
    yield
    kkt_hs = [_dot(tinv[h], kkg) for h in heads]
    yield
    for h in heads:
        u0 = jnp.where(mhs[h], u0_hs[h], u0)
        kkt = jnp.where(mhs[h], kkt_hs[h], kkt)
    for h in heads:
        qp = qp - jnp.where(mhs[h], _dot(a_rb[h], kkt_hs[h]), 0.0)
        o0 = o0 + jnp.where(mhs[h], _dot(a_rk[h], v) - _dot(a_rb[h], u0_hs[h]), 0.0)
        yield

    bd = (_iota((MIX_W, MIX_W), 0) // HEAD_D) == (_iota((MIX_W, MIX_W), 1) // HEAD_D)
    col = _iota((1, L), 1)
    if chained:
        hbd = h_scr[...]
    bd_cs = [jnp.where((col // C) == ck, bd_t, 0.0) for ck in range(nck)]
    mbs, gbs = [], []
    for ck in range(nck):
        mbs.append(jnp.where(bd, _dot(bd_cs[ck], kkt), 0.0))
        gbs.append(jnp.where(bd, _dot(jnp.where((col // C) == ck, kd_t, 0.0), v) - _dot(bd_cs[ck], u0), 0.0))
        yield
    dcols = [jnp.sum(jnp.where(col == ck * C, e_t, 0.0), axis=1, keepdims=True) for ck in range(nck)]
    y_parts = []
    for ck in range(nck):
        if not chained:
            hbd = _expand_state(s0_ref[ck].T)
        y_parts.append(_dot(qp[ck * C:(ck + 1) * C], hbd))
        hbd = dcols[ck] * hbd - _dot(mbs[ck], hbd) + gbs[ck]
        yield
        if not chained:
            sout_ref[ck] = _fold_state(hbd).T
    if chained:
        h_scr[...] = hbd
        sout_ref[...] = _fold_state(hbd).T

    y = o0 + jnp.concatenate(y_parts, axis=0)
    mean = _dot_xr(y, ones_h) * (1.0 / HEAD_D)
    yc = y - mean
    var = _dot_xr(yc * yc, ones_h) * (1.0 / HEAD_D)
    yn = yc * lax.rsqrt(var + RW_GN_EPS) * lnw_ref[...] + lnb_ref[...]
    bonus = _dot_xr(r * k2 * rk_ref[...], ones_h) * v
    y_ref[...] = ((yn + bonus) * g).astype(y_ref.dtype)


def _rwkv_kernel(*refs, **kw):
    _drain(_rwkv_steps(*refs, **kw))


def _rwkv(p_rw, shift_rows, s0, rw, B, T, chained):
    n = p_rw.shape[0]
    names = ("mu", "w0", "w2", "a0", "a2", "g2", "kk", "ka", "rk", "lnw", "lnb")
    wargs = [rw[k] for k in names]
    wspecs = [_full(a.shape) for a in wargs]
    if chained:
        L, C = min(256, T), min(64, T)
        nblk = T // L
        grid = (B, nblk)
        rowmap = lambda b, j: (b * nblk + j, 0)
        in_specs = [pl.BlockSpec((L, RW_COLS), rowmap)] + wspecs
        args = [p_rw] + wargs
        out_specs = [pl.BlockSpec((L, MIX_W), rowmap), pl.BlockSpec((None, MIX_W, HEAD_D), lambda b, j: (b, 0, 0))]
        scratch = [pltpu.VMEM((MIX_W, MIX_W), F32), pltpu.VMEM((8, RW_COLS), F32)]
        sem = ("parallel", "arbitrary")
    else:
        L, C = n, T
        grid = (1,)
        in_specs = [_full((L, RW_COLS)), _full(shift_rows.shape), _full(s0.shape)] + wspecs
        args = [p_rw, shift_rows, s0] + wargs
        out_specs = [_full((L, MIX_W)), _full((B, MIX_W, HEAD_D))]
        scratch = []
        sem = ("arbitrary",)
    return pl.pallas_call(
        functools.partial(_rwkv_kernel, L=L, C=C, chained=chained),
        grid=grid, in_specs=in_specs, out_specs=out_specs,
        out_shape=[jax.ShapeDtypeStruct((n, MIX_W), BF16), jax.ShapeDtypeStruct((B, MIX_W, HEAD_D), F32)],
        scratch_shapes=scratch, compiler_params=_params(sem), name="rwkv")(*args)


def _mixers_kernel(*refs, L, hg_c, rw_c, layer):
    hg_in, rw_in = refs[0:3], refs[3:15]
    hg_out, rw_out = refs[15:17], refs[17:19]
    hg_scr, rw_scr = refs[19:20], refs[20:22]

    @pl.when(pl.program_id(1) == 0)
    def _():
        for scr in (*hg_scr, *rw_scr):
            scr[...] = jnp.zeros(scr.shape, F32)
    _drain(_rwkv_steps(*rw_in, *rw_out, *rw_scr, L=L, C=rw_c, chained=True, own_init=False),
           _hgrn_steps(*hg_in, *hg_out, *hg_scr, L=L, C=hg_c, layer=layer, chained=True, own_init=False))


def _mixers_prompt(p_hg, p_rw, lb_logits, gn, rw, layer, B, T):
    n = p_hg.shape[0]
    names = ("mu", "w0", "w2", "a0", "a2", "g2", "kk", "ka", "rk", "lnw", "lnb")
    wargs = [rw[k] for k in names]
    L = min(256, T)
    nblk = T // L
    rowmap = lambda b, j: (b * nblk + j, 0)
    in_specs = [pl.BlockSpec((L, 1024), rowmap), _full(lb_logits.shape), _full(gn.shape),
                pl.BlockSpec((L, RW_COLS), rowmap)] + [_full(a.shape) for a in wargs]
    out_specs = [pl.BlockSpec((L, MIX_W), rowmap),
                 pl.BlockSpec((None, HG_HEADS, HEAD_D, HEAD_D), lambda b, j: (b, 0, 0, 0)),
                 pl.BlockSpec((L, MIX_W), rowmap),
                 pl.BlockSpec((None, MIX_W, HEAD_D), lambda b, j: (b, 0, 0))]
    return pl.pallas_call(
        functools.partial(_mixers_kernel, L=L, hg_c=min(16, T), rw_c=min(64, T), layer=layer),
        grid=(B, nblk), in_specs=in_specs, out_specs=out_specs,
        out_shape=[jax.ShapeDtypeStruct((n, MIX_W), BF16),
                   jax.ShapeDtypeStruct((B, HG_HEADS, HEAD_D, HEAD_D), F32),
                   jax.ShapeDtypeStruct((n, MIX_W), BF16),
                   jax.ShapeDtypeStruct((B, MIX_W, HEAD_D), F32)],
        scratch_shapes=[pltpu.VMEM((MIX_W, MIX_W), F32), pltpu.VMEM((MIX_W, MIX_W), F32), pltpu.VMEM((8, RW_COLS), F32)],
        compiler_params=_params(("parallel", "arbitrary")), name="mixers")(p_hg, lb_logits, gn, p_rw, *wargs)


def _attn_prompt_kernel(q_ref, k_ref, v_ref, o_ref, *, tq):
    i = pl.program_id(2)
    qs = [q_ref[:, hh * HEAD_SLOT:(hh + 1) * HEAD_SLOT] for hh in range(2)]

    def step(j, carry, diagonal):
        start = pl.multiple_of(j * tq, tq)
        vb = v_ref[pl.ds(start, tq), :]
        new = []
        for hh in range(2):
            m, l, acc = carry[hh]
            kb = k_ref[pl.ds(start, tq), hh * HEAD_SLOT:(hh + 1) * HEAD_SLOT]
            s = lax.dot_general(qs[hh], kb, (((1,), (1,)), ((), ())), preferred_element_type=F32)
            if diagonal:
                s = jnp.where(_iota((tq, tq), 1) <= _iota((tq, tq), 0), s, NEG_INF)
            m_new = jnp.maximum(m, jnp.max(s, axis=-1, keepdims=True))
            pexp = jnp.exp2(s - m_new)
            alpha = jnp.exp2(m - m_new)
            l = alpha * l + jnp.sum(pexp, axis=-1, keepdims=True)
            acc = alpha * acc + jnp.dot(pexp.astype(BF16), vb, preferred_element_type=F32)
            new.append((m_new, l, acc))
        return tuple(new)

    one = (jnp.full((tq, 1), NEG_INF, F32), jnp.zeros((tq, 1), F32), jnp.zeros((tq, 2 * MLA_VDIM), F32))
    carry = lax.fori_loop(0, i, lambda j, c: step(j, c, False), (one, one))
    (_, l0, acc0), (_, l1, acc1) = step(i, carry, True)
    lane = _iota((tq, 2 * MLA_VDIM), 1)
    o_ref[...] = jnp.where(lane < MLA_VDIM, acc0 / l0, acc1 / l1).astype(o_ref.dtype)


def _attn_prompt(q, k, v, B, T, tq=1024):
    n = q.shape[0]
    tq = min(tq, T)
    nq = T // tq
    return pl.pallas_call(
        functools.partial(_attn_prompt_kernel, tq=tq),
        grid=(B, MLA_HEADS // 2, nq),
        in_specs=[pl.BlockSpec((tq, 2 * HEAD_SLOT), lambda b, hp, i: (b * nq + i, hp)),
                  pl.BlockSpec((T, 2 * HEAD_SLOT), lambda b, hp, i: (b, hp)),
                  pl.BlockSpec((T, 2 * MLA_VDIM), lambda b, hp, i: (b, hp))],
        out_specs=pl.BlockSpec((tq, 2 * MLA_VDIM), lambda b, hp, i: (b * nq + i, hp)),
        out_shape=jax.ShapeDtypeStruct((n, MLA_HEADS * MLA_VDIM), BF16),
        compiler_params=_params(("parallel", "parallel", "arbitrary")), name="attn_prompt")(q, k, v)


PAGE_GROUP = 16
SEQ_PER_STEP = 4


def _attn_sample_kernel(pt_ref, qlat_ref, q_ref, c_ref, kpe_ref, wuv_ref, ckv_hbm, kpe_hbm, o_ref,
                        cbuf, kbuf, sem, *, T, layer, n_pages):
    G, NS = PAGE_GROUP, SEQ_PER_STEP
    b0 = pl.program_id(0) * NS
    n_groups = n_pages // G
    R = MLA_HEADS * T

    def page_copies(g, slot):
        out = []
        for sq in range(NS):
            for i in range(G):
                pg = pt_ref[b0 + sq, g * G + i]
                out.append(pltpu.make_async_copy(ckv_hbm.at[layer, pg], cbuf.at[slot, sq, i], sem.at[0, slot]))
                out.append(pltpu.make_async_copy(kpe_hbm.at[layer, pg], kbuf.at[slot, sq, i], sem.at[1, slot]))
        return out

    for cp in page_copies(0, 0):
        cp.start()

    qls, qps = [], []
    for sq in range(NS):
        rows = slice(sq * T, (sq + 1) * T)
        qls.append(jnp.concatenate([qlat_ref[rows, h * MLA_KV_RANK:(h + 1) * MLA_KV_RANK]
                                    for h in range(MLA_HEADS)], axis=0).astype(BF16))
        qs = jnp.concatenate([q_ref[rows, h * HEAD_SLOT:(h + 1) * HEAD_SLOT] for h in range(MLA_HEADS)], axis=0)
        qps.append(qs[:, ROPE_LANE0:ROPE_LANE0 + MLA_ROPE].astype(BF16))

    def update(carry, s, cb):
        m, l, acc = carry
        m_new = jnp.maximum(m, jnp.max(s, axis=-1, keepdims=True))
        pexp = jnp.exp(s - m_new)
        alpha = jnp.exp(m - m_new)
        l = alpha * l + jnp.sum(pexp, axis=-1, keepdims=True)
        acc = alpha * acc + jnp.dot(pexp.astype(BF16), cb, preferred_element_type=F32)
        return m_new, l, acc

    def group(g, carries):
        slot = g & 1

        @pl.when(g + 1 < n_groups)
        def _():
            for cp in page_copies(g + 1, 1 - slot):
                cp.start()
        for cp in page_copies(g, slot):
            cp.wait()
        seqs = range(NS)
        cbs = [cbuf[slot, sq].reshape(G * PAGE_SIZE, MLA_KV_RANK).astype(BF16) for sq in seqs]
        kts = [kbuf[slot, sq].astype(BF16) for sq in seqs]
        ss = [_dot_nt(qls[sq], cbs[sq]) for sq in seqs]
        ss = [ss[sq] + jnp.concatenate([jnp.dot(qps[sq], kts[sq][i], preferred_element_type=F32) for i in range(G)], axis=-1)
              for sq in seqs]
        ms = [jnp.maximum(carries[sq][0], jnp.max(ss[sq], axis=-1, keepdims=True)) for sq in seqs]
        ps = [jnp.exp(ss[sq] - ms[sq]) for sq in seqs]
        alphas = [jnp.exp(carries[sq][0] - ms[sq]) for sq in seqs]
        ls = [alphas[sq] * carries[sq][1] + jnp.sum(ps[sq], axis=-1, keepdims=True) for sq in seqs]
        accs = [alphas[sq] * carries[sq][2] + jnp.dot(ps[sq].astype(BF16), cbs[sq], preferred_element_type=F32) for sq in seqs]
        return tuple((ms[sq], ls[sq], accs[sq]) for sq in seqs)

    one = (jnp.full((R, 1), NEG_INF, F32), jnp.zeros((R, 1), F32), jnp.zeros((R, MLA_KV_RANK), F32))
    carries = lax.fori_loop(0, n_groups, group, (one,) * NS)

    pad = PAGE_SIZE - T
    tq = _iota((R, PAGE_SIZE), 0) & (T - 1)
    lane_h = _iota((T, MLA_HEADS * MLA_VDIM), 1) // MLA_VDIM
    for sq in range(NS):
        rows = slice(sq * T, (sq + 1) * T)
        cn = jnp.concatenate([c_ref[rows, :], jnp.zeros((pad, MLA_KV_RANK), F32)], axis=0).astype(BF16)
        kn = jnp.concatenate([kpe_ref[rows, :], jnp.zeros((pad, MLA_ROPE), F32)], axis=0).astype(BF16)
        s = jnp.where(_iota((R, PAGE_SIZE), 1) <= tq, _dot_nt(qls[sq], cn) + _dot_nt(qps[sq], kn), NEG_INF)
        _, l, acc = update(carries[sq], s, cn)
        z = _dot(acc / l, wuv_ref[...])
        out = jnp.zeros((T, MLA_HEADS * MLA_VDIM), F32)
        for h in range(MLA_HEADS):
            out = jnp.where(lane_h == h, z[h * T:(h + 1) * T], out)
        o_ref[rows, :] = out.astype(o_ref.dtype)


def _attn_sample(qlat, q, c, kpe, wuv, cache_ckv, cache_kpe_t, page_table, layer, B, T):
    n = qlat.shape[0]
    n_pages = page_table.shape[1]
    ns = SEQ_PER_STEP
    row = lambda w: pl.BlockSpec((ns * T, w), lambda b, pt: (b, 0))
    hbm = pl.BlockSpec(memory_space=pl.ANY)
    grid_spec = pltpu.PrefetchScalarGridSpec(
        num_scalar_prefetch=1, grid=(B // ns,),
        in_specs=[row(qlat.shape[1]), row(q.shape[1]), row(MLA_KV_RANK), row(MLA_ROPE),
                  pl.BlockSpec(wuv.shape, lambda b, pt: (0, 0)), hbm, hbm],
        out_specs=row(MLA_HEADS * MLA_VDIM),
        scratch_shapes=[pltpu.VMEM((2, ns, PAGE_GROUP, PAGE_SIZE, MLA_KV_RANK), F32),
                        pltpu.VMEM((2, ns, PAGE_GROUP, MLA_ROPE, PAGE_SIZE), F32),
                        pltpu.SemaphoreType.DMA((2, 2))])
    return pl.pallas_call(
        functools.partial(_attn_sample_kernel, T=T, layer=layer, n_pages=n_pages), grid_spec=grid_spec,
        out_shape=jax.ShapeDtypeStruct((n, MLA_HEADS * MLA_VDIM), F32),
        compiler_params=_params(("arbitrary",)), name="attn_sample")(
            page_table, qlat, q, c, kpe, wuv, cache_ckv, cache_kpe_t)


def _rope_rows():
    half = MLA_ROPE // 2
    inv = jnp.exp(-jnp.log(ROPE_THETA) * jnp.arange(half, dtype=F32) / half)
    lane = np.arange(LANES)
    on = (lane >= ROPE_LANE0) & (lane < ROPE_LANE0 + MLA_ROPE)
    inv_l = jnp.where(on, inv[(lane - ROPE_LANE0) % half], 0.0)
    sign = np.where(on, np.where(lane < ROPE_LANE0 + half, -1.0, 1.0), 0.0)
    rows = [inv_l, (lane < ROPE_LANE0).astype(np.float32), sign, on.astype(np.float32)]
    rows += [np.zeros(LANES, np.float32)] * 4
    return jnp.stack([jnp.asarray(r, F32) for r in rows])


def _layer_weights(l, W):
    half = MLA_ROPE // 2
    r2 = lambda a: a.reshape(1, -1)
    w_in = W["w_in"][l]
    whg, wrw, wmla = w_in[:, :1024], w_in[:, 1024:1920], w_in[:, 1920:]
    wkpe = wmla[:, MLA_Q_RANK + MLA_KV_RANK:]
    z = lambda n: jnp.zeros((D_MODEL, n), F32)
    wmla_pad = jnp.concatenate([wmla[:, :MLA_Q_RANK + MLA_KV_RANK], z(ROPE_LANE0), wkpe, z(HEAD_SLOT - ROPE_LANE0 - MLA_ROPE),
                                z(ROPE_LANE0), wkpe[:, half:], wkpe[:, :half], z(HEAD_SLOT - ROPE_LANE0 - MLA_ROPE)], axis=1)
    wqb = W["mla_wqb"][l].reshape(MLA_Q_RANK, MLA_HEADS, MLA_NOPE + MLA_ROPE)
    zq = lambda n: jnp.zeros((MLA_Q_RANK, MLA_HEADS, n), F32)
    rest = HEAD_SLOT - MLA_NOPE - MLA_ROPE
    wq_pad = jnp.concatenate([wqb, zq(rest)], axis=-1)
    wq_rot = jnp.concatenate([zq(MLA_NOPE), wqb[..., MLA_NOPE + half:], wqb[..., MLA_NOPE:MLA_NOPE + half], zq(rest)], axis=-1)
    hw = MLA_HEADS * HEAD_SLOT
    wq = jnp.concatenate([wq_pad.reshape(MLA_Q_RANK, hw), wq_rot.reshape(MLA_Q_RANK, hw)], axis=1)
    wuk = W["mla_wuk"][l]
    wuk_pad = jnp.concatenate([wuk, jnp.zeros((MLA_KV_RANK, MLA_HEADS, HEAD_SLOT - MLA_NOPE), F32)], axis=-1).reshape(MLA_KV_RANK, hw)
    wukt = wuk.transpose(1, 2, 0).reshape(MLA_HEADS * MLA_NOPE, MLA_KV_RANK)
    wuv = W["mla_wuv"][l].reshape(MLA_KV_RANK, MLA_HEADS * MLA_VDIM)
    proj = dict(g=r2(W["norm_mix"][l]), whg=whg.astype(BF16), wrw=wrw.astype(BF16), wmla=wmla_pad.astype(BF16),
                qn=r2(W["mla_q_norm"][l]), wq=wq.astype(BF16), kvn=r2(W["mla_kv_norm"][l]), rope=_rope_rows(),
                wuk=wuk_pad.astype(BF16), wuv=wuv.astype(BF16), wukt=wukt.astype(BF16))
    zr = lambda n: jnp.zeros((n, RW_WIDTH), F32)
    rwk = dict(mu=r2(W["rw_mu"][l]), w0=r2(W["rw_w0"][l]),
               w2=jnp.concatenate([W["rw_w2"][l], zr(96)], axis=0).astype(BF16),
               a0=r2(W["rw_a0"][l]),
               a2=jnp.concatenate([zr(32), W["rw_a2"][l], zr(64)], axis=0).astype(BF16),
               g2=jnp.concatenate([zr(64), W["rw_g2"][l]], axis=0).astype(BF16),
               kk=r2(W["rw_kk"][l]), ka=r2(W["rw_ka"][l]), rk=r2(W["rw_rk"][l]),
               lnw=r2(W["rw_ln_w"][l]), lnb=r2(W["rw_ln_b"][l]))
    return dict(
        proj=proj, rwkv=rwk,
        ffn1=(r2(W["norm_ffn1"][l]), W["ffn1_wi"][l].astype(BF16), W["ffn1_wo"][l].astype(BF16)),
        ffn2=(r2(W["norm_ffn2"][l]), W["ffn2_wi"][l].astype(BF16), W["ffn2_wo"][l].astype(BF16)),
        wout=W["w_out"][l].astype(BF16), hgn=r2(jnp.tile(W["hg_norm"][l], HG_HEADS)))


def _trunk(x, B, T, pos0, LW, W, sample, st=None):
    n = B * T
    x = x.reshape(n, D_MODEL)
    hg_out, rw_out, sh_out, c_out, kpe_out = [], [], [], [], []
    for l in range(DEPTH):
        lw = LW[l]
        x = _ffn(x, *lw["ffn1"])
        if sample:
            p_hg, p_rw, q, qlat, c, kpe = _proj(x, lw["proj"], T, pos0, True)
            o_hg, hg_s = _hgrn(p_hg, W["hg_lb_logits"], lw["hgn"], st["hg"][l], l, B, T, False)
            shift_rows = jnp.repeat(st["sh"][l], T, axis=0)
            o_rw, rw_s = _rwkv(p_rw, shift_rows, st["rw"][l].reshape(B, MIX_W, HEAD_D), lw["rwkv"], B, T, False)
            o_mla = _attn_sample(qlat, q, c, kpe, lw["proj"]["wuv"], st["ckv"], st["kpe"], st["pt"], l, B, T)
        else:
            p_hg, p_rw, q, k, v, c, kpe = _proj(x, lw["proj"], T, pos0, False)
            o_hg, hg_s, o_rw, rw_s = _mixers_prompt(p_hg, p_rw, W["hg_lb_logits"], lw["hgn"], lw["rwkv"], l, B, T)
            o_mla = _attn_prompt(q, k, v, B, T)
        gf = W["norm_final"].reshape(1, -1) if l == DEPTH - 1 else None
        x = _ffn(x, *lw["ffn2"], mix=(o_hg, o_rw, o_mla), wout=lw["wout"], gf=gf)
        hg_out.append(hg_s)
        rw_out.append(rw_s.reshape(B, RW_HEADS, RW_DH, RW_DH))
        sh_out.append(p_rw.reshape(B, T, RW_COLS)[:, -1])
        c_out.append(c.reshape(B, T, MLA_KV_RANK))
        kpe_out.append(kpe.reshape(B, T, MLA_ROPE))
    st_ = jnp.stack
    return x.reshape(B, T, D_MODEL), st_(hg_out), st_(rw_out), st_(sh_out), st_(c_out), st_(kpe_out)


def kernel(x_prompt, x_sample, cache_mla_ckv, cache_mla_kpe, state_hgrn, state_rwkv, state_rwkv_shift, page_table, norm_ffn1, ffn1_wi, ffn1_wo, norm_mix, w_in, w_out, hg_lb_logits, hg_norm, rw_mu, rw_w0, rw_w2, rw_a0, rw_a2, rw_g2, rw_kk, rw_ka, rw_rk, rw_ln_w, rw_ln_b, mla_q_norm, mla_wqb, mla_kv_norm, mla_wuk, mla_wuv, norm_ffn2, ffn2_wi, ffn2_wo, norm_final):
    W = dict(norm_ffn1=norm_ffn1, ffn1_wi=ffn1_wi, ffn1_wo=ffn1_wo, norm_mix=norm_mix, w_in=w_in, w_out=w_out,
             hg_lb_logits=hg_lb_logits, hg_norm=hg_norm, rw_mu=rw_mu, rw_w0=rw_w0, rw_w2=rw_w2, rw_a0=rw_a0,
             rw_a2=rw_a2, rw_g2=rw_g2, rw_kk=rw_kk, rw_ka=rw_ka, rw_rk=rw_rk, rw_ln_w=rw_ln_w, rw_ln_b=rw_ln_b,
             mla_q_norm=mla_q_norm, mla_wqb=mla_wqb, mla_kv_norm=mla_kv_norm, mla_wuk=mla_wuk, mla_wuv=mla_wuv,
             norm_ffn2=norm_ffn2, ffn2_wi=ffn2_wi, ffn2_wo=ffn2_wo, norm_final=norm_final)
    LW = [_layer_weights(l, W) for l in range(DEPTH)]
    Bp, Tp, _ = x_prompt.shape
    y_p, hg_p, rw_p, sh_p, ckv_p, kpe_p = _trunk(x_prompt, Bp, Tp, 0, LW, W, False)
    Bs, Ts, _ = x_sample.shape
    past_len = page_table.shape[1] * cache_mla_ckv.shape[2]
    st = dict(hg=state_hgrn, rw=state_rwkv, sh=state_rwkv_shift, ckv=cache_mla_ckv,
              kpe=jnp.swapaxes(cache_mla_kpe, 2, 3), pt=page_table)
    y_s, hg_s, rw_s, sh_s, ckv_s, kpe_s = _trunk(x_sample, Bs, Ts, past_len, LW, W, True, st)
    return (y_p, y_s, hg_p, hg_s, rw_p, rw_s, sh_p, sh_s, ckv_p, ckv_s, kpe_p, kpe_s)
```

```python
import functools
import math

import jax
import jax.numpy as jnp
import numpy as np
from jax import lax
from jax.experimental import pallas as pl
from jax.experimental.pallas import tpu as pltpu

F32 = jnp.float32
BF16 = jnp.bfloat16

D_MODEL = 1024
DEPTH = 2
PAGE_SIZE = 128
EPS = 1e-6
D_FF = 2816
HG_HEADS = 4
HG_DK = 64
HG_KW = 256
HG_MIN_F = 1e-30
RW_HEADS = 4
RW_DH = 64
RW_WIDTH = 256
RW_COLS = 896
RW_GN_EPS = 64e-5
MLA_HEADS = 8
MLA_NOPE = 64
MLA_ROPE = 32
MLA_VDIM = 64
MLA_Q_RANK = 384
MLA_KV_RANK = 256
MLA_SCALE = (MLA_NOPE + MLA_ROPE) ** -0.5
ROPE_THETA = 10000.0

LANES = 128
HEAD_SLOT = 128
ROPE_LANE0 = 64
VMEM_LIMIT = 56 * 1024 * 1024
NEG_INF = -1e30


def _dot(a, b):
    return jnp.dot(a.astype(BF16), b.astype(BF16), preferred_element_type=F32)


def _dot_nt(a, b):
    return lax.dot_general(a.astype(BF16), b.astype(BF16), (((1,), (1,)), ((), ())), preferred_element_type=F32)


def _dot_xl(m, b, pieces=2):
    out = None
    rem = b
    for _ in range(pieces):
        part = rem.astype(BF16)
        term = jnp.dot(m, part, preferred_element_type=F32)
        out = term if out is None else out + term
        rem = rem - part.astype(F32)
    return out


def _dot_xr(a, m, pieces=2):
    out = None
    rem = a
    for _ in range(pieces):
        part = rem.astype(BF16)
        term = jnp.dot(part, m, preferred_element_type=F32)
        out = term if out is None else out + term
        rem = rem - part.astype(F32)
    return out


def _rms(x, g, eps=EPS):
    return x * lax.rsqrt(jnp.mean(x * x, axis=-1, keepdims=True) + eps) * g


def _sigmoid(x):
    return 1.0 / (1.0 + jnp.exp(-x))


def _iota(shape, dim):
    return lax.broadcasted_iota(jnp.int32, shape, dim)


def _head_block_ones(n, hd):
    return (_iota((n, n), 0) // hd == _iota((n, n), 1) // hd).astype(BF16)


def _full(shape):
    return pl.BlockSpec(shape, lambda *_: (0,) * len(shape), pipeline_mode=pl.Buffered(1))


def _layer_block(shape, layer):
    return pl.BlockSpec((None,) + tuple(shape[1:]), lambda *_: (layer,) + (0,) * (len(shape) - 1),
                        pipeline_mode=pl.Buffered(1))


def _params(sem):
    return pltpu.CompilerParams(dimension_semantics=sem, vmem_limit_bytes=VMEM_LIMIT)


FF_CHUNK = 256


def _ffn_kernel(*refs, has_mix, final):
    it = iter(refs)
    x_ref = next(it)
    if has_mix:
        ohg_ref, orw_ref, omla_ref, wout_ref = next(it), next(it), next(it), next(it)
    g_ref, wi_ref, wo_ref = next(it), next(it), next(it)
    gf_ref = next(it) if final else None
    o_ref = next(it)

    x = x_ref[...]
    if has_mix:
        o = jnp.concatenate([r[...].astype(BF16) for r in (ohg_ref, orw_ref, omla_ref)], axis=-1)
        x = x + jnp.dot(o, wout_ref[...].astype(BF16), preferred_element_type=F32)
    xn = _rms(x, g_ref[...]).astype(BF16)
    acc = jnp.zeros(x.shape, F32)
    for c in range(D_FF // FF_CHUNK):
        lo = c * FF_CHUNK
        a = jnp.dot(xn, wi_ref[:, lo:lo + FF_CHUNK].astype(BF16), preferred_element_type=F32)
        b = jnp.dot(xn, wi_ref[:, D_FF + lo:D_FF + lo + FF_CHUNK].astype(BF16), preferred_element_type=F32)
        h = (a * _sigmoid(a) * b).astype(BF16)
        acc = acc + jnp.dot(h, wo_ref[lo:lo + FF_CHUNK, :].astype(BF16), preferred_element_type=F32)
    y = x + 0.5 * acc
    if final:
        y = _rms(y, gf_ref[...])
    o_ref[...] = y


def _ffn(x, g, wi, wo, layer, mix=None, wout=None, gf=None, tm=512):
    n = x.shape[0]
    tm = min(tm, n)
    row = lambda w: pl.BlockSpec((tm, w), lambda i: (i, 0))
    args, specs = [x], [row(D_MODEL)]
    if mix is not None:
        for m in mix:
            args.append(m)
            specs.append(row(m.shape[1]))
        args.append(wout)
        specs.append(_layer_block(wout.shape, layer))
    args += [g, wi, wo]
    specs += [_full(g.shape), _layer_block(wi.shape, layer), _layer_block(wo.shape, layer)]
    if gf is not None:
        args.append(gf)
        specs.append(_full(gf.shape))
    return pl.pallas_call(
        functools.partial(_ffn_kernel, has_mix=mix is not None, final=gf is not None),
        grid=(n // tm,), in_specs=specs, out_specs=row(D_MODEL),
        out_shape=jax.ShapeDtypeStruct((n, D_MODEL), F32),
        compiler_params=_params(("parallel",)), name="ffn")(*args)


def _ffn_stream_kernel(*refs, has_mix, final):
    it = iter(refs)
    x_ref = next(it)
    if has_mix:
        ohg_ref, orw_ref, omla_ref, wout_ref = next(it), next(it), next(it), next(it)
    g_ref, wia_ref, wib_ref, wo_ref = next(it), next(it), next(it), next(it)
    gf_ref = next(it) if final else None
    o_ref = next(it)
    x_scr, xn_scr, acc_scr = next(it), next(it), next(it)
    c = pl.program_id(0)

    @pl.when(c == 0)
    def _():
        x = x_ref[...]
        if has_mix:
            o = jnp.concatenate([r[...].astype(BF16) for r in (ohg_ref, orw_ref, omla_ref)], axis=-1)
            x = x + jnp.dot(o, wout_ref[...].astype(BF16), preferred_element_type=F32)
        x_scr[...] = x
        xn_scr[...] = _rms(x, g_ref[...]).astype(BF16)
        acc_scr[...] = jnp.zeros(acc_scr.shape, F32)

    xn = xn_scr[...]
    a = jnp.dot(xn, wia_ref[...].astype(BF16), preferred_element_type=F32)
    b = jnp.dot(xn, wib_ref[...].astype(BF16), preferred_element_type=F32)
    h = (a * _sigmoid(a) * b).astype(BF16)
    acc_scr[...] += jnp.dot(h, wo_ref[...].astype(BF16), preferred_element_type=F32)

    @pl.when(c == pl.num_programs(0) - 1)
    def _():
        y = x_scr[...] + 0.5 * acc_scr[...]
        if final:
            y = _rms(y, gf_ref[...])
        o_ref[...] = y


def _ffn_stream(x, g, wi, wo, layer, mix=None, wout=None, gf=None):
    n = x.shape[0]
    n_chunks = D_FF // FF_CHUNK
    const = lambda a: pl.BlockSpec(a.shape, lambda c: (0,) * a.ndim)
    args, specs = [x], [const(x)]
    if mix is not None:
        args += list(mix) + [wout]
        specs += [const(m) for m in mix] + [_layer_block(wout.shape, layer)]
    args += [g, wi, wi, wo]
    specs += [const(g),
              pl.BlockSpec((None, D_MODEL, FF_CHUNK), lambda c: (layer, 0, c)),
              pl.BlockSpec((None, D_MODEL, FF_CHUNK), lambda c: (layer, 0, n_chunks + c)),
              pl.BlockSpec((None, FF_CHUNK, D_MODEL), lambda c: (layer, c, 0))]
    if gf is not None:
        args.append(gf)
        specs.append(const(gf))
    return pl.pallas_call(
        functools.partial(_ffn_stream_kernel, has_mix=mix is not None, final=gf is not None),
        grid=(n_chunks,), in_specs=specs, out_specs=pl.BlockSpec((n, D_MODEL), lambda c: (0, 0)),
        out_shape=jax.ShapeDtypeStruct((n, D_MODEL), F32),
        scratch_shapes=[pltpu.VMEM((n, D_MODEL), F32), pltpu.VMEM((n, D_MODEL), BF16), pltpu.VMEM((n, D_MODEL), F32)],
        compiler_params=_params(("arbitrary",)), name="ffn_stream")(*args)


def _proj_kernel(*refs, seq_len, pos0, sample):
    it = iter(refs)
    x_ref, g_ref, win_ref, wk2_ref = (next(it) for _ in range(4))
    qn_ref, wq_ref, kvn_ref, rope_ref = (next(it) for _ in range(4))
    if sample:
        wukt_ref = next(it)
    else:
        wuk_ref, wuv_ref = next(it), next(it)
    phg_ref, prw_ref, q_ref = next(it), next(it), next(it)
    if sample:
        qlat_ref = next(it)
    else:
        k_ref, v_ref = next(it), next(it)
    c_ref, kpe_ref = next(it), next(it)

    tm = x_ref.shape[0]
    xn = _rms(x_ref[...], g_ref[...]).astype(BF16)
    phg_ref[...] = jnp.dot(xn, win_ref[:, 0:1024].astype(BF16), preferred_element_type=F32)
    prw_ref[...] = jnp.dot(xn, win_ref[:, 1024:1920].astype(BF16), preferred_element_type=F32)
    pm = jnp.dot(xn, win_ref[:, 1920:2560].astype(BF16), preferred_element_type=F32)
    pk = jnp.dot(xn, wk2_ref[...], preferred_element_type=F32)

    row = pl.program_id(0) * tm + _iota((tm, 1), 0)
    pos = (pos0 + (row & (seq_len - 1))).astype(F32)
    rope = rope_ref[...]
    ang = pos * rope[0:1, :]
    cos_s = rope[1:2, :] + rope[3:4, :] * jnp.cos(ang)
    sin_s = rope[2:3, :] * jnp.sin(ang)

    c = _rms(pm[:, MLA_Q_RANK:MLA_Q_RANK + MLA_KV_RANK], kvn_ref[...])
    c_ref[...] = c
    k_slot = pk[:, :HEAD_SLOT] * cos_s + pk[:, HEAD_SLOT:] * sin_s
    kpe_ref[...] = k_slot[:, ROPE_LANE0:ROPE_LANE0 + MLA_ROPE]

    qn = _rms(pm[:, :MLA_Q_RANK], qn_ref[...]).astype(BF16)
    q2 = jnp.dot(qn, wq_ref[...], preferred_element_type=F32)
    hw = MLA_HEADS * HEAD_SLOT
    q_scale = MLA_SCALE if sample else MLA_SCALE * math.log2(math.e)
    cos_q = jnp.concatenate([cos_s * q_scale] * MLA_HEADS, axis=-1)
    sin_q = jnp.concatenate([sin_s * q_scale] * MLA_HEADS, axis=-1)
    q = q2[:, :hw] * cos_q + q2[:, hw:] * sin_q
    if sample:
        q_ref[...] = q
        for h in range(MLA_HEADS):
            qh = q[:, h * HEAD_SLOT:h * HEAD_SLOT + MLA_NOPE]
            qlat_ref[:, h * MLA_KV_RANK:(h + 1) * MLA_KV_RANK] = _dot(qh, wukt_ref[h * MLA_NOPE:(h + 1) * MLA_NOPE, :])
    else:
        q_ref[...] = q.astype(BF16)
        cb = c.astype(BF16)
        kn = jnp.dot(cb, wuk_ref[...], preferred_element_type=F32)
        k_ref[...] = (kn + jnp.concatenate([k_slot] * MLA_HEADS, axis=-1)).astype(BF16)
        v_ref[...] = jnp.dot(cb, wuv_ref[...], preferred_element_type=F32).astype(BF16)


def _proj(x, pw, seq_len, pos0, sample, tm=512):
    n = x.shape[0]
    tm = min(tm, n)
    row = lambda w: pl.BlockSpec((tm, w), lambda i: (i, 0))
    hw = MLA_HEADS * HEAD_SLOT
    args = [x, pw["g"], pw["win"], pw["wk2"], pw["qn"], pw["wq"], pw["kvn"], pw["rope"]]
    args += [pw["wukt"]] if sample else [pw["wuk"], pw["wuv"]]
    specs = [row(D_MODEL)] + [_full(a.shape) for a in args[1:]]
    specs[2] = _layer_block(pw["win"].shape, pw["layer"])
    qdt = F32 if sample else BF16
    outs = [((n, 1024), F32), ((n, RW_COLS), F32), ((n, hw), qdt)]
    outs += [((n, MLA_HEADS * MLA_KV_RANK), F32)] if sample else [((n, hw), BF16), ((n, MLA_HEADS * MLA_VDIM), BF16)]
    outs += [((n, MLA_KV_RANK), F32), ((n, MLA_ROPE), F32)]
    return pl.pallas_call(
        functools.partial(_proj_kernel, seq_len=seq_len, pos0=pos0, sample=sample),
        grid=(n // tm,), in_specs=specs, out_specs=[row(s[1]) for s, _ in outs],
        out_shape=[jax.ShapeDtypeStruct(s, d) for s, d in outs],
        compiler_params=_params(("parallel",)), name="proj")(*args)


MIX_W = 256
HEAD_D = 64


def _chunk_masks(L, C):
    r, c = _iota((L, L), 0), _iota((L, L), 1)
    same = (r // C) == (c // C)
    return same, same & (c <= r), same & (c < r)


def _expand_state(s_kv):
    t = jnp.concatenate([s_kv] * (MIX_W // HEAD_D), axis=0)
    bd = (_iota((MIX_W, MIX_W), 0) // HEAD_D) == (_iota((MIX_W, MIX_W), 1) // HEAD_D)
    return jnp.where(bd, t, 0.0)


def _fold_state(hbd):
    out = hbd[0:HEAD_D]
    for h in range(1, MIX_W // HEAD_D):
        out = out + hbd[h * HEAD_D:(h + 1) * HEAD_D]
    return out


def _hgrn_steps(*refs, L, C, layer, chained, own_init=True):
    it = iter(refs)
    p_ref, lbl_ref, gn_ref = next(it), next(it), next(it)
    s0_ref = None if chained else next(it)
    o_ref, sout_ref = next(it), next(it)
    h_scr = next(it) if chained else None

    nck = L // C
    p = p_ref[...]
    q, fz, vi, gate = p[:, 0:256], p[:, 256:512], p[:, 512:768], p[:, 768:1024]

    lg = lbl_ref[...]
    e = jnp.exp(lg - jnp.max(lg, axis=0, keepdims=True))
    pr = e / jnp.sum(e, axis=0, keepdims=True)
    cs = pr[0:1]
    for i in range(1, layer + 1):
        cs = cs + pr[i:i + 1]
    lb = jnp.clip(cs - pr[0:1], 0.0, 1.0)

    f = lb + (1.0 - lb) * _sigmoid(fz)
    log_f = jnp.log(jnp.maximum(f, HG_MIN_F))
    k = (1.0 - lb) * _sigmoid(-fz)

    same, incl, _ = _chunk_masks(L, C)
    b = _dot_xl(incl.astype(BF16), log_f, pieces=3)
    b_last = _dot_xl(same.astype(BF16), log_f, pieces=3)
    ones_h = _head_block_ones(MIX_W, HEAD_D)

    b3, q3, k3, v3 = (a.reshape(nck, C, MIX_W) for a in (b, q, k, vi))
    SUB = 8
    tiles = [jnp.zeros((nck, min(SUB, C), MIX_W), F32) for _ in range(max(C // SUB, 1))]
    for s in range(C):
        t0 = (s // SUB) * SUB
        bt, qt = b3[:, t0:, :], q3[:, t0:, :]
        rows = C - t0
        w = jnp.exp(jnp.minimum(bt - b3[:, s:s + 1, :], 0.0))
        x = jnp.where(t0 + _iota((nck, rows, MIX_W), 1) >= s, qt * w * k3[:, s:s + 1, :], 0.0)
        a_s = _dot(x.reshape(nck * rows, MIX_W), ones_h).reshape(nck, rows, MIX_W)
        upd = a_s * v3[:, s:s + 1, :]
        for ti in range(t0 // SUB, len(tiles)):
            lo = ti * SUB - t0
            tiles[ti] = tiles[ti] + upd[:, lo:lo + SUB, :]
        yield
    o_intra = jnp.concatenate(tiles, axis=1).reshape(L, MIX_W)

    qe = q * jnp.exp(b)
    kd_t = (k * jnp.exp(b_last - b)).T
    e_t = jnp.exp(b_last).T
    bd = (_iota((MIX_W, MIX_W), 0) // HEAD_D) == (_iota((MIX_W, MIX_W), 1) // HEAD_D)
    col = _iota((1, L), 1)

    if chained:
        if own_init:
            @pl.when(pl.program_id(1) == 0)
            def _():
                h_scr[...] = jnp.zeros((MIX_W, MIX_W), F32)
        hbd = h_scr[...]
    gks = []
    for ck in range(nck):
        gks.append(jnp.where(bd, _dot(jnp.where((col // C) == ck, kd_t, 0.0), vi), 0.0))
        yield
    dcols = [jnp.sum(jnp.where(col == ck * C, e_t, 0.0), axis=1, keepdims=True) for ck in range(nck)]
    o_parts = []
    for ck in range(nck):
        if not chained:
            s_kv = jnp.concatenate([s0_ref[ck, h] for h in range(HG_HEADS)], axis=-1)
            hbd = _expand_state(s_kv)
        o_parts.append(_dot(qe[ck * C:(ck + 1) * C], hbd))
        hbd = dcols[ck] * hbd + gks[ck]
        if not chained:
            s_new = _fold_state(hbd)
            for h in range(HG_HEADS):
                sout_ref[ck, h] = s_new[:, h * HEAD_D:(h + 1) * HEAD_D]
    if chained:
        h_scr[...] = hbd
        s_new = _fold_state(hbd)
        for h in range(HG_HEADS):
            sout_ref[h] = s_new[:, h * HEAD_D:(h + 1) * HEAD_D]

    o = o_intra + jnp.concatenate(o_parts, axis=0)
    ms = _dot_xr(o * o, ones_h) * (1.0 / HEAD_D)
    on = o * lax.rsqrt(ms + EPS) * gn_ref[...]
    o_ref[...] = (on * (gate * _sigmoid(gate))).astype(o_ref.dtype)


def _drain(*gens):
    alive = list(gens)
    while alive:
        for g in list(alive):
            try:
                next(g)
            except StopIteration:
                alive.remove(g)


def _hgrn_kernel(*refs, **kw):
    _drain(_hgrn_steps(*refs, **kw))


def _hgrn(p_hg, lb_logits, gn, s0, layer, B, T, chained):
    n = p_hg.shape[0]
    if chained:
        L, C = min(256, T), min(16, T)
        nblk = T // L
        grid = (B, nblk)
        rowmap = lambda b, j: (b * nblk + j, 0)
        in_specs = [pl.BlockSpec((L, 1024), rowmap), _full(lb_logits.shape), _full(gn.shape)]
        args = [p_hg, lb_logits, gn]
        out_specs = [pl.BlockSpec((L, MIX_W), rowmap),
                     pl.BlockSpec((None, HG_HEADS, HEAD_D, HEAD_D), lambda b, j: (b, 0, 0, 0))]
        scratch = [pltpu.VMEM((MIX_W, MIX_W), F32)]
        sem = ("parallel", "arbitrary")
    else:
        L, C = n, T
        grid = (1,)
        in_specs = [_full((L, 1024)), _full(lb_logits.shape), _full(gn.shape), _full(s0.shape)]
        args = [p_hg, lb_logits, gn, s0]
        out_specs = [_full((L, MIX_W)), _full((B, HG_HEADS, HEAD_D, HEAD_D))]
        scratch = []
        sem = ("arbitrary",)
    return pl.pallas_call(
        functools.partial(_hgrn_kernel, L=L, C=C, layer=layer, chained=chained),
        grid=grid, in_specs=in_specs, out_specs=out_specs,
        out_shape=[jax.ShapeDtypeStruct((n, MIX_W), BF16),
                   jax.ShapeDtypeStruct((B, HG_HEADS, HEAD_D, HEAD_D), F32)],
        scratch_shapes=scratch, compiler_params=_params(sem), name="hgrn")(*args)


def _rwkv_steps(*refs, L, C, chained, own_init=True):
    it = iter(refs)
    p_ref = next(it)
    if not chained:
        shrow_ref, s0_ref = next(it), next(it)
    (mu_ref, w0_ref, w2_ref, a0_ref, a2_ref, g2_ref, kkw_ref, ka_ref, rk_ref, lnw_ref, lnb_ref) = (next(it) for _ in range(11))
    y_ref, sout_ref = next(it), next(it)
    if chained:
        h_scr, carry_scr = next(it), next(it)

    nck = L // C
    p = p_ref[...]
    rowi = _iota((L, 1), 0)
    rolled = pltpu.roll(p, 1, 0)
    if chained:
        if own_init:
            @pl.when(pl.program_id(1) == 0)
            def _():
                h_scr[...] = jnp.zeros((MIX_W, MIX_W), F32)
                carry_scr[...] = jnp.zeros(carry_scr.shape, F32)
        prev = jnp.where(rowi == 0, carry_scr[0:1, :], rolled)
        carry_scr[0:1, :] = p[L - 1:L, :]
    else:
        prev = jnp.where((rowi & (C - 1)) == 0, shrow_ref[...], rolled)
    ps = p + (prev - p) * mu_ref[...]
    r, k, v, tail = ps[:, 0:256], ps[:, 256:512], ps[:, 512:768], ps[:, 768:896]

    w_raw = -jax.nn.softplus(-(w0_ref[...] + _dot(jnp.tanh(tail), w2_ref[...]))) - 0.5
    lw = -jnp.exp(w_raw)
    a = _sigmoid(a0_ref[...] + _dot(tail, a2_ref[...]))
    g = _dot(_sigmoid(tail), g2_ref[...])

    ones_h = _head_block_ones(MIX_W, HEAD_D)
    kk = k * kkw_ref[...]
    kk = kk / jnp.maximum(jnp.sqrt(_dot_xr(kk * kk, ones_h)), 1e-12)
    k2 = k * (1.0 + (a - 1.0) * ka_ref[...])
    bb = kk * a

    same, incl, strict = _chunk_masks(L, C)
    gc = _dot_xl(incl.astype(BF16), lw)
    g_last = _dot_xl(same.astype(BF16), lw)
    eg, eng = jnp.exp(gc), jnp.exp(-gc)
    rg, kg, bg, kkg = r * eg, k2 * eng, bb * eng, kk * jnp.exp(gc - lw)
    to_end = jnp.exp(g_last - gc)
    kd_t, bd_t, e_t = (k2 * to_end).T, (bb * to_end).T, jnp.exp(g_last).T

    lane_h = _iota((1, MIX_W), 1) // HEAD_D
    eye = (_iota((L, L), 0) == _iota((L, L), 1)).astype(F32)
    u0 = jnp.zeros((L, MIX_W), F32)
    kkt = jnp.zeros((L, MIX_W), F32)
    qp = rg
    o0 = jnp.zeros((L, MIX_W), F32)
    heads = range(RW_HEADS)
    mhs = [lane_h == h for h in heads]
    kkg_hs = [jnp.where(mh, kkg, 0.0) for mh in mhs]
    rg_hs = [jnp.where(mh, rg, 0.0) for mh in mhs]
    a_kb, a_kk, a_rk, a_rb = [], [], [], []
    for h in heads:
        a_kb.append(jnp.where(strict, _dot_nt(kkg_hs[h], bg), 0.0))
        a_kk.append(jnp.where(strict, _dot_nt(kkg_hs[h], kg), 0.0))
        yield
    for h in heads:
        a_rk.append(jnp.where(incl, _dot_nt(rg_hs[h], kg), 0.0))
        a_rb.append(jnp.where(incl, _dot_nt(rg_hs[h], bg), 0.0))
        yield
    npow = [-a for a in a_kb]
    tinv = [eye + n_ for n_ in npow]
    for _ in range(int(math.log2(C)) - 1):
        npow = [_dot(n_, n_) for n_ in npow]
        yield
        tinv = [t_ + _dot(t_, n_) for t_, n_ in zip(tinv, npow)]
        yield
    x1 = [_dot(a_kk[h], v) for h in heads]
    yield
    u0_hs = [_dot(tinv[h], x1[h]) for h in heads]
    yield
    kkt_hs = [_dot(tinv[h], kkg) for h in heads]
    yield
    for h in heads:
        u0 = jnp.where(mhs[h], u0_hs[h], u0)
        kkt = jnp.where(mhs[h], kkt_hs[h], kkt)
    for h in heads:
        qp = qp - jnp.where(mhs[h], _dot(a_rb[h], kkt_hs[h]), 0.0)
        o0 = o0 + jnp.where(mhs[h], _dot(a_rk[h], v) - _dot(a_rb[h], u0_hs[h]), 0.0)
        yield

    bd = (_iota((MIX_W, MIX_W), 0) // HEAD_D) == (_iota((MIX_W, MIX_W), 1) // HEAD_D)
    col = _iota((1, L), 1)
    if chained:
        hbd = h_scr[...]
    bd_cs = [jnp.where((col // C) == ck, bd_t, 0.0) for ck in range(nck)]
    mbs, gbs = [], []
    for ck in range(nck):
        mbs.append(jnp.where(bd, _dot(bd_cs[ck], kkt), 0.0))
        gbs.append(jnp.where(bd, _dot(jnp.where((col // C) == ck, kd_t, 0.0), v) - _dot(bd_cs[ck], u0), 0.0))
        yield
    dcols = [jnp.sum(jnp.where(col == ck * C, e_t, 0.0), axis=1, keepdims=True) for ck in range(nck)]
    y_parts = []
    for ck in range(nck):
        if not chained:
            hbd = _expand_state(s0_ref[ck].T)
        y_parts.append(_dot(qp[ck * C:(ck + 1) * C], hbd))
        hbd = dcols[ck] * hbd - _dot(mbs[ck], hbd) + gbs[ck]
        yield
        if not chained:
            sout_ref[ck] = _fold_state(hbd).T
    if chained:
        h_scr[...] = hbd
        sout_ref[...] = _fold_state(hbd).T

    y = o0 + jnp.concatenate(y_parts, axis=0)
    mean = _dot_xr(y, ones_h) * (1.0 / HEAD_D)
    yc = y - mean
    var = _dot_xr(yc * yc, ones_h) * (1.0 / HEAD_D)
    yn = yc * lax.rsqrt(var + RW_GN_EPS) * lnw_ref[...] + lnb_ref[...]
    bonus = _dot_xr(r * k2 * rk_ref[...], ones_h) * v
    y_ref[...] = ((yn + bonus) * g).astype(y_ref.dtype)


def _rwkv_kernel(*refs, **kw):
    _drain(_rwkv_steps(*refs, **kw))


def _rwkv(p_rw, shift_rows, s0, rw, B, T, chained):
    n = p_rw.shape[0]
    names = ("mu", "w0", "w2", "a0", "a2", "g2", "kk", "ka", "rk", "lnw", "lnb")
    wargs = [rw[k] for k in names]
    wspecs = [_full(a.shape) for a in wargs]
    if chained:
        L, C = min(256, T), min(64, T)
        nblk = T // L
        grid = (B, nblk)
        rowmap = lambda b, j: (b * nblk + j, 0)
        in_specs = [pl.BlockSpec((L, RW_COLS), rowmap)] + wspecs
        args = [p_rw] + wargs
        out_specs = [pl.BlockSpec((L, MIX_W), rowmap), pl.BlockSpec((None, MIX_W, HEAD_D), lambda b, j: (b, 0, 0))]
        scratch = [pltpu.VMEM((MIX_W, MIX_W), F32), pltpu.VMEM((8, RW_COLS), F32)]
        sem = ("parallel", "arbitrary")
    else:
        L, C = n, T
        grid = (1,)
        in_specs = [_full((L, RW_COLS)), _full(shift_rows.shape), _full(s0.shape)] + wspecs
        args = [p_rw, shift_rows, s0] + wargs
        out_specs = [_full((L, MIX_W)), _full((B, MIX_W, HEAD_D))]
        scratch = []
        sem = ("arbitrary",)
    return pl.pallas_call(
        functools.partial(_rwkv_kernel, L=L, C=C, chained=chained),
        grid=grid, in_specs=in_specs, out_specs=out_specs,
        out_shape=[jax.ShapeDtypeStruct((n, MIX_W), BF16), jax.ShapeDtypeStruct((B, MIX_W, HEAD_D), F32)],
        scratch_shapes=scratch, compiler_params=_params(sem), name="rwkv")(*args)


def _mixers_kernel(*refs, L, hg_c, rw_c, layer):
    hg_in, rw_in = refs[0:3], refs[3:15]
    hg_out, rw_out = refs[15:17], refs[17:19]
    hg_scr, rw_scr = refs[19:20], refs[20:22]

    @pl.when(pl.program_id(1) == 0)
    def _():
        for scr in (*hg_scr, *rw_scr):
            scr[...] = jnp.zeros(scr.shape, F32)
    _drain(_rwkv_steps(*rw_in, *rw_out, *rw_scr, L=L, C=rw_c, chained=True, own_init=False),
           _hgrn_steps(*hg_in, *hg_out, *hg_scr, L=L, C=hg_c, layer=layer, chained=True, own_init=False))


def _mixers_prompt(p_hg, p_rw, lb_logits, gn, rw, layer, B, T):
    n = p_hg.shape[0]
    names = ("mu", "w0", "w2", "a0", "a2", "g2", "kk", "ka", "rk", "lnw", "lnb")
    wargs = [rw[k] for k in names]
    L = min(256, T)
    nblk = T // L
    rowmap = lambda b, j: (b * nblk + j, 0)
    in_specs = [pl.BlockSpec((L, 1024), rowmap), _full(lb_logits.shape), _full(gn.shape),
                pl.BlockSpec((L, RW_COLS), rowmap)] + [_full(a.shape) for a in wargs]
    out_specs = [pl.BlockSpec((L, MIX_W), rowmap),
                 pl.BlockSpec((None, HG_HEADS, HEAD_D, HEAD_D), lambda b, j: (b, 0, 0, 0)),
                 pl.BlockSpec((L, MIX_W), rowmap),
                 pl.BlockSpec((None, MIX_W, HEAD_D), lambda b, j: (b, 0, 0))]
    return pl.pallas_call(
        functools.partial(_mixers_kernel, L=L, hg_c=min(16, T), rw_c=min(64, T), layer=layer),
        grid=(B, nblk), in_specs=in_specs, out_specs=out_specs,
        out_shape=[jax.ShapeDtypeStruct((n, MIX_W), BF16),
                   jax.ShapeDtypeStruct((B, HG_HEADS, HEAD_D, HEAD_D), F32),
                   jax.ShapeDtypeStruct((n, MIX_W), BF16),
                   jax.ShapeDtypeStruct((B, MIX_W, HEAD_D), F32)],
        scratch_shapes=[pltpu.VMEM((MIX_W, MIX_W), F32), pltpu.VMEM((MIX_W, MIX_W), F32), pltpu.VMEM((8, RW_COLS), F32)],
        compiler_params=_params(("parallel", "arbitrary")), name="mixers")(p_hg, lb_logits, gn, p_rw, *wargs)


def _attn_prompt_kernel(q_ref, k_ref, v_ref, o_ref, *, tq):
    i = pl.program_id(2)
    qs = [q_ref[:, hh * HEAD_SLOT:(hh + 1) * HEAD_SLOT] for hh in range(2)]

    def step(j, carry, diagonal):
        start = pl.multiple_of(j * tq, tq)
        vb = v_ref[pl.ds(start, tq), :]
        new = []
        for hh in range(2):
            m, l, acc = carry[hh]
            kb = k_ref[pl.ds(start, tq), hh * HEAD_SLOT:(hh + 1) * HEAD_SLOT]
            s = lax.dot_general(qs[hh], kb, (((1,), (1,)), ((), ())), preferred_element_type=F32)
            if diagonal:
                s = jnp.where(_iota((tq, tq), 1) <= _iota((tq, tq), 0), s, NEG_INF)
            m_new = jnp.maximum(m, jnp.max(s, axis=-1, keepdims=True))
            pexp = jnp.exp2(s - m_new)
            alpha = jnp.exp2(m - m_new)
            l = alpha * l + jnp.sum(pexp, axis=-1, keepdims=True)
            acc = alpha * acc + jnp.dot(pexp.astype(BF16), vb, preferred_element_type=F32)
            new.append((m_new, l, acc))
        return tuple(new)

    one = (jnp.full((tq, 1), NEG_INF, F32), jnp.zeros((tq, 1), F32), jnp.zeros((tq, 2 * MLA_VDIM), F32))
    carry = lax.fori_loop(0, i, lambda j, c: step(j, c, False), (one, one))
    (_, l0, acc0), (_, l1, acc1) = step(i, carry, True)
    lane = _iota((tq, 2 * MLA_VDIM), 1)
    o_ref[...] = jnp.where(lane < MLA_VDIM, acc0 / l0, acc1 / l1).astype(o_ref.dtype)


def _attn_prompt(q, k, v, B, T, tq=1024):
    n = q.shape[0]
    tq = min(tq, T)
    nq = T // tq
    return pl.pallas_call(
        functools.partial(_attn_prompt_kernel, tq=tq),
        grid=(B, MLA_HEADS // 2, nq),
        in_specs=[pl.BlockSpec((tq, 2 * HEAD_SLOT), lambda b, hp, i: (b * nq + i, hp)),
                  pl.BlockSpec((T, 2 * HEAD_SLOT), lambda b, hp, i: (b, hp)),
                  pl.BlockSpec((T, 2 * MLA_VDIM), lambda b, hp, i: (b, hp))],
        out_specs=pl.BlockSpec((tq, 2 * MLA_VDIM), lambda b, hp, i: (b * nq + i, hp)),
        out_shape=jax.ShapeDtypeStruct((n, MLA_HEADS * MLA_VDIM), BF16),
        compiler_params=_params(("parallel", "parallel", "arbitrary")), name="attn_prompt")(q, k, v)


PAGE_GROUP = 16
SEQ_PER_STEP = 4


def _attn_sample_kernel(pt_ref, qlat_ref, q_ref, c_ref, kpe_ref, wuv_ref, ckv_hbm, kpe_hbm, o_ref,
                        cbuf, kbuf, sem, *, T, layer, n_pages):
    G, NS = PAGE_GROUP, SEQ_PER_STEP
    b0 = pl.program_id(0) * NS
    n_groups = n_pages // G
    R = MLA_HEADS * T

    def page_copies(g, slot):
        out = []
        for sq in range(NS):
            for i in range(G):
                pg = pt_ref[b0 + sq, g * G + i]
                out.append(pltpu.make_async_copy(ckv_hbm.at[layer, pg], cbuf.at[slot, sq, i], sem.at[0, slot]))
                out.append(pltpu.make_async_copy(kpe_hbm.at[layer, pg], kbuf.at[slot, sq, i], sem.at[1, slot]))
        return out

    for cp in page_copies(0, 0):
        cp.start()

    qls, qps = [], []
    for sq in range(NS):
        rows = slice(sq * T, (sq + 1) * T)
        qls.append(jnp.concatenate([qlat_ref[rows, h * MLA_KV_RANK:(h + 1) * MLA_KV_RANK]
                                    for h in range(MLA_HEADS)], axis=0).astype(BF16))
        qs = jnp.concatenate([q_ref[rows, h * HEAD_SLOT:(h + 1) * HEAD_SLOT] for h in range(MLA_HEADS)], axis=0)
        qps.append(qs[:, ROPE_LANE0:ROPE_LANE0 + MLA_ROPE].astype(BF16))

    def update(carry, s, cb):
        m, l, acc = carry
        m_new = jnp.maximum(m, jnp.max(s, axis=-1, keepdims=True))
        pexp = jnp.exp(s - m_new)
        alpha = jnp.exp(m - m_new)
        l = alpha * l + jnp.sum(pexp, axis=-1, keepdims=True)
        acc = alpha * acc + jnp.dot(pexp.astype(BF16), cb, preferred_element_type=F32)
        return m_new, l, acc

    def group(g, carries):
        slot = g & 1

        @pl.when(g + 1 < n_groups)
        def _():
            for cp in page_copies(g + 1, 1 - slot):
                cp.start()
        for cp in page_copies(g, slot):
            cp.wait()
        seqs = range(NS)
        cbs = [cbuf[slot, sq].reshape(G * PAGE_SIZE, MLA_KV_RANK).astype(BF16) for sq in seqs]
        kts = [kbuf[slot, sq].astype(BF16) for sq in seqs]
        ss = [_dot_nt(qls[sq], cbs[sq]) for sq in seqs]
        ss = [ss[sq] + jnp.concatenate([jnp.dot(qps[sq], kts[sq][i], preferred_element_type=F32) for i in range(G)], axis=-1)
              for sq in seqs]
        ms = [jnp.maximum(carries[sq][0], jnp.max(ss[sq], axis=-1, keepdims=True)) for sq in seqs]
        ps = [jnp.exp(ss[sq] - ms[sq]) for sq in seqs]
        alphas = [jnp.exp(carries[sq][0] - ms[sq]) for sq in seqs]
        ls = [alphas[sq] * carries[sq][1] + jnp.sum(ps[sq], axis=-1, keepdims=True) for sq in seqs]
        accs = [alphas[sq] * carries[sq][2] + jnp.dot(ps[sq].astype(BF16), cbs[sq], preferred_element_type=F32) for sq in seqs]
        return tuple((ms[sq], ls[sq], accs[sq]) for sq in seqs)

    one = (jnp.full((R, 1), NEG_INF, F32), jnp.zeros((R, 1), F32), jnp.zeros((R, MLA_KV_RANK), F32))
    carries = lax.fori_loop(0, n_groups, group, (one,) * NS)

    pad = PAGE_SIZE - T
    tq = _iota((R, PAGE_SIZE), 0) & (T - 1)
    lane_h = _iota((T, MLA_HEADS * MLA_VDIM), 1) // MLA_VDIM
    for sq in range(NS):
        rows = slice(sq * T, (sq + 1) * T)
        cn = jnp.concatenate([c_ref[rows, :], jnp.zeros((pad, MLA_KV_RANK), F32)], axis=0).astype(BF16)
        kn = jnp.concatenate([kpe_ref[rows, :], jnp.zeros((pad, MLA_ROPE), F32)], axis=0).astype(BF16)
        s = jnp.where(_iota((R, PAGE_SIZE), 1) <= tq, _dot_nt(qls[sq], cn) + _dot_nt(qps[sq], kn), NEG_INF)
        _, l, acc = update(carries[sq], s, cn)
        z = _dot(acc / l, wuv_ref[...])
        out = jnp.zeros((T, MLA_HEADS * MLA_VDIM), F32)
        for h in range(MLA_HEADS):
            out = jnp.where(lane_h == h, z[h * T:(h + 1) * T], out)
        o_ref[rows, :] = out.astype(o_ref.dtype)


def _attn_sample(qlat, q, c, kpe, wuv, cache_ckv, cache_kpe_t, page_table, layer, B, T):
    n = qlat.shape[0]
    n_pages = page_table.shape[1]
    ns = SEQ_PER_STEP
    row = lambda w: pl.BlockSpec((ns * T, w), lambda b, pt: (b, 0))
    hbm = pl.BlockSpec(memory_space=pl.ANY)
    grid_spec = pltpu.PrefetchScalarGridSpec(
        num_scalar_prefetch=1, grid=(B // ns,),
        in_specs=[row(qlat.shape[1]), row(q.shape[1]), row(MLA_KV_RANK), row(MLA_ROPE),
                  pl.BlockSpec(wuv.shape, lambda b, pt: (0, 0)), hbm, hbm],
        out_specs=row(MLA_HEADS * MLA_VDIM),
        scratch_shapes=[pltpu.VMEM((2, ns, PAGE_GROUP, PAGE_SIZE, MLA_KV_RANK), F32),
                        pltpu.VMEM((2, ns, PAGE_GROUP, MLA_ROPE, PAGE_SIZE), F32),
                        pltpu.SemaphoreType.DMA((2, 2))])
    return pl.pallas_call(
        functools.partial(_attn_sample_kernel, T=T, layer=layer, n_pages=n_pages), grid_spec=grid_spec,
        out_shape=jax.ShapeDtypeStruct((n, MLA_HEADS * MLA_VDIM), F32),
        compiler_params=_params(("arbitrary",)), name="attn_sample")(
            page_table, qlat, q, c, kpe, wuv, cache_ckv, cache_kpe_t)


def _rope_rows():
    half = MLA_ROPE // 2
    inv = jnp.exp(-jnp.log(ROPE_THETA) * jnp.arange(half, dtype=F32) / half)
    lane = np.arange(LANES)
    on = (lane >= ROPE_LANE0) & (lane < ROPE_LANE0 + MLA_ROPE)
    inv_l = jnp.where(on, inv[(lane - ROPE_LANE0) % half], 0.0)
    sign = np.where(on, np.where(lane < ROPE_LANE0 + half, -1.0, 1.0), 0.0)
    rows = [inv_l, (lane < ROPE_LANE0).astype(np.float32), sign, on.astype(np.float32)]
    rows += [np.zeros(LANES, np.float32)] * 4
    return jnp.stack([jnp.asarray(r, F32) for r in rows])


def _layer_weights(l, W):
    half = MLA_ROPE // 2
    r2 = lambda a: a.reshape(1, -1)
    wkpe = W["w_in"][l][:, W["w_in"].shape[2] - MLA_ROPE:]
    z = lambda n: jnp.zeros((D_MODEL, n), F32)
    wk2 = jnp.concatenate([z(ROPE_LANE0), wkpe, z(HEAD_SLOT - ROPE_LANE0 - MLA_ROPE),
                           z(ROPE_LANE0), wkpe[:, half:], wkpe[:, :half], z(HEAD_SLOT - ROPE_LANE0 - MLA_ROPE)], axis=1)
    wqb = W["mla_wqb"][l].reshape(MLA_Q_RANK, MLA_HEADS, MLA_NOPE + MLA_ROPE)
    zq = lambda n: jnp.zeros((MLA_Q_RANK, MLA_HEADS, n), F32)
    rest = HEAD_SLOT - MLA_NOPE - MLA_ROPE
    wq_pad = jnp.concatenate([wqb, zq(rest)], axis=-1)
    wq_rot = jnp.concatenate([zq(MLA_NOPE), wqb[..., MLA_NOPE + half:], wqb[..., MLA_NOPE:MLA_NOPE + half], zq(rest)], axis=-1)
    hw = MLA_HEADS * HEAD_SLOT
    wq = jnp.concatenate([wq_pad.reshape(MLA_Q_RANK, hw), wq_rot.reshape(MLA_Q_RANK, hw)], axis=1)
    wuk = W["mla_wuk"][l]
    wuk_pad = jnp.concatenate([wuk, jnp.zeros((MLA_KV_RANK, MLA_HEADS, HEAD_SLOT - MLA_NOPE), F32)], axis=-1).reshape(MLA_KV_RANK, hw)
    wukt = wuk.transpose(1, 2, 0).reshape(MLA_HEADS * MLA_NOPE, MLA_KV_RANK)
    wuv = W["mla_wuv"][l].reshape(MLA_KV_RANK, MLA_HEADS * MLA_VDIM)
    proj = dict(g=r2(W["norm_mix"][l]), win=W["w_in"], layer=l, wk2=wk2.astype(BF16),
                qn=r2(W["mla_q_norm"][l]), wq=wq.astype(BF16), kvn=r2(W["mla_kv_norm"][l]), rope=_rope_rows(),
                wuk=wuk_pad.astype(BF16), wuv=wuv.astype(BF16), wukt=wukt.astype(BF16))
    zr = lambda n: jnp.zeros((n, RW_WIDTH), F32)
    rwk = dict(mu=r2(W["rw_mu"][l]), w0=r2(W["rw_w0"][l]),
               w2=jnp.concatenate([W["rw_w2"][l], zr(96)], axis=0).astype(BF16),
               a0=r2(W["rw_a0"][l]),
               a2=jnp.concatenate([zr(32), W["rw_a2"][l], zr(64)], axis=0).astype(BF16),
               g2=jnp.concatenate([zr(64), W["rw_g2"][l]], axis=0).astype(BF16),
               kk=r2(W["rw_kk"][l]), ka=r2(W["rw_ka"][l]), rk=r2(W["rw_rk"][l]),
               lnw=r2(W["rw_ln_w"][l]), lnb=r2(W["rw_ln_b"][l]))
    return dict(
        proj=proj, rwkv=rwk,
        ffn1=(r2(W["norm_ffn1"][l]), W["ffn1_wi"], W["ffn1_wo"], l),
        ffn2=(r2(W["norm_ffn2"][l]), W["ffn2_wi"], W["ffn2_wo"], l),
        wout=W["w_out"], hgn=r2(jnp.tile(W["hg_norm"][l], HG_HEADS)))


def _trunk(x, B, T, pos0, LW, W, sample, st=None):
    n = B * T
    x = x.reshape(n, D_MODEL)
    hg_out, rw_out, sh_out, c_out, kpe_out = [], [], [], [], []
    for l in range(DEPTH):
        lw = LW[l]
        ffn = _ffn_stream if sample else _ffn
        x = ffn(x, *lw["ffn1"])
        if sample:
            p_hg, p_rw, q, qlat, c, kpe = _proj(x, lw["proj"], T, pos0, True)
            o_hg, hg_s = _hgrn(p_hg, W["hg_lb_logits"], lw["hgn"], st["hg"][l], l, B, T, False)
            shift_rows = jnp.repeat(st["sh"][l], T, axis=0)
            o_rw, rw_s = _rwkv(p_rw, shift_rows, st["rw"][l].reshape(B, MIX_W, HEAD_D), lw["rwkv"], B, T, False)
            o_mla = _attn_sample(qlat, q, c, kpe, lw["proj"]["wuv"], st["ckv"], st["kpe"], st["pt"], l, B, T)
        else:
            p_hg, p_rw, q, k, v, c, kpe = _proj(x, lw["proj"], T, pos0, False)
            o_hg, hg_s, o_rw, rw_s = _mixers_prompt(p_hg, p_rw, W["hg_lb_logits"], lw["hgn"], lw["rwkv"], l, B, T)
            o_mla = _attn_prompt(q, k, v, B, T)
        gf = W["norm_final"].reshape(1, -1) if l == DEPTH - 1 else None
        x = ffn(x, *lw["ffn2"], mix=(o_hg, o_rw, o_mla), wout=lw["wout"], gf=gf)
        hg_out.append(hg_s)
        rw_out.append(rw_s.reshape(B, RW_HEADS, RW_DH, RW_DH))
        sh_out.append(p_rw.reshape(B, T, RW_COLS)[:, -1])
        c_out.append(c.reshape(B, T, MLA_KV_RANK))
        kpe_out.append(kpe.reshape(B, T, MLA_ROPE))
    st_ = jnp.stack
    return x.reshape(B, T, D_MODEL), st_(hg_out), st_(rw_out), st_(sh_out), st_(c_out), st_(kpe_out)


def kernel(x_prompt, x_sample, cache_mla_ckv, cache_mla_kpe, state_hgrn, state_rwkv, state_rwkv_shift, page_table, norm_ffn1, ffn1_wi, ffn1_wo, norm_mix, w_in, w_out, hg_lb_logits, hg_norm, rw_mu, rw_w0, rw_w2, rw_a0, rw_a2, rw_g2, rw_kk, rw_ka, rw_rk, rw_ln_w, rw_ln_b, mla_q_norm, mla_wqb, mla_kv_norm, mla_wuk, mla_wuv, norm_ffn2, ffn2_wi, ffn2_wo, norm_final):
    W = dict(norm_ffn1=norm_ffn1, ffn1_wi=ffn1_wi, ffn1_wo=ffn1_wo, norm_mix=norm_mix, w_in=w_in, w_out=w_out,
             hg_lb_logits=hg_lb_logits, hg_norm=hg_norm, rw_mu=rw_mu, rw_w0=rw_w0, rw_w2=rw_w2, rw_a0=rw_a0,
             rw_a2=rw_a2, rw_g2=rw_g2, rw_kk=rw_kk, rw_ka=rw_ka, rw_rk=rw_rk, rw_ln_w=rw_ln_w, rw_ln_b=rw_ln_b,
             mla_q_norm=mla_q_norm, mla_wqb=mla_wqb, mla_kv_norm=mla_kv_norm, mla_wuk=mla_wuk, mla_wuv=mla_wuv,
             norm_ffn2=norm_ffn2, ffn2_wi=ffn2_wi, ffn2_wo=ffn2_wo, norm_final=norm_final)
    LW = [_layer_weights(l, W) for l in range(DEPTH)]
    Bp, Tp, _ = x_prompt.shape
    y_p, hg_p, rw_p, sh_p, ckv_p, kpe_p = _trunk(x_prompt, Bp, Tp, 0, LW, W, False)
    Bs, Ts, _ = x_sample.shape
    past_len = page_table.shape[1] * cache_mla_ckv.shape[2]
    st = dict(hg=state_hgrn, rw=state_rwkv, sh=state_rwkv_shift, ckv=cache_mla_ckv,
              kpe=jnp.swapaxes(cache_mla_kpe, 2, 3), pt=page_table)
    y_s, hg_s, rw_s, sh_s, ckv_s, kpe_s = _trunk(x_sample, Bs, Ts, past_len, LW, W, True, st)
    return (y_p, y_s, hg_p, hg_s, rw_p, rw_s, sh_p, sh_s, ckv_p, ckv_s, kpe_p, kpe_s)
```

```python
import functools
import math

import jax
import jax.numpy as jnp
import numpy as np
from jax import lax
from jax.experimental import pallas as pl
from jax.experimental.pallas import tpu as pltpu

F32 = jnp.float32
BF16 = jnp.bfloat16

D_MODEL = 1024
DEPTH = 2
PAGE_SIZE = 128
EPS = 1e-6
D_FF = 2816
HG_HEADS = 4
HG_DK = 64
HG_KW = 256
HG_MIN_F = 1e-30
RW_HEADS = 4
RW_DH = 64
RW_WIDTH = 256
RW_COLS = 896
RW_GN_EPS = 64e-5
MLA_HEADS = 8
MLA_NOPE = 64
MLA_ROPE = 32
MLA_VDIM = 64
MLA_Q_RANK = 384
MLA_KV_RANK = 256
MLA_SCALE = (MLA_NOPE + MLA_ROPE) ** -0.5
ROPE_THETA = 10000.0

LANES = 128
HEAD_SLOT = 128
ROPE_LANE0 = 64
VMEM_LIMIT = 56 * 1024 * 1024
NEG_INF = -1e30


def _dot(a, b):
    return jnp.dot(a.astype(BF16), b.astype(BF16), preferred_element_type=F32)


def _dot_nt(a, b):
    return lax.dot_general(a.astype(BF16), b.astype(BF16), (((1,), (1,)), ((), ())), preferred_element_type=F32)


def _dot_xl(m, b, pieces=2):
    out = None
    rem = b
    for _ in range(pieces):
        part = rem.astype(BF16)
        term = jnp.dot(m, part, preferred_element_type=F32)
        out = term if out is None else out + term
        rem = rem - part.astype(F32)
    return out


def _dot_xr(a, m, pieces=2):
    out = None
    rem = a
    for _ in range(pieces):
        part = rem.astype(BF16)
        term = jnp.dot(part, m, preferred_element_type=F32)
        out = term if out is None else out + term
        rem = rem - part.astype(F32)
    return out


def _rms(x, g, eps=EPS):
    return x * lax.rsqrt(jnp.mean(x * x, axis=-1, keepdims=True) + eps) * g


def _sigmoid(x):
    return 1.0 / (1.0 + jnp.exp(-x))


def _iota(shape, dim):
    return lax.broadcasted_iota(jnp.int32, shape, dim)


def _head_block_ones(n, hd):
    return (_iota((n, n), 0) // hd == _iota((n, n), 1) // hd).astype(BF16)


def _full(shape):
    return pl.BlockSpec(shape, lambda *_: (0,) * len(shape), pipeline_mode=pl.Buffered(1))


def _layer_block(shape, layer):
    return pl.BlockSpec((None,) + tuple(shape[1:]), lambda *_: (layer,) + (0,) * (len(shape) - 1),
                        pipeline_mode=pl.Buffered(1))


def _params(sem):
    return pltpu.CompilerParams(dimension_semantics=sem, vmem_limit_bytes=VMEM_LIMIT)


FF_CHUNK = 256


def _ffn_kernel(*refs, has_mix, final):
    it = iter(refs)
    x_ref = next(it)
    if has_mix:
        ohg_ref, orw_ref, omla_ref, wout_ref = next(it), next(it), next(it), next(it)
    g_ref, wi_ref, wo_ref = next(it), next(it), next(it)
    gf_ref = next(it) if final else None
    o_ref = next(it)

    x = x_ref[...]
    if has_mix:
        o = jnp.concatenate([r[...].astype(BF16) for r in (ohg_ref, orw_ref, omla_ref)], axis=-1)
        x = x + jnp.dot(o, wout_ref[...].astype(BF16), preferred_element_type=F32)
    xn = _rms(x, g_ref[...]).astype(BF16)
    acc = jnp.zeros(x.shape, F32)
    for c in range(D_FF // FF_CHUNK):
        lo = c * FF_CHUNK
        a = jnp.dot(xn, wi_ref[:, lo:lo + FF_CHUNK].astype(BF16), preferred_element_type=F32)
        b = jnp.dot(xn, wi_ref[:, D_FF + lo:D_FF + lo + FF_CHUNK].astype(BF16), preferred_element_type=F32)
        h = (a * _sigmoid(a) * b).astype(BF16)
        acc = acc + jnp.dot(h, wo_ref[lo:lo + FF_CHUNK, :].astype(BF16), preferred_element_type=F32)
    y = x + 0.5 * acc
    if final:
        y = _rms(y, gf_ref[...])
    o_ref[...] = y


def _ffn(x, g, wi, wo, layer, mix=None, wout=None, gf=None, tm=512):
    n = x.shape[0]
    tm = min(tm, n)
    row = lambda w: pl.BlockSpec((tm, w), lambda i: (i, 0))
    args, specs = [x], [row(D_MODEL)]
    if mix is not None:
        for m in mix:
            args.append(m)
            specs.append(row(m.shape[1]))
        args.append(wout)
        specs.append(_layer_block(wout.shape, layer))
    args += [g, wi, wo]
    specs += [_full(g.shape), _layer_block(wi.shape, layer), _layer_block(wo.shape, layer)]
    if gf is not None:
        args.append(gf)
        specs.append(_full(gf.shape))
    return pl.pallas_call(
        functools.partial(_ffn_kernel, has_mix=mix is not None, final=gf is not None),
        grid=(n // tm,), in_specs=specs, out_specs=row(D_MODEL),
        out_shape=jax.ShapeDtypeStruct((n, D_MODEL), F32),
        compiler_params=_params(("parallel",)), name="ffn")(*args)


def _ffn_stream_kernel(*refs, has_mix, final):
    it = iter(refs)
    x_ref = next(it)
    if has_mix:
        ohg_ref, orw_ref, omla_ref, wout_ref = next(it), next(it), next(it), next(it)
    g_ref, wia_ref, wib_ref, wo_ref = next(it), next(it), next(it), next(it)
    gf_ref = next(it) if final else None
    o_ref = next(it)
    x_scr, xn_scr, acc_scr = next(it), next(it), next(it)
    c = pl.program_id(0)

    @pl.when(c == 0)
    def _():
        x = x_ref[...]
        if has_mix:
            o = jnp.concatenate([r[...].astype(BF16) for r in (ohg_ref, orw_ref, omla_ref)], axis=-1)
            x = x + jnp.dot(o, wout_ref[...].astype(BF16), preferred_element_type=F32)
        x_scr[...] = x
        xn_scr[...] = _rms(x, g_ref[...]).astype(BF16)
        acc_scr[...] = jnp.zeros(acc_scr.shape, F32)

    xn = xn_scr[...]
    a = jnp.dot(xn, wia_ref[...].astype(BF16), preferred_element_type=F32)
    b = jnp.dot(xn, wib_ref[...].astype(BF16), preferred_element_type=F32)
    h = (a * _sigmoid(a) * b).astype(BF16)
    acc_scr[...] += jnp.dot(h, wo_ref[...].astype(BF16), preferred_element_type=F32)

    @pl.when(c == pl.num_programs(0) - 1)
    def _():
        y = x_scr[...] + 0.5 * acc_scr[...]
        if final:
            y = _rms(y, gf_ref[...])
        o_ref[...] = y


def _ffn_stream(x, g, wi, wo, layer, mix=None, wout=None, gf=None):
    n = x.shape[0]
    n_chunks = D_FF // FF_CHUNK
    const = lambda a: pl.BlockSpec(a.shape, lambda c: (0,) * a.ndim)
    args, specs = [x], [const(x)]
    if mix is not None:
        args += list(mix) + [wout]
        specs += [const(m) for m in mix] + [_layer_block(wout.shape, layer)]
    args += [g, wi, wi, wo]
    specs += [const(g),
              pl.BlockSpec((None, D_MODEL, FF_CHUNK), lambda c: (layer, 0, c)),
              pl.BlockSpec((None, D_MODEL, FF_CHUNK), lambda c: (layer, 0, n_chunks + c)),
              pl.BlockSpec((None, FF_CHUNK, D_MODEL), lambda c: (layer, c, 0))]
    if gf is not None:
        args.append(gf)
        specs.append(const(gf))
    return pl.pallas_call(
        functools.partial(_ffn_stream_kernel, has_mix=mix is not None, final=gf is not None),
        grid=(n_chunks,), in_specs=specs, out_specs=pl.BlockSpec((n, D_MODEL), lambda c: (0, 0)),
        out_shape=jax.ShapeDtypeStruct((n, D_MODEL), F32),
        scratch_shapes=[pltpu.VMEM((n, D_MODEL), F32), pltpu.VMEM((n, D_MODEL), BF16), pltpu.VMEM((n, D_MODEL), F32)],
        compiler_params=_params(("arbitrary",)), name="ffn_stream")(*args)


def _proj_kernel(*refs, seq_len, pos0, sample):
    it = iter(refs)
    x_ref, g_ref, win_ref = (next(it) for _ in range(3))
    qn_ref, wq_ref, kvn_ref, rope_ref = (next(it) for _ in range(4))
    if sample:
        wukt_ref = next(it)
    else:
        wuk_ref, wuv_ref = next(it), next(it)
    phg_ref, prw_ref, q_ref = next(it), next(it), next(it)
    if sample:
        qlat_ref = next(it)
    else:
        k_ref, v_ref = next(it), next(it)
    c_ref, kpe_ref = next(it), next(it)

    tm = x_ref.shape[0]
    xn = _rms(x_ref[...], g_ref[...]).astype(BF16)
    phg_ref[...] = _dot_nt(xn, win_ref[0:1024, :])
    prw_ref[...] = _dot_nt(xn, win_ref[1024:1920, :])
    pm = _dot_nt(xn, win_ref[1920:2560, :])
    wk = win_ref[2560:2560 + MLA_ROPE, :]
    zrow = lambda n_: jnp.zeros((n_, D_MODEL), F32)
    tail = HEAD_SLOT - ROPE_LANE0 - MLA_ROPE
    wk2 = jnp.concatenate([zrow(ROPE_LANE0), wk, zrow(tail),
                           zrow(ROPE_LANE0), wk[MLA_ROPE // 2:], wk[:MLA_ROPE // 2], zrow(tail)], axis=0)
    pk = _dot_nt(xn, wk2)

    row = pl.program_id(0) * tm + _iota((tm, 1), 0)
    pos = (pos0 + (row & (seq_len - 1))).astype(F32)
    rope = rope_ref[...]
    ang = pos * rope[0:1, :]
    cos_s = rope[1:2, :] + rope[3:4, :] * jnp.cos(ang)
    sin_s = rope[2:3, :] * jnp.sin(ang)

    c = _rms(pm[:, MLA_Q_RANK:MLA_Q_RANK + MLA_KV_RANK], kvn_ref[...])
    c_ref[...] = c
    k_slot = pk[:, :HEAD_SLOT] * cos_s + pk[:, HEAD_SLOT:] * sin_s
    kpe_ref[...] = k_slot[:, ROPE_LANE0:ROPE_LANE0 + MLA_ROPE]

    qn = _rms(pm[:, :MLA_Q_RANK], qn_ref[...]).astype(BF16)
    q2 = jnp.dot(qn, wq_ref[...], preferred_element_type=F32)
    hw = MLA_HEADS * HEAD_SLOT
    q_scale = MLA_SCALE if sample else MLA_SCALE * math.log2(math.e)
    cos_q = jnp.concatenate([cos_s * q_scale] * MLA_HEADS, axis=-1)
    sin_q = jnp.concatenate([sin_s * q_scale] * MLA_HEADS, axis=-1)
    q = q2[:, :hw] * cos_q + q2[:, hw:] * sin_q
    if sample:
        q_ref[...] = q
        for h in range(MLA_HEADS):
            qh = q[:, h * HEAD_SLOT:h * HEAD_SLOT + MLA_NOPE]
            qlat_ref[:, h * MLA_KV_RANK:(h + 1) * MLA_KV_RANK] = _dot(qh, wukt_ref[h * MLA_NOPE:(h + 1) * MLA_NOPE, :])
    else:
        q_ref[...] = q.astype(BF16)
        cb = c.astype(BF16)
        kn = jnp.dot(cb, wuk_ref[...], preferred_element_type=F32)
        k_ref[...] = (kn + jnp.concatenate([k_slot] * MLA_HEADS, axis=-1)).astype(BF16)
        v_ref[...] = jnp.dot(cb, wuv_ref[...], preferred_element_type=F32).astype(BF16)


def _proj(x, pw, seq_len, pos0, sample, tm=512):
    n = x.shape[0]
    tm = min(tm, n)
    row = lambda w: pl.BlockSpec((tm, w), lambda i: (i, 0))
    hw = MLA_HEADS * HEAD_SLOT
    args = [x, pw["g"], pw["win"], pw["qn"], pw["wq"], pw["kvn"], pw["rope"]]
    args += [pw["wukt"]] if sample else [pw["wuk"], pw["wuv"]]
    specs = [row(D_MODEL)] + [_full(a.shape) for a in args[1:]]
    specs[2] = _layer_block(pw["win"].shape, pw["layer"])
    qdt = F32 if sample else BF16
    outs = [((n, 1024), F32), ((n, RW_COLS), F32), ((n, hw), qdt)]
    outs += [((n, MLA_HEADS * MLA_KV_RANK), F32)] if sample else [((n, hw), BF16), ((n, MLA_HEADS * MLA_VDIM), BF16)]
    outs += [((n, MLA_KV_RANK), F32), ((n, MLA_ROPE), F32)]
    return pl.pallas_call(
        functools.partial(_proj_kernel, seq_len=seq_len, pos0=pos0, sample=sample),
        grid=(n // tm,), in_specs=specs, out_specs=[row(s[1]) for s, _ in outs],
        out_shape=[jax.ShapeDtypeStruct(s, d) for s, d in outs],
        compiler_params=_params(("parallel",)), name="proj")(*args)


MIX_W = 256
HEAD_D = 64


def _chunk_masks(L, C):
    r, c = _iota((L, L), 0), _iota((L, L), 1)
    same = (r // C) == (c // C)
    return same, same & (c <= r), same & (c < r)


def _chunk_last(a, nck, C):
    a3 = a.reshape(nck, C, a.shape[-1])
    return jnp.broadcast_to(a3[:, C - 1:C, :], a3.shape).reshape(a.shape)


def _expand_state(s_kv):
    t = jnp.concatenate([s_kv] * (MIX_W // HEAD_D), axis=0)
    bd = (_iota((MIX_W, MIX_W), 0) // HEAD_D) == (_iota((MIX_W, MIX_W), 1) // HEAD_D)
    return jnp.where(bd, t, 0.0)


def _fold_state(hbd):
    out = hbd[0:HEAD_D]
    for h in range(1, MIX_W // HEAD_D):
        out = out + hbd[h * HEAD_D:(h + 1) * HEAD_D]
    return out


def _hgrn_steps(*refs, L, C, layer, chained, own_init=True):
    it = iter(refs)
    p_ref, lbl_ref, gn_ref = next(it), next(it), next(it)
    s0_ref = None if chained else next(it)
    o_ref, sout_ref = next(it), next(it)
    h_scr = next(it) if chained else None

    nck = L // C
    p = p_ref[...]
    q, fz, vi, gate = p[:, 0:256], p[:, 256:512], p[:, 512:768], p[:, 768:1024]

    lg = lbl_ref[...]
    e = jnp.exp(lg - jnp.max(lg, axis=0, keepdims=True))
    pr = e / jnp.sum(e, axis=0, keepdims=True)
    cs = pr[0:1]
    for i in range(1, layer + 1):
        cs = cs + pr[i:i + 1]
    lb = jnp.clip(cs - pr[0:1], 0.0, 1.0)

    f = lb + (1.0 - lb) * _sigmoid(fz)
    log_f = jnp.log(jnp.maximum(f, HG_MIN_F))
    k = (1.0 - lb) * _sigmoid(-fz)

    same, incl, _ = _chunk_masks(L, C)
    b = _dot_xl(incl.astype(BF16), log_f, pieces=3)
    b_last = _chunk_last(b, nck, C)
    ones_h = _head_block_ones(MIX_W, HEAD_D)

    b3, q3, k3, v3 = (a.reshape(nck, C, MIX_W) for a in (b, q, k, vi))
    SUB = 8
    tiles = [jnp.zeros((nck, min(SUB, C), MIX_W), F32) for _ in range(max(C // SUB, 1))]
    for s in range(C):
        t0 = (s // SUB) * SUB
        bt, qt = b3[:, t0:, :], q3[:, t0:, :]
        rows = C - t0
        w = jnp.exp(bt - b3[:, s:s + 1, :])
        x = jnp.where(t0 + _iota((nck, rows, MIX_W), 1) >= s, qt * w * k3[:, s:s + 1, :], 0.0)
        a_s = _dot(x.reshape(nck * rows, MIX_W), ones_h).reshape(nck, rows, MIX_W)
        upd = a_s * v3[:, s:s + 1, :]
        for ti in range(t0 // SUB, len(tiles)):
            lo = ti * SUB - t0
            tiles[ti] = tiles[ti] + upd[:, lo:lo + SUB, :]
        yield
    o_intra = jnp.concatenate(tiles, axis=1).reshape(L, MIX_W)

    qe = q * jnp.exp(b)
    kd_t = (k * jnp.exp(b_last - b)).T
    e_t = jnp.exp(b_last).T
    bd = (_iota((MIX_W, MIX_W), 0) // HEAD_D) == (_iota((MIX_W, MIX_W), 1) // HEAD_D)
    col = _iota((1, L), 1)

    if chained:
        if own_init:
            @pl.when(pl.program_id(1) == 0)
            def _():
                h_scr[...] = jnp.zeros((MIX_W, MIX_W), F32)
        hbd = h_scr[...]
    gks = []
    for ck in range(nck):
        gks.append(jnp.where(bd, _dot(jnp.where((col // C) == ck, kd_t, 0.0), vi), 0.0))
        yield
    dcols = [jnp.sum(jnp.where(col == ck * C, e_t, 0.0), axis=1, keepdims=True) for ck in range(nck)]
    o_parts = []
    for ck in range(nck):
        if not chained:
            s_kv = jnp.concatenate([s0_ref[ck, h] for h in range(HG_HEADS)], axis=-1)
            hbd = _expand_state(s_kv)
        o_parts.append(_dot(qe[ck * C:(ck + 1) * C], hbd))
        hbd = dcols[ck] * hbd + gks[ck]
        if not chained:
            s_new = _fold_state(hbd)
            for h in range(HG_HEADS):
                sout_ref[ck, h] = s_new[:, h * HEAD_D:(h + 1) * HEAD_D]
    if chained:
        h_scr[...] = hbd
        s_new = _fold_state(hbd)
        for h in range(HG_HEADS):
            sout_ref[h] = s_new[:, h * HEAD_D:(h + 1) * HEAD_D]

    o = o_intra + jnp.concatenate(o_parts, axis=0)
    ms = _dot_xr(o * o, ones_h) * (1.0 / HEAD_D)
    on = o * lax.rsqrt(ms + EPS) * gn_ref[...]
    o_ref[...] = (on * (gate * _sigmoid(gate))).astype(o_ref.dtype)


def _drain(*gens):
    alive = list(gens)
    while alive:
        for g in list(alive):
            try:
                next(g)
            except StopIteration:
                alive.remove(g)


def _hgrn_kernel(*refs, **kw):
    _drain(_hgrn_steps(*refs, **kw))


def _hgrn(p_hg, lb_logits, gn, s0, layer, B, T, chained):
    n = p_hg.shape[0]
    if chained:
        L, C = min(256, T), min(16, T)
        nblk = T // L
        grid = (B, nblk)
        rowmap = lambda b, j: (b * nblk + j, 0)
        in_specs = [pl.BlockSpec((L, 1024), rowmap), _full(lb_logits.shape), _full(gn.shape)]
        args = [p_hg, lb_logits, gn]
        out_specs = [pl.BlockSpec((L, MIX_W), rowmap),
                     pl.BlockSpec((None, HG_HEADS, HEAD_D, HEAD_D), lambda b, j: (b, 0, 0, 0))]
        scratch = [pltpu.VMEM((MIX_W, MIX_W), F32)]
        sem = ("parallel", "arbitrary")
    else:
        L, C = n, T
        grid = (1,)
        in_specs = [_full((L, 1024)), _full(lb_logits.shape), _full(gn.shape), _full(s0.shape)]
        args = [p_hg, lb_logits, gn, s0]
        out_specs = [_full((L, MIX_W)), _full((B, HG_HEADS, HEAD_D, HEAD_D))]
        scratch = []
        sem = ("arbitrary",)
    return pl.pallas_call(
        functools.partial(_hgrn_kernel, L=L, C=C, layer=layer, chained=chained),
        grid=grid, in_specs=in_specs, out_specs=out_specs,
        out_shape=[jax.ShapeDtypeStruct((n, MIX_W), BF16),
                   jax.ShapeDtypeStruct((B, HG_HEADS, HEAD_D, HEAD_D), F32)],
        scratch_shapes=scratch, compiler_params=_params(sem), name="hgrn")(*args)


def _rwkv_steps(*refs, L, C, chained, own_init=True):
    it = iter(refs)
    p_ref = next(it)
    if not chained:
        shrow_ref, s0_ref = next(it), next(it)
    (mu_ref, w0_ref, w2_ref, a0_ref, a2_ref, g2_ref, kkw_ref, ka_ref, rk_ref, lnw_ref, lnb_ref) = (next(it) for _ in range(11))
    y_ref, sout_ref = next(it), next(it)
    if chained:
        h_scr, carry_scr = next(it), next(it)

    nck = L // C
    p = p_ref[...]
    rowi = _iota((L, 1), 0)
    rolled = pltpu.roll(p, 1, 0)
    if chained:
        if own_init:
            @pl.when(pl.program_id(1) == 0)
            def _():
                h_scr[...] = jnp.zeros((MIX_W, MIX_W), F32)
                carry_scr[...] = jnp.zeros(carry_scr.shape, F32)
        prev = jnp.where(rowi == 0, carry_scr[0:1, :], rolled)
        carry_scr[0:1, :] = p[L - 1:L, :]
    else:
        prev = jnp.where((rowi & (C - 1)) == 0, shrow_ref[...], rolled)
    ps = p + (prev - p) * mu_ref[...]
    r, k, v, tail = ps[:, 0:256], ps[:, 256:512], ps[:, 512:768], ps[:, 768:896]

    w_raw = -jax.nn.softplus(-(w0_ref[...] + _dot(jnp.tanh(tail), w2_ref[...]))) - 0.5
    lw = -jnp.exp(w_raw)
    a = _sigmoid(a0_ref[...] + _dot(tail, a2_ref[...]))
    g = _dot(_sigmoid(tail), g2_ref[...])

    ones_h = _head_block_ones(MIX_W, HEAD_D)
    kk = k * kkw_ref[...]
    kk = kk / jnp.maximum(jnp.sqrt(_dot_xr(kk * kk, ones_h)), 1e-12)
    k2 = k * (1.0 + (a - 1.0) * ka_ref[...])
    bb = kk * a

    same, incl, strict = _chunk_masks(L, C)
    gc = _dot_xl(incl.astype(BF16), lw)
    g_last = _chunk_last(gc, nck, C)
    eg, eng = jnp.exp(gc), jnp.exp(-gc)
    rg, kg, bg, kkg = r * eg, k2 * eng, bb * eng, kk * jnp.exp(gc - lw)
    to_end = jnp.exp(g_last - gc)
    kd_t, bd_t, e_t = (k2 * to_end).T, (bb * to_end).T, jnp.exp(g_last).T

    lane_h = _iota((1, MIX_W), 1) // HEAD_D
    eye = (_iota((L, L), 0) == _iota((L, L), 1)).astype(F32)
    u0 = jnp.zeros((L, MIX_W), F32)
    kkt = jnp.zeros((L, MIX_W), F32)
    qp = rg
    o0 = jnp.zeros((L, MIX_W), F32)
    heads = range(RW_HEADS)
    mhs = [lane_h == h for h in heads]
    kkg_hs = [jnp.where(mh, kkg, 0.0) for mh in mhs]
    rg_hs = [jnp.where(mh, rg, 0.0) for mh in mhs]
    a_kb, a_kk, a_rk, a_rb = [], [], [], []
    for h in heads:
        a_kb.append(jnp.where(strict, _dot_nt(kkg_hs[h], bg), 0.0))
        a_kk.append(jnp.where(strict, _dot_nt(kkg_hs[h], kg), 0.0))
        yield
    for h in heads:
        a_rk.append(jnp.where(incl, _dot_nt(rg_hs[h], kg), 0.0))
        a_rb.append(jnp.where(incl, _dot_nt(rg_hs[h], bg), 0.0))
        yield
    npow = [-a for a in a_kb]
    tinv = [eye + n_ for n_ in npow]
    for _ in range(int(math.log2(C)) - 1):
        npow = [_dot(n_, n_) for n_ in npow]
        yield
        tinv = [t_ + _dot(t_, n_) for t_, n_ in zip(tinv, npow)]
        yield
    x1 = [_dot(a_kk[h], v) for h in heads]
    yield
    u0_hs = [_dot(tinv[h], x1[h]) for h in heads]
    yield
    kkt_hs = [_dot(tinv[h], kkg) for h in heads]
    yield
    for h in heads:
        u0 = jnp.where(mhs[h], u0_hs[h], u0)
        kkt = jnp.where(mhs[h], kkt_hs[h], kkt)
    for h in heads:
        qp = qp - jnp.where(mhs[h], _dot(a_rb[h], kkt_hs[h]), 0.0)
        o0 = o0 + jnp.where(mhs[h], _dot(a_rk[h], v) - _dot(a_rb[h], u0_hs[h]), 0.0)
        yield

    bd = (_iota((MIX_W, MIX_W), 0) // HEAD_D) == (_iota((MIX_W, MIX_W), 1) // HEAD_D)
    col = _iota((1, L), 1)
    if chained:
        hbd = h_scr[...]
    bd_cs = [jnp.where((col // C) == ck, bd_t, 0.0) for ck in range(nck)]
    mbs, gbs = [], []
    for ck in range(nck):
        mbs.append(jnp.where(bd, _dot(bd_cs[ck], kkt), 0.0))
        gbs.append(jnp.where(bd, _dot(jnp.where((col // C) == ck, kd_t, 0.0), v) - _dot(bd_cs[ck], u0), 0.0))
        yield
    dcols = [jnp.sum(jnp.where(col == ck * C, e_t, 0.0), axis=1, keepdims=True) for ck in range(nck)]
    y_parts = []
    for ck in range(nck):
        if not chained:
            hbd = _expand_state(s0_ref[ck].T)
        y_parts.append(_dot(qp[ck * C:(ck + 1) * C], hbd))
        hbd = dcols[ck] * hbd - _dot(mbs[ck], hbd) + gbs[ck]
        yield
        if not chained:
            sout_ref[ck] = _fold_state(hbd).T
    if chained:
        h_scr[...] = hbd
        sout_ref[...] = _fold_state(hbd).T

    y = o0 + jnp.concatenate(y_parts, axis=0)
    mean = _dot_xr(y, ones_h) * (1.0 / HEAD_D)
    yc = y - mean
    var = _dot_xr(yc * yc, ones_h) * (1.0 / HEAD_D)
    yn = yc * lax.rsqrt(var + RW_GN_EPS) * lnw_ref[...] + lnb_ref[...]
    bonus = _dot_xr(r * k2 * rk_ref[...], ones_h) * v
    y_ref[...] = ((yn + bonus) * g).astype(y_ref.dtype)


def _rwkv_kernel(*refs, **kw):
    _drain(_rwkv_steps(*refs, **kw))


def _rwkv(p_rw, shift_rows, s0, rw, B, T, chained):
    n = p_rw.shape[0]
    names = ("mu", "w0", "w2", "a0", "a2", "g2", "kk", "ka", "rk", "lnw", "lnb")
    wargs = [rw[k] for k in names]
    wspecs = [_full(a.shape) for a in wargs]
    if chained:
        L, C = min(256, T), min(64, T)
        nblk = T // L
        grid = (B, nblk)
        rowmap = lambda b, j: (b * nblk + j, 0)
        in_specs = [pl.BlockSpec((L, RW_COLS), rowmap)] + wspecs
        args = [p_rw] + wargs
        out_specs = [pl.BlockSpec((L, MIX_W), rowmap), pl.BlockSpec((None, MIX_W, HEAD_D), lambda b, j: (b, 0, 0))]
        scratch = [pltpu.VMEM((MIX_W, MIX_W), F32), pltpu.VMEM((8, RW_COLS), F32)]
        sem = ("parallel", "arbitrary")
    else:
        L, C = n, T
        grid = (1,)
        in_specs = [_full((L, RW_COLS)), _full(shift_rows.shape), _full(s0.shape)] + wspecs
        args = [p_rw, shift_rows, s0] + wargs
        out_specs = [_full((L, MIX_W)), _full((B, MIX_W, HEAD_D))]
        scratch = []
        sem = ("arbitrary",)
    return pl.pallas_call(
        functools.partial(_rwkv_kernel, L=L, C=C, chained=chained),
        grid=grid, in_specs=in_specs, out_specs=out_specs,
        out_shape=[jax.ShapeDtypeStruct((n, MIX_W), BF16), jax.ShapeDtypeStruct((B, MIX_W, HEAD_D), F32)],
        scratch_shapes=scratch, compiler_params=_params(sem), name="rwkv")(*args)


def _mixers_kernel(*refs, L, hg_c, rw_c, layer):
    hg_in, rw_in = refs[0:3], refs[3:15]
    hg_out, rw_out = refs[15:17], refs[17:19]
    hg_scr, rw_scr = refs[19:20], refs[20:22]

    @pl.when(pl.program_id(1) == 0)
    def _():
        for scr in (*hg_scr, *rw_scr):
            scr[...] = jnp.zeros(scr.shape, F32)
    _drain(_rwkv_steps(*rw_in, *rw_out, *rw_scr, L=L, C=rw_c, chained=True, own_init=False),
           _hgrn_steps(*hg_in, *hg_out, *hg_scr, L=L, C=hg_c, layer=layer, chained=True, own_init=False))


def _mixers_prompt(p_hg, p_rw, lb_logits, gn, rw, layer, B, T):
    n = p_hg.shape[0]
    names = ("mu", "w0", "w2", "a0", "a2", "g2", "kk", "ka", "rk", "lnw", "lnb")
    wargs = [rw[k] for k in names]
    L = min(256, T)
    nblk = T // L
    rowmap = lambda b, j: (b * nblk + j, 0)
    in_specs = [pl.BlockSpec((L, 1024), rowmap), _full(lb_logits.shape), _full(gn.shape),
                pl.BlockSpec((L, RW_COLS), rowmap)] + [_full(a.shape) for a in wargs]
    out_specs = [pl.BlockSpec((L, MIX_W), rowmap),
                 pl.BlockSpec((None, HG_HEADS, HEAD_D, HEAD_D), lambda b, j: (b, 0, 0, 0)),
                 pl.BlockSpec((L, MIX_W), rowmap),
                 pl.BlockSpec((None, MIX_W, HEAD_D), lambda b, j: (b, 0, 0))]
    return pl.pallas_call(
        functools.partial(_mixers_kernel, L=L, hg_c=min(16, T), rw_c=min(64, T), layer=layer),
        grid=(B, nblk), in_specs=in_specs, out_specs=out_specs,
        out_shape=[jax.ShapeDtypeStruct((n, MIX_W), BF16),
                   jax.ShapeDtypeStruct((B, HG_HEADS, HEAD_D, HEAD_D), F32),
                   jax.ShapeDtypeStruct((n, MIX_W), BF16),
                   jax.ShapeDtypeStruct((B, MIX_W, HEAD_D), F32)],
        scratch_shapes=[pltpu.VMEM((MIX_W, MIX_W), F32), pltpu.VMEM((MIX_W, MIX_W), F32), pltpu.VMEM((8, RW_COLS), F32)],
        compiler_params=_params(("parallel", "arbitrary")), name="mixers")(p_hg, lb_logits, gn, p_rw, *wargs)


def _attn_prompt_kernel(q_ref, k_ref, v_ref, o_ref, *, tq):
    i = pl.program_id(2)
    qs = [q_ref[:, hh * HEAD_SLOT:(hh + 1) * HEAD_SLOT] for hh in range(2)]

    def step(j, carry, diagonal):
        start = pl.multiple_of(j * tq, tq)
        vb = v_ref[pl.ds(start, tq), :]
        new = []
        for hh in range(2):
            m, l, acc = carry[hh]
            kb = k_ref[pl.ds(start, tq), hh * HEAD_SLOT:(hh + 1) * HEAD_SLOT]
            s = lax.dot_general(qs[hh], kb, (((1,), (1,)), ((), ())), preferred_element_type=F32)
            if diagonal:
                s = jnp.where(_iota((tq, tq), 1) <= _iota((tq, tq), 0), s, NEG_INF)
            m_new = jnp.maximum(m, jnp.max(s, axis=-1, keepdims=True))
            pexp = jnp.exp2(s - m_new)
            alpha = jnp.exp2(m - m_new)
            l = alpha * l + jnp.sum(pexp, axis=-1, keepdims=True)
            acc = alpha * acc + jnp.dot(pexp.astype(BF16), vb, preferred_element_type=F32)
            new.append((m_new, l, acc))
        return tuple(new)

    one = (jnp.full((tq, 1), NEG_INF, F32), jnp.zeros((tq, 1), F32), jnp.zeros((tq, 2 * MLA_VDIM), F32))
    carry = lax.fori_loop(0, i, lambda j, c: step(j, c, False), (one, one))
    (_, l0, acc0), (_, l1, acc1) = step(i, carry, True)
    lane = _iota((tq, 2 * MLA_VDIM), 1)
    o_ref[...] = jnp.where(lane < MLA_VDIM, acc0 / l0, acc1 / l1).astype(o_ref.dtype)


def _attn_prompt(q, k, v, B, T, tq=1024):
    n = q.shape[0]
    tq = min(tq, T)
    nq = T // tq
    return pl.pallas_call(
        functools.partial(_attn_prompt_kernel, tq=tq),
        grid=(B, MLA_HEADS // 2, nq),
        in_specs=[pl.BlockSpec((tq, 2 * HEAD_SLOT), lambda b, hp, i: (b * nq + i, hp)),
                  pl.BlockSpec((T, 2 * HEAD_SLOT), lambda b, hp, i: (b, hp)),
                  pl.BlockSpec((T, 2 * MLA_VDIM), lambda b, hp, i: (b, hp))],
        out_specs=pl.BlockSpec((tq, 2 * MLA_VDIM), lambda b, hp, i: (b * nq + i, hp)),
        out_shape=jax.ShapeDtypeStruct((n, MLA_HEADS * MLA_VDIM), BF16),
        compiler_params=_params(("parallel", "parallel", "arbitrary")), name="attn_prompt")(q, k, v)


PAGE_GROUP = 16
SEQ_PER_STEP = 4


def _attn_sample_kernel(pt_ref, qlat_ref, q_ref, c_ref, kpe_ref, wuv_ref, ckv_hbm, kpe_hbm, o_ref,
                        cbuf, kbuf, sem, *, T, layer, n_pages):
    G, NS = PAGE_GROUP, SEQ_PER_STEP
    b0 = pl.program_id(0) * NS
    n_groups = n_pages // G
    R = MLA_HEADS * T

    def page_copies(g, slot, first_seq=b0):
        out = []
        for sq in range(NS):
            for i in range(G):
                pg = pt_ref[first_seq + sq, g * G + i]
                out.append(pltpu.make_async_copy(ckv_hbm.at[layer, pg], cbuf.at[slot, sq, i], sem.at[0, slot]))
                out.append(pltpu.make_async_copy(kpe_hbm.at[layer, pg], kbuf.at[slot, sq, i], sem.at[1, slot]))
        return out

    carry_over = n_groups % 2 == 0
    first_step = pl.program_id(0) == 0
    last_step = pl.program_id(0) == pl.num_programs(0) - 1

    def start_first_group():
        for cp in page_copies(0, 0):
            cp.start()

    if carry_over:
        pl.when(first_step)(start_first_group)
    else:
        start_first_group()

    qls, qps = [], []
    for sq in range(NS):
        rows = slice(sq * T, (sq + 1) * T)
        qls.append(jnp.concatenate([qlat_ref[rows, h * MLA_KV_RANK:(h + 1) * MLA_KV_RANK]
                                    for h in range(MLA_HEADS)], axis=0).astype(BF16))
        qs = jnp.concatenate([q_ref[rows, h * HEAD_SLOT:(h + 1) * HEAD_SLOT] for h in range(MLA_HEADS)], axis=0)
        qps.append(qs[:, ROPE_LANE0:ROPE_LANE0 + MLA_ROPE].astype(BF16))

    def update(carry, s, cb):
        m, l, acc = carry
        m_new = jnp.maximum(m, jnp.max(s, axis=-1, keepdims=True))
        pexp = jnp.exp(s - m_new)
        alpha = jnp.exp(m - m_new)
        l = alpha * l + jnp.sum(pexp, axis=-1, keepdims=True)
        acc = alpha * acc + jnp.dot(pexp.astype(BF16), cb, preferred_element_type=F32)
        return m_new, l, acc

    def group(g, carries):
        slot = g & 1

        @pl.when(g + 1 < n_groups)
        def _():
            for cp in page_copies(g + 1, 1 - slot):
                cp.start()
        if carry_over:
            @pl.when((g + 1 == n_groups) & jnp.logical_not(last_step))
            def _():
                for cp in page_copies(0, 0, b0 + NS):
                    cp.start()
        for cp in page_copies(g, slot):
            cp.wait()
        seqs = range(NS)
        cbs = [cbuf[slot, sq].reshape(G * PAGE_SIZE, MLA_KV_RANK).astype(BF16) for sq in seqs]
        kts = [kbuf[slot, sq].astype(BF16) for sq in seqs]
        ss = [_dot_nt(qls[sq], cbs[sq]) for sq in seqs]
        ss = [ss[sq] + jnp.concatenate([jnp.dot(qps[sq], kts[sq][i], preferred_element_type=F32) for i in range(G)], axis=-1)
              for sq in seqs]
        ms = [jnp.maximum(carries[sq][0], jnp.max(ss[sq], axis=-1, keepdims=True)) for sq in seqs]
        ps = [jnp.exp(ss[sq] - ms[sq]) for sq in seqs]
        alphas = [jnp.exp(carries[sq][0] - ms[sq]) for sq in seqs]
        ls = [alphas[sq] * carries[sq][1] + jnp.sum(ps[sq], axis=-1, keepdims=True) for sq in seqs]
        accs = [alphas[sq] * carries[sq][2] + jnp.dot(ps[sq].astype(BF16), cbs[sq], preferred_element_type=F32) for sq in seqs]
        return tuple((ms[sq], ls[sq], accs[sq]) for sq in seqs)

    one = (jnp.full((R, 1), NEG_INF, F32), jnp.zeros((R, 1), F32), jnp.zeros((R, MLA_KV_RANK), F32))
    carries = lax.fori_loop(0, n_groups, group, (one,) * NS)

    pad = PAGE_SIZE - T
    tq = _iota((R, PAGE_SIZE), 0) & (T - 1)
    lane_h = _iota((T, MLA_HEADS * MLA_VDIM), 1) // MLA_VDIM
    for sq in range(NS):
        rows = slice(sq * T, (sq + 1) * T)
        cn = jnp.concatenate([c_ref[rows, :], jnp.zeros((pad, MLA_KV_RANK), F32)], axis=0).astype(BF16)
        kn = jnp.concatenate([kpe_ref[rows, :], jnp.zeros((pad, MLA_ROPE), F32)], axis=0).astype(BF16)
        s = jnp.where(_iota((R, PAGE_SIZE), 1) <= tq, _dot_nt(qls[sq], cn) + _dot_nt(qps[sq], kn), NEG_INF)
        _, l, acc = update(carries[sq], s, cn)
        z = _dot(acc / l, wuv_ref[...])
        out = jnp.zeros((T, MLA_HEADS * MLA_VDIM), F32)
        for h in range(MLA_HEADS):
            out = jnp.where(lane_h == h, z[h * T:(h + 1) * T], out)
        o_ref[rows, :] = out.astype(o_ref.dtype)


def _attn_sample(qlat, q, c, kpe, wuv, cache_ckv, cache_kpe_t, page_table, layer, B, T):
    n = qlat.shape[0]
    n_pages = page_table.shape[1]
    ns = SEQ_PER_STEP
    row = lambda w: pl.BlockSpec((ns * T, w), lambda b, pt: (b, 0))
    hbm = pl.BlockSpec(memory_space=pl.ANY)
    grid_spec = pltpu.PrefetchScalarGridSpec(
        num_scalar_prefetch=1, grid=(B // ns,),
        in_specs=[row(qlat.shape[1]), row(q.shape[1]), row(MLA_KV_RANK), row(MLA_ROPE),
                  pl.BlockSpec(wuv.shape, lambda b, pt: (0, 0)), hbm, hbm],
        out_specs=row(MLA_HEADS * MLA_VDIM),
        scratch_shapes=[pltpu.VMEM((2, ns, PAGE_GROUP, PAGE_SIZE, MLA_KV_RANK), F32),
                        pltpu.VMEM((2, ns, PAGE_GROUP, MLA_ROPE, PAGE_SIZE), F32),
                        pltpu.SemaphoreType.DMA((2, 2))])
    return pl.pallas_call(
        functools.partial(_attn_sample_kernel, T=T, layer=layer, n_pages=n_pages), grid_spec=grid_spec,
        out_shape=jax.ShapeDtypeStruct((n, MLA_HEADS * MLA_VDIM), F32),
        compiler_params=_params(("arbitrary",)), name="attn_sample")(
            page_table, qlat, q, c, kpe, wuv, cache_ckv, cache_kpe_t)


def _rope_rows():
    half = MLA_ROPE // 2
    inv = jnp.exp(-jnp.log(ROPE_THETA) * jnp.arange(half, dtype=F32) / half)
    lane = np.arange(LANES)
    on = (lane >= ROPE_LANE0) & (lane < ROPE_LANE0 + MLA_ROPE)
    inv_l = jnp.where(on, inv[(lane - ROPE_LANE0) % half], 0.0)
    sign = np.where(on, np.where(lane < ROPE_LANE0 + half, -1.0, 1.0), 0.0)
    rows = [inv_l, (lane < ROPE_LANE0).astype(np.float32), sign, on.astype(np.float32)]
    rows += [np.zeros(LANES, np.float32)] * 4
    return jnp.stack([jnp.asarray(r, F32) for r in rows])


def _layer_weights(l, W):
    half = MLA_ROPE // 2
    r2 = lambda a: a.reshape(1, -1)
    win_t = jnp.swapaxes(W["w_in"], 1, 2)
    wqb = W["mla_wqb"][l].reshape(MLA_Q_RANK, MLA_HEADS, MLA_NOPE + MLA_ROPE)
    zq = lambda n: jnp.zeros((MLA_Q_RANK, MLA_HEADS, n), F32)
    rest = HEAD_SLOT - MLA_NOPE - MLA_ROPE
    wq_pad = jnp.concatenate([wqb, zq(rest)], axis=-1)
    wq_rot = jnp.concatenate([zq(MLA_NOPE), wqb[..., MLA_NOPE + half:], wqb[..., MLA_NOPE:MLA_NOPE + half], zq(rest)], axis=-1)
    hw = MLA_HEADS * HEAD_SLOT
    wq = jnp.concatenate([wq_pad.reshape(MLA_Q_RANK, hw), wq_rot.reshape(MLA_Q_RANK, hw)], axis=1)
    wuk = W["mla_wuk"][l]
    wuk_pad = jnp.concatenate([wuk, jnp.zeros((MLA_KV_RANK, MLA_HEADS, HEAD_SLOT - MLA_NOPE), F32)], axis=-1).reshape(MLA_KV_RANK, hw)
    wukt = wuk.transpose(1, 2, 0).reshape(MLA_HEADS * MLA_NOPE, MLA_KV_RANK)
    wuv = W["mla_wuv"][l].reshape(MLA_KV_RANK, MLA_HEADS * MLA_VDIM)
    proj = dict(g=r2(W["norm_mix"][l]), win=win_t, layer=l,
                qn=r2(W["mla_q_norm"][l]), wq=wq.astype(BF16), kvn=r2(W["mla_kv_norm"][l]), rope=_rope_rows(),
                wuk=wuk_pad.astype(BF16), wuv=wuv.astype(BF16), wukt=wukt.astype(BF16))
    zr = lambda n: jnp.zeros((n, RW_WIDTH), F32)
    rwk = dict(mu=r2(W["rw_mu"][l]), w0=r2(W["rw_w0"][l]),
               w2=jnp.concatenate([W["rw_w2"][l], zr(96)], axis=0).astype(BF16),
               a0=r2(W["rw_a0"][l]),
               a2=jnp.concatenate([zr(32), W["rw_a2"][l], zr(64)], axis=0).astype(BF16),
               g2=jnp.concatenate([zr(64), W["rw_g2"][l]], axis=0).astype(BF16),
               kk=r2(W["rw_kk"][l]), ka=r2(W["rw_ka"][l]), rk=r2(W["rw_rk"][l]),
               lnw=r2(W["rw_ln_w"][l]), lnb=r2(W["rw_ln_b"][l]))
    return dict(
        proj=proj, rwkv=rwk,
        ffn1=(r2(W["norm_ffn1"][l]), W["ffn1_wi"], W["ffn1_wo"], l),
        ffn2=(r2(W["norm_ffn2"][l]), W["ffn2_wi"], W["ffn2_wo"], l),
        wout=W["w_out"], hgn=r2(jnp.tile(W["hg_norm"][l], HG_HEADS)))


def _trunk(x, B, T, pos0, LW, W, sample, st=None):
    n = B * T
    x = x.reshape(n, D_MODEL)
    hg_out, rw_out, sh_out, c_out, kpe_out = [], [], [], [], []
    for l in range(DEPTH):
        lw = LW[l]
        ffn = _ffn_stream if sample else _ffn
        x = ffn(x, *lw["ffn1"])
        if sample:
            p_hg, p_rw, q, qlat, c, kpe = _proj(x, lw["proj"], T, pos0, True)
            o_hg, hg_s = _hgrn(p_hg, W["hg_lb_logits"], lw["hgn"], st["hg"][l], l, B, T, False)
            shift_rows = jnp.repeat(st["sh"][l], T, axis=0)
            o_rw, rw_s = _rwkv(p_rw, shift_rows, st["rw"][l].reshape(B, MIX_W, HEAD_D), lw["rwkv"], B, T, False)
            o_mla = _attn_sample(qlat, q, c, kpe, lw["proj"]["wuv"], st["ckv"], st["kpe"], st["pt"], l, B, T)
        else:
            p_hg, p_rw, q, k, v, c, kpe = _proj(x, lw["proj"], T, pos0, False)
            o_hg, hg_s, o_rw, rw_s = _mixers_prompt(p_hg, p_rw, W["hg_lb_logits"], lw["hgn"], lw["rwkv"], l, B, T)
            o_mla = _attn_prompt(q, k, v, B, T)
        gf = W["norm_final"].reshape(1, -1) if l == DEPTH - 1 else None
        x = ffn(x, *lw["ffn2"], mix=(o_hg, o_rw, o_mla), wout=lw["wout"], gf=gf)
        hg_out.append(hg_s)
        rw_out.append(rw_s.reshape(B, RW_HEADS, RW_DH, RW_DH))
        sh_out.append(p_rw.reshape(B, T, RW_COLS)[:, -1])
        c_out.append(c.reshape(B, T, MLA_KV_RANK))
        kpe_out.append(kpe.reshape(B, T, MLA_ROPE))
    st_ = jnp.stack
    return x.reshape(B, T, D_MODEL), st_(hg_out), st_(rw_out), st_(sh_out), st_(c_out), st_(kpe_out)


def kernel(x_prompt, x_sample, cache_mla_ckv, cache_mla_kpe, state_hgrn, state_rwkv, state_rwkv_shift, page_table, norm_ffn1, ffn1_wi, ffn1_wo, norm_mix, w_in, w_out, hg_lb_logits, hg_norm, rw_mu, rw_w0, rw_w2, rw_a0, rw_a2, rw_g2, rw_kk, rw_ka, rw_rk, rw_ln_w, rw_ln_b, mla_q_norm, mla_wqb, mla_kv_norm, mla_wuk, mla_wuv, norm_ffn2, ffn2_wi, ffn2_wo, norm_final):
    W = dict(norm_ffn1=norm_ffn1, ffn1_wi=ffn1_wi, ffn1_wo=ffn1_wo, norm_mix=norm_mix, w_in=w_in, w_out=w_out,
             hg_lb_logits=hg_lb_logits, hg_norm=hg_norm, rw_mu=rw_mu, rw_w0=rw_w0, rw_w2=rw_w2, rw_a0=rw_a0,
             rw_a2=rw_a2, rw_g2=rw_g2, rw_kk=rw_kk, rw_ka=rw_ka, rw_rk=rw_rk, rw_ln_w=rw_ln_w, rw_ln_b=rw_ln_b,
             mla_q_norm=mla_q_norm, mla_wqb=mla_wqb, mla_kv_norm=mla_kv_norm, mla_wuk=mla_wuk, mla_wuv=mla_wuv,
             norm_ffn2=norm_ffn2, ffn2_wi=ffn2_wi, ffn2_wo=ffn2_wo, norm_final=norm_final)
    LW = [_layer_weights(l, W) for l in range(DEPTH)]
    Bp, Tp, _ = x_prompt.shape
    y_p, hg_p, rw_p, sh_p, ckv_p, kpe_p = _trunk(x_prompt, Bp, Tp, 0, LW, W, False)
    Bs, Ts, _ = x_sample.shape
    past_len = page_table.shape[1] * cache_mla_ckv.shape[2]
    st = dict(hg=state_hgrn, rw=state_rwkv, sh=state_rwkv_shift, ckv=cache_mla_ckv,
              kpe=jnp.swapaxes(cache_mla_kpe, 2, 3), pt=page_table)
    y_s, hg_s, rw_s, sh_s, ckv_s, kpe_s = _trunk(x_sample, Bs, Ts, past_len, LW, W, True, st)
    return (y_p, y_s, hg_p, hg_s, rw_p, rw_s, sh_p, sh_s, ckv_p, ckv_s, kpe_p, kpe_s)
```

```python
import functools
import math

import jax
import jax.numpy as jnp
import numpy as np
from jax import lax
from jax.experimental import pallas as pl
from jax.experimental.pallas import tpu as pltpu

F32 = jnp.float32
BF16 = jnp.bfloat16

D_MODEL = 1024
DEPTH = 2
PAGE_SIZE = 128
EPS = 1e-6
D_FF = 2816
HG_HEADS = 4
HG_DK = 64
HG_KW = 256
HG_MIN_F = 1e-30
RW_HEADS = 4
RW_DH = 64
RW_WIDTH = 256
RW_COLS = 896
RW_GN_EPS = 64e-5
MLA_HEADS = 8
MLA_NOPE = 64
MLA_ROPE = 32
MLA_VDIM = 64
MLA_Q_RANK = 384
MLA_KV_RANK = 256
MLA_SCALE = (MLA_NOPE + MLA_ROPE) ** -0.5
ROPE_THETA = 10000.0

LANES = 128
HEAD_SLOT = 128
ROPE_LANE0 = 64
VMEM_LIMIT = 56 * 1024 * 1024
NEG_INF = -1e30

ROW_TILE = 512
MIX_BLOCK = 256
HG_CHUNK = 16
RW_CHUNK = 64
ATTN_TILE = 1024
HG_COLS = 2 * HG_KW + 2 * HG_HEADS * HG_DK
MLA_COL0 = HG_COLS + RW_COLS


def _dot(a, b):
    return jnp.dot(a.astype(BF16), b.astype(BF16), preferred_element_type=F32)


def _dot_nt(a, b):
    return lax.dot_general(a.astype(BF16), b.astype(BF16), (((1,), (1,)), ((), ())), preferred_element_type=F32)


def _dot_xl(m, b, pieces=2):
    out = None
    rem = b
    for _ in range(pieces):
        part = rem.astype(BF16)
        term = jnp.dot(m, part, preferred_element_type=F32)
        out = term if out is None else out + term
        rem = rem - part.astype(F32)
    return out


def _dot_xr(a, m, pieces=2):
    out = None
    rem = a
    for _ in range(pieces):
        part = rem.astype(BF16)
        term = jnp.dot(part, m, preferred_element_type=F32)
        out = term if out is None else out + term
        rem = rem - part.astype(F32)
    return out


def _rms(x, g, eps=EPS):
    return x * lax.rsqrt(jnp.mean(x * x, axis=-1, keepdims=True) + eps) * g


def _sigmoid(x):
    return 1.0 / (1.0 + jnp.exp(-x))


def _iota(shape, dim):
    return lax.broadcasted_iota(jnp.int32, shape, dim)


def _head_block_ones(n, hd):
    return (_iota((n, n), 0) // hd == _iota((n, n), 1) // hd).astype(BF16)


def _full(shape):
    return pl.BlockSpec(shape, lambda *_: (0,) * len(shape), pipeline_mode=pl.Buffered(1))


def _layer_block(shape, layer):
    return pl.BlockSpec((None,) + tuple(shape[1:]), lambda *_: (layer,) + (0,) * (len(shape) - 1),
                        pipeline_mode=pl.Buffered(1))


def _params(sem):
    return pltpu.CompilerParams(dimension_semantics=sem, vmem_limit_bytes=VMEM_LIMIT)


FF_CHUNK = 256


def _ffn_kernel(*refs, has_mix, final):
    it = iter(refs)
    x_ref = next(it)
    if has_mix:
        ohg_ref, orw_ref, omla_ref, wout_ref = next(it), next(it), next(it), next(it)
    g_ref, wi_ref, wo_ref = next(it), next(it), next(it)
    gf_ref = next(it) if final else None
    o_ref = next(it)

    x = x_ref[...]
    if has_mix:
        o = jnp.concatenate([r[...].astype(BF16) for r in (ohg_ref, orw_ref, omla_ref)], axis=-1)
        x = x + jnp.dot(o, wout_ref[...].astype(BF16), preferred_element_type=F32)
    xn = _rms(x, g_ref[...]).astype(BF16)
    acc = jnp.zeros(x.shape, F32)
    for c in range(D_FF // FF_CHUNK):
        lo = c * FF_CHUNK
        a = jnp.dot(xn, wi_ref[:, lo:lo + FF_CHUNK].astype(BF16), preferred_element_type=F32)
        b = jnp.dot(xn, wi_ref[:, D_FF + lo:D_FF + lo + FF_CHUNK].astype(BF16), preferred_element_type=F32)
        h = (a * _sigmoid(a) * b).astype(BF16)
        acc = acc + jnp.dot(h, wo_ref[lo:lo + FF_CHUNK, :].astype(BF16), preferred_element_type=F32)
    y = x + 0.5 * acc
    if final:
        y = _rms(y, gf_ref[...])
    o_ref[...] = y


def _ffn(x, g, wi, wo, layer, mix=None, wout=None, gf=None, tm=ROW_TILE):
    n = x.shape[0]
    tm = min(tm, n)
    row = lambda w: pl.BlockSpec((tm, w), lambda i: (i, 0))
    args, specs = [x], [row(D_MODEL)]
    if mix is not None:
        for m in mix:
            args.append(m)
            specs.append(row(m.shape[1]))
        args.append(wout)
        specs.append(_layer_block(wout.shape, layer))
    args += [g, wi, wo]
    specs += [_full(g.shape), _layer_block(wi.shape, layer), _layer_block(wo.shape, layer)]
    if gf is not None:
        args.append(gf)
        specs.append(_full(gf.shape))
    return pl.pallas_call(
        functools.partial(_ffn_kernel, has_mix=mix is not None, final=gf is not None),
        grid=(n // tm,), in_specs=specs, out_specs=row(D_MODEL),
        out_shape=jax.ShapeDtypeStruct((n, D_MODEL), F32),
        compiler_params=_params(("parallel",)), name="ffn")(*args)


def _ffn_stream_kernel(*refs, has_mix, final):
    it = iter(refs)
    x_ref = next(it)
    if has_mix:
        ohg_ref, orw_ref, omla_ref, wout_ref = next(it), next(it), next(it), next(it)
    g_ref, wia_ref, wib_ref, wo_ref = next(it), next(it), next(it), next(it)
    gf_ref = next(it) if final else None
    o_ref = next(it)
    x_scr, xn_scr, acc_scr = next(it), next(it), next(it)
    c = pl.program_id(0)

    @pl.when(c == 0)
    def _():
        x = x_ref[...]
        if has_mix:
            o = jnp.concatenate([r[...].astype(BF16) for r in (ohg_ref, orw_ref, omla_ref)], axis=-1)
            x = x + jnp.dot(o, wout_ref[...].astype(BF16), preferred_element_type=F32)
        x_scr[...] = x
        xn_scr[...] = _rms(x, g_ref[...]).astype(BF16)
        acc_scr[...] = jnp.zeros(acc_scr.shape, F32)

    xn = xn_scr[...]
    a = jnp.dot(xn, wia_ref[...].astype(BF16), preferred_element_type=F32)
    b = jnp.dot(xn, wib_ref[...].astype(BF16), preferred_element_type=F32)
    h = (a * _sigmoid(a) * b).astype(BF16)
    acc_scr[...] += jnp.dot(h, wo_ref[...].astype(BF16), preferred_element_type=F32)

    @pl.when(c == pl.num_programs(0) - 1)
    def _():
        y = x_scr[...] + 0.5 * acc_scr[...]
        if final:
            y = _rms(y, gf_ref[...])
        o_ref[...] = y


def _ffn_stream(x, g, wi, wo, layer, mix=None, wout=None, gf=None):
    n = x.shape[0]
    n_chunks = D_FF // FF_CHUNK
    const = lambda a: pl.BlockSpec(a.shape, lambda c: (0,) * a.ndim)
    args, specs = [x], [const(x)]
    if mix is not None:
        args += list(mix) + [wout]
        specs += [const(m) for m in mix] + [_layer_block(wout.shape, layer)]
    args += [g, wi, wi, wo]
    specs += [const(g),
              pl.BlockSpec((None, D_MODEL, FF_CHUNK), lambda c: (layer, 0, c)),
              pl.BlockSpec((None, D_MODEL, FF_CHUNK), lambda c: (layer, 0, n_chunks + c)),
              pl.BlockSpec((None, FF_CHUNK, D_MODEL), lambda c: (layer, c, 0))]
    if gf is not None:
        args.append(gf)
        specs.append(const(gf))
    return pl.pallas_call(
        functools.partial(_ffn_stream_kernel, has_mix=mix is not None, final=gf is not None),
        grid=(n_chunks,), in_specs=specs, out_specs=pl.BlockSpec((n, D_MODEL), lambda c: (0, 0)),
        out_shape=jax.ShapeDtypeStruct((n, D_MODEL), F32),
        scratch_shapes=[pltpu.VMEM((n, D_MODEL), F32), pltpu.VMEM((n, D_MODEL), BF16), pltpu.VMEM((n, D_MODEL), F32)],
        compiler_params=_params(("arbitrary",)), name="ffn_stream")(*args)


def _proj_kernel(*refs, seq_len, pos0, sample, layer):
    it = iter(refs)
    x_ref, g_ref, win_ref = (next(it) for _ in range(3))
    qn_ref, wq_ref, kvn_ref, rope_ref = (next(it) for _ in range(4))
    cprev_ref, kprev_ref = (next(it), next(it)) if layer else (None, None)
    if sample:
        wukt_ref = next(it)
    else:
        wuk_ref, wuv_ref = next(it), next(it)
    phg_ref, prw_ref, q_ref = next(it), next(it), next(it)
    if sample:
        qlat_ref = next(it)
    else:
        k_ref, v_ref = next(it), next(it)
    c_ref, kpe_ref = next(it), next(it)

    tm = x_ref.shape[0]
    xn = _rms(x_ref[...], g_ref[...]).astype(BF16)
    mla_lat = MLA_Q_RANK + MLA_KV_RANK
    phg_ref[...] = _dot_nt(xn, win_ref[0:HG_COLS, :])
    prw_ref[...] = _dot_nt(xn, win_ref[HG_COLS:MLA_COL0, :])
    pm = _dot_nt(xn, win_ref[MLA_COL0:MLA_COL0 + mla_lat, :])
    wk = win_ref[MLA_COL0 + mla_lat:MLA_COL0 + mla_lat + MLA_ROPE, :]
    zrow = lambda n_: jnp.zeros((n_, D_MODEL), F32)
    tail = HEAD_SLOT - ROPE_LANE0 - MLA_ROPE
    wk2 = jnp.concatenate([zrow(ROPE_LANE0), wk, zrow(tail),
                           zrow(ROPE_LANE0), wk[MLA_ROPE // 2:], wk[:MLA_ROPE // 2], zrow(tail)], axis=0)
    pk = _dot_nt(xn, wk2)

    row = pl.program_id(0) * tm + _iota((tm, 1), 0)
    pos = (pos0 + (row & (seq_len - 1))).astype(F32)
    rope = rope_ref[...]
    ang = pos * rope[0:1, :]
    cos_s = rope[1:2, :] + rope[3:4, :] * jnp.cos(ang)
    sin_s = rope[2:3, :] * jnp.sin(ang)

    c = _rms(pm[:, MLA_Q_RANK:MLA_Q_RANK + MLA_KV_RANK], kvn_ref[...])
    if layer:
        c_ref[0:layer] = cprev_ref[...]
        kpe_ref[0:layer] = kprev_ref[...]
    c_ref[layer] = c
    k_slot = pk[:, :HEAD_SLOT] * cos_s + pk[:, HEAD_SLOT:] * sin_s
    kpe_ref[layer] = k_slot[:, ROPE_LANE0:ROPE_LANE0 + MLA_ROPE]

    qn = _rms(pm[:, :MLA_Q_RANK], qn_ref[...]).astype(BF16)
    q2 = jnp.dot(qn, wq_ref[...], preferred_element_type=F32)
    hw = MLA_HEADS * HEAD_SLOT
    q_scale = MLA_SCALE if sample else MLA_SCALE * math.log2(math.e)
    cos_q = jnp.concatenate([cos_s * q_scale] * MLA_HEADS, axis=-1)
    sin_q = jnp.concatenate([sin_s * q_scale] * MLA_HEADS, axis=-1)
    q = q2[:, :hw] * cos_q + q2[:, hw:] * sin_q
    if sample:
        q_ref[...] = q
        for h in range(MLA_HEADS):
            qh = q[:, h * HEAD_SLOT:h * HEAD_SLOT + MLA_NOPE]
            qlat_ref[:, h * MLA_KV_RANK:(h + 1) * MLA_KV_RANK] = _dot(qh, wukt_ref[h * MLA_NOPE:(h + 1) * MLA_NOPE, :])
    else:
        q_ref[...] = q.astype(BF16)
        cb = c.astype(BF16)
        kn = jnp.dot(cb, wuk_ref[...], preferred_element_type=F32)
        k_ref[...] = (kn + jnp.concatenate([k_slot] * MLA_HEADS, axis=-1)).astype(BF16)
        v_ref[...] = jnp.dot(cb, wuv_ref[...], preferred_element_type=F32).astype(BF16)


def _proj(x, pw, seq_len, pos0, sample, prev=None, tm=ROW_TILE):
    n = x.shape[0]
    tm = min(tm, n)
    row = lambda w: pl.BlockSpec((tm, w), lambda i: (i, 0))
    stack = lambda depth, w: pl.BlockSpec((depth, tm, w), lambda i: (0, i, 0))
    hw = MLA_HEADS * HEAD_SLOT
    layer = pw["layer"]
    args = [x, pw["g"], pw["win"], pw["qn"], pw["wq"], pw["kvn"], pw["rope"]]
    specs = [row(D_MODEL)] + [_full(a.shape) for a in args[1:]]
    specs[2] = _layer_block(pw["win"].shape, layer)
    if layer:
        args += list(prev)
        specs += [stack(layer, MLA_KV_RANK), stack(layer, MLA_ROPE)]
    tail = [pw["wukt"]] if sample else [pw["wuk"], pw["wuv"]]
    args += tail
    specs += [_full(a.shape) for a in tail]
    qdt = F32 if sample else BF16
    outs = [((n, HG_COLS), F32), ((n, RW_COLS), F32), ((n, hw), qdt)]
    outs += [((n, MLA_HEADS * MLA_KV_RANK), F32)] if sample else [((n, hw), BF16), ((n, MLA_HEADS * MLA_VDIM), BF16)]
    out_specs = [row(s[1]) for s, _ in outs] + [stack(layer + 1, MLA_KV_RANK), stack(layer + 1, MLA_ROPE)]
    outs += [((layer + 1, n, MLA_KV_RANK), F32), ((layer + 1, n, MLA_ROPE), F32)]
    return pl.pallas_call(
        functools.partial(_proj_kernel, seq_len=seq_len, pos0=pos0, sample=sample, layer=layer),
        grid=(n // tm,), in_specs=specs, out_specs=out_specs,
        out_shape=[jax.ShapeDtypeStruct(s, d) for s, d in outs],
        compiler_params=_params(("parallel",)), name="proj")(*args)


MIX_W = 256
HEAD_D = 64


def _chunk_masks(L, C):
    r, c = _iota((L, L), 0), _iota((L, L), 1)
    same = (r // C) == (c // C)
    return same, same & (c <= r), same & (c < r)


def _chunk_last(a, nck, C):
    a3 = a.reshape(nck, C, a.shape[-1])
    return jnp.broadcast_to(a3[:, C - 1:C, :], a3.shape).reshape(a.shape)


def _expand_state(s_kv):
    t = jnp.concatenate([s_kv] * (MIX_W // HEAD_D), axis=0)
    bd = (_iota((MIX_W, MIX_W), 0) // HEAD_D) == (_iota((MIX_W, MIX_W), 1) // HEAD_D)
    return jnp.where(bd, t, 0.0)


def _fold_state(hbd):
    out = hbd[0:HEAD_D]
    for h in range(1, MIX_W // HEAD_D):
        out = out + hbd[h * HEAD_D:(h + 1) * HEAD_D]
    return out


def _hgrn_steps(*refs, L, C, layer, chained):
    it = iter(refs)
    p_ref, lbl_ref, gn_ref = next(it), next(it), next(it)
    s0_ref = None if chained else next(it)
    o_ref, sout_ref = next(it), next(it)
    h_scr = next(it) if chained else None

    nck = L // C
    p = p_ref[...]
    q, fz, vi, gate = (p[:, i * MIX_W:(i + 1) * MIX_W] for i in range(4))

    lg = lbl_ref[...]
    e = jnp.exp(lg - jnp.max(lg, axis=0, keepdims=True))
    pr = e / jnp.sum(e, axis=0, keepdims=True)
    cs = pr[0:1]
    for i in range(1, layer + 1):
        cs = cs + pr[i:i + 1]
    lb = jnp.clip(cs - pr[0:1], 0.0, 1.0)

    f = lb + (1.0 - lb) * _sigmoid(fz)
    log_f = jnp.log(jnp.maximum(f, HG_MIN_F))
    k = (1.0 - lb) * _sigmoid(-fz)

    same, incl, _ = _chunk_masks(L, C)
    b = _dot_xl(incl.astype(BF16), log_f, pieces=3)
    b_last = _chunk_last(b, nck, C)
    ones_h = _head_block_ones(MIX_W, HEAD_D)

    b3, q3, k3, v3 = (a.reshape(nck, C, MIX_W) for a in (b, q, k, vi))
    SUB = 8
    tiles = [jnp.zeros((nck, min(SUB, C), MIX_W), F32) for _ in range(max(C // SUB, 1))]
    for s in range(C):
        t0 = (s // SUB) * SUB
        bt, qt = b3[:, t0:, :], q3[:, t0:, :]
        rows = C - t0
        w = jnp.exp(bt - b3[:, s:s + 1, :])
        x = jnp.where(t0 + _iota((nck, rows, MIX_W), 1) >= s, qt * w * k3[:, s:s + 1, :], 0.0)
        a_s = _dot(x.reshape(nck * rows, MIX_W), ones_h).reshape(nck, rows, MIX_W)
        upd = a_s * v3[:, s:s + 1, :]
        for ti in range(t0 // SUB, len(tiles)):
            lo = ti * SUB - t0
            tiles[ti] = tiles[ti] + upd[:, lo:lo + SUB, :]
        yield
    o_intra = jnp.concatenate(tiles, axis=1).reshape(L, MIX_W)

    qe = q * jnp.exp(b)
    kd_t = (k * jnp.exp(b_last - b)).T
    e_t = jnp.exp(b_last).T
    bd = (_iota((MIX_W, MIX_W), 0) // HEAD_D) == (_iota((MIX_W, MIX_W), 1) // HEAD_D)
    col = _iota((1, L), 1)

    if chained:
        hbd = h_scr[...]
    gks = []
    for ck in range(nck):
        gks.append(jnp.where(bd, _dot(jnp.where((col // C) == ck, kd_t, 0.0), vi), 0.0))
        yield
    dcols = [jnp.sum(jnp.where(col == ck * C, e_t, 0.0), axis=1, keepdims=True) for ck in range(nck)]
    o_parts = []
    for ck in range(nck):
        if not chained:
            s_kv = jnp.concatenate([s0_ref[ck, h] for h in range(HG_HEADS)], axis=-1)
            hbd = _expand_state(s_kv)
        o_parts.append(_dot(qe[ck * C:(ck + 1) * C], hbd))
        hbd = dcols[ck] * hbd + gks[ck]
        if not chained:
            s_new = _fold_state(hbd)
            for h in range(HG_HEADS):
                sout_ref[ck, h] = s_new[:, h * HEAD_D:(h + 1) * HEAD_D]
    if chained:
        h_scr[...] = hbd
        s_new = _fold_state(hbd)
        for h in range(HG_HEADS):
            sout_ref[h] = s_new[:, h * HEAD_D:(h + 1) * HEAD_D]

    o = o_intra + jnp.concatenate(o_parts, axis=0)
    ms = _dot_xr(o * o, ones_h) * (1.0 / HEAD_D)
    on = o * lax.rsqrt(ms + EPS) * gn_ref[...]
    o_ref[...] = (on * (gate * _sigmoid(gate))).astype(o_ref.dtype)


def _drain(*gens):
    alive = list(gens)
    while alive:
        for g in list(alive):
            try:
                next(g)
            except StopIteration:
                alive.remove(g)


def _hgrn_kernel(*refs, **kw):
    _drain(_hgrn_steps(*refs, **kw))


def _hgrn_sample(p_hg, lb_logits, gn, s0, layer, B, T):
    n = p_hg.shape[0]
    return pl.pallas_call(
        functools.partial(_hgrn_kernel, L=n, C=T, layer=layer, chained=False),
        grid=(1,),
        in_specs=[_full((n, HG_COLS)), _full(lb_logits.shape), _full(gn.shape), _layer_block(s0.shape, layer)],
        out_specs=[_full((n, MIX_W)), _full((B, HG_HEADS, HEAD_D, HEAD_D))],
        out_shape=[jax.ShapeDtypeStruct((n, MIX_W), BF16),
                   jax.ShapeDtypeStruct((B, HG_HEADS, HEAD_D, HEAD_D), F32)],
        compiler_params=_params(("arbitrary",)), name="hgrn")(p_hg, lb_logits, gn, s0)


def _rwkv_steps(*refs, L, C, chained):
    it = iter(refs)
    p_ref = next(it)
    if not chained:
        shrow_ref, s0_ref = next(it), next(it)
    (mu_ref, w0_ref, w2_ref, a0_ref, a2_ref, g2_ref, kkw_ref, ka_ref, rk_ref, lnw_ref, lnb_ref) = (next(it) for _ in range(11))
    y_ref, sout_ref = next(it), next(it)
    if chained:
        h_scr, carry_scr = next(it), next(it)

    nck = L // C
    p = p_ref[...]
    rowi = _iota((L, 1), 0)
    rolled = pltpu.roll(p, 1, 0)
    if chained:
        prev = jnp.where(rowi == 0, carry_scr[0:1, :], rolled)
        carry_scr[0:1, :] = p[L - 1:L, :]
    else:
        prev = jnp.where((rowi & (C - 1)) == 0, shrow_ref[...], rolled)
    ps = p + (prev - p) * mu_ref[...]
    r, k, v, tail = ps[:, 0:MIX_W], ps[:, MIX_W:2 * MIX_W], ps[:, 2 * MIX_W:3 * MIX_W], ps[:, 3 * MIX_W:RW_COLS]

    w_raw = -jax.nn.softplus(-(w0_ref[...] + _dot(jnp.tanh(tail), w2_ref[...]))) - 0.5
    lw = -jnp.exp(w_raw)
    a = _sigmoid(a0_ref[...] + _dot(tail, a2_ref[...]))
    g = _dot(_sigmoid(tail), g2_ref[...])

    ones_h = _head_block_ones(MIX_W, HEAD_D)
    kk = k * kkw_ref[...]
    kk = kk / jnp.maximum(jnp.sqrt(_dot_xr(kk * kk, ones_h)), 1e-12)
    k2 = k * (1.0 + (a - 1.0) * ka_ref[...])
    bb = kk * a

    same, incl, strict = _chunk_masks(L, C)
    gc = _dot_xl(incl.astype(BF16), lw)
    g_last = _chunk_last(gc, nck, C)
    eg, eng = jnp.exp(gc), jnp.exp(-gc)
    rg, kg, bg, kkg = r * eg, k2 * eng, bb * eng, kk * jnp.exp(gc - lw)
    to_end = jnp.exp(g_last - gc)
    kd_t, bd_t, e_t = (k2 * to_end).T, (bb * to_end).T, jnp.exp(g_last).T

    lane_h = _iota((1, MIX_W), 1) // HEAD_D
    eye = (_iota((L, L), 0) == _iota((L, L), 1)).astype(F32)
    u0 = jnp.zeros((L, MIX_W), F32)
    kkt = jnp.zeros((L, MIX_W), F32)
    qp = rg
    o0 = jnp.zeros((L, MIX_W), F32)
    heads = range(RW_HEADS)
    mhs = [lane_h == h for h in heads]
    kkg_hs = [jnp.where(mh, kkg, 0.0) for mh in mhs]
    rg_hs = [jnp.where(mh, rg, 0.0) for mh in mhs]
    a_kb, a_kk, a_rk, a_rb = [], [], [], []
    for h in heads:
        a_kb.append(jnp.where(strict, _dot_nt(kkg_hs[h], bg), 0.0))
        a_kk.append(jnp.where(strict, _dot_nt(kkg_hs[h], kg), 0.0))
        yield
    for h in heads:
        a_rk.append(jnp.where(incl, _dot_nt(rg_hs[h], kg), 0.0))
        a_rb.append(jnp.where(incl, _dot_nt(rg_hs[h], bg), 0.0))
        yield
    npow = [-a for a in a_kb]
    tinv = [eye + n_ for n_ in npow]
    for _ in range(int(math.log2(C)) - 1):
        npow = [_dot(n_, n_) for n_ in npow]
        yield
        tinv = [t_ + _dot(t_, n_) for t_, n_ in zip(tinv, npow)]
        yield
    x1 = [_dot(a_kk[h], v) for h in heads]
    yield
    u0_hs = [_dot(tinv[h], x1[h]) for h in heads]
    yield
    kkt_hs = [_dot(tinv[h], kkg) for h in heads]
    yield
    for h in heads:
        u0 = jnp.where(mhs[h], u0_hs[h], u0)
        kkt = jnp.where(mhs[h], kkt_hs[h], kkt)
    for h in heads:
        qp = qp - jnp.where(mhs[h], _dot(a_rb[h], kkt_hs[h]), 0.0)
        o0 = o0 + jnp.where(mhs[h], _dot(a_rk[h], v) - _dot(a_rb[h], u0_hs[h]), 0.0)
        yield

    bd = (_iota((MIX_W, MIX_W), 0) // HEAD_D) == (_iota((MIX_W, MIX_W), 1) // HEAD_D)
    col = _iota((1, L), 1)
    if chained:
        hbd = h_scr[...]
    bd_cs = [jnp.where((col // C) == ck, bd_t, 0.0) for ck in range(nck)]
    mbs, gbs = [], []
    for ck in range(nck):
        mbs.append(jnp.where(bd, _dot(bd_cs[ck], kkt), 0.0))
        gbs.append(jnp.where(bd, _dot(jnp.where((col // C) == ck, kd_t, 0.0), v) - _dot(bd_cs[ck], u0), 0.0))
        yield
    dcols = [jnp.sum(jnp.where(col == ck * C, e_t, 0.0), axis=1, keepdims=True) for ck in range(nck)]
    y_parts = []
    for ck in range(nck):
        if not chained:
            hbd = _expand_state(s0_ref[ck].T)
        y_parts.append(_dot(qp[ck * C:(ck + 1) * C], hbd))
        hbd = dcols[ck] * hbd - _dot(mbs[ck], hbd) + gbs[ck]
        yield
        if not chained:
            sout_ref[ck] = _fold_state(hbd).T
    if chained:
        h_scr[...] = hbd
        sout_ref[...] = _fold_state(hbd).T

    y = o0 + jnp.concatenate(y_parts, axis=0)
    mean = _dot_xr(y, ones_h) * (1.0 / HEAD_D)
    yc = y - mean
    var = _dot_xr(yc * yc, ones_h) * (1.0 / HEAD_D)
    yn = yc * lax.rsqrt(var + RW_GN_EPS) * lnw_ref[...] + lnb_ref[...]
    bonus = _dot_xr(r * k2 * rk_ref[...], ones_h) * v
    y_ref[...] = ((yn + bonus) * g).astype(y_ref.dtype)


def _rwkv_kernel(*refs, **kw):
    _drain(_rwkv_steps(*refs, **kw))


RWKV_PARAMS = ("mu", "w0", "w2", "a0", "a2", "g2", "kk", "ka", "rk", "lnw", "lnb")


def _rwkv_sample(p_rw, shift_rows, s0, rw, layer, B, T):
    n = p_rw.shape[0]
    wargs = [rw[k] for k in RWKV_PARAMS]
    return pl.pallas_call(
        functools.partial(_rwkv_kernel, L=n, C=T, chained=False),
        grid=(1,),
        in_specs=[_full((n, RW_COLS)), _full(shift_rows.shape), _layer_block(s0.shape, layer)] + [_full(a.shape) for a in wargs],
        out_specs=[_full((n, MIX_W)), _full((B, MIX_W, HEAD_D))],
        out_shape=[jax.ShapeDtypeStruct((n, MIX_W), BF16), jax.ShapeDtypeStruct((B, MIX_W, HEAD_D), F32)],
        compiler_params=_params(("arbitrary",)), name="rwkv")(p_rw, shift_rows, s0, *wargs)


def _mixers_kernel(*refs, L, hg_c, rw_c, layer):
    hg_in, rw_in = refs[0:3], refs[3:15]
    hg_out, rw_out = refs[15:17], refs[17:19]
    hg_scr, rw_scr = refs[19:20], refs[20:22]

    @pl.when(pl.program_id(1) == 0)
    def _():
        for scr in (*hg_scr, *rw_scr):
            scr[...] = jnp.zeros(scr.shape, F32)
    _drain(_rwkv_steps(*rw_in, *rw_out, *rw_scr, L=L, C=rw_c, chained=True),
           _hgrn_steps(*hg_in, *hg_out, *hg_scr, L=L, C=hg_c, layer=layer, chained=True))


def _mixers_prompt(p_hg, p_rw, lb_logits, gn, rw, layer, B, T):
    n = p_hg.shape[0]
    wargs = [rw[k] for k in RWKV_PARAMS]
    L = min(MIX_BLOCK, T)
    nblk = T // L
    rowmap = lambda b, j: (b * nblk + j, 0)
    in_specs = [pl.BlockSpec((L, HG_COLS), rowmap), _full(lb_logits.shape), _full(gn.shape),
                pl.BlockSpec((L, RW_COLS), rowmap)] + [_full(a.shape) for a in wargs]
    out_specs = [pl.BlockSpec((L, MIX_W), rowmap),
                 pl.BlockSpec((None, HG_HEADS, HEAD_D, HEAD_D), lambda b, j: (b, 0, 0, 0)),
                 pl.BlockSpec((L, MIX_W), rowmap),
                 pl.BlockSpec((None, MIX_W, HEAD_D), lambda b, j: (b, 0, 0))]
    return pl.pallas_call(
        functools.partial(_mixers_kernel, L=L, hg_c=min(HG_CHUNK, T), rw_c=min(RW_CHUNK, T), layer=layer),
        grid=(B, nblk), in_specs=in_specs, out_specs=out_specs,
        out_shape=[jax.ShapeDtypeStruct((n, MIX_W), BF16),
                   jax.ShapeDtypeStruct((B, HG_HEADS, HEAD_D, HEAD_D), F32),
                   jax.ShapeDtypeStruct((n, MIX_W), BF16),
                   jax.ShapeDtypeStruct((B, MIX_W, HEAD_D), F32)],
        scratch_shapes=[pltpu.VMEM((MIX_W, MIX_W), F32), pltpu.VMEM((MIX_W, MIX_W), F32), pltpu.VMEM((8, RW_COLS), F32)],
        compiler_params=_params(("parallel", "arbitrary")), name="mixers")(p_hg, lb_logits, gn, p_rw, *wargs)


def _attn_prompt_kernel(q_ref, k_ref, v_ref, o_ref, *, tq):
    i = pl.program_id(2)
    qs = [q_ref[:, hh * HEAD_SLOT:(hh + 1) * HEAD_SLOT] for hh in range(2)]

    def step(j, carry, diagonal):
        start = pl.multiple_of(j * tq, tq)
        vb = v_ref[pl.ds(start, tq), :]
        new = []
        for hh in range(2):
            m, l, acc = carry[hh]
            kb = k_ref[pl.ds(start, tq), hh * HEAD_SLOT:(hh + 1) * HEAD_SLOT]
            s = lax.dot_general(qs[hh], kb, (((1,), (1,)), ((), ())), preferred_element_type=F32)
            if diagonal:
                s = jnp.where(_iota((tq, tq), 1) <= _iota((tq, tq), 0), s, NEG_INF)
            m_new = jnp.maximum(m, jnp.max(s, axis=-1, keepdims=True))
            pexp = jnp.exp2(s - m_new)
            alpha = jnp.exp2(m - m_new)
            l = alpha * l + jnp.sum(pexp, axis=-1, keepdims=True)
            acc = alpha * acc + jnp.dot(pexp.astype(BF16), vb, preferred_element_type=F32)
            new.append((m_new, l, acc))
        return tuple(new)

    one = (jnp.full((tq, 1), NEG_INF, F32), jnp.zeros((tq, 1), F32), jnp.zeros((tq, 2 * MLA_VDIM), F32))
    carry = lax.fori_loop(0, i, lambda j, c: step(j, c, False), (one, one))
    (_, l0, acc0), (_, l1, acc1) = step(i, carry, True)
    lane = _iota((tq, 2 * MLA_VDIM), 1)
    o_ref[...] = jnp.where(lane < MLA_VDIM, acc0 / l0, acc1 / l1).astype(o_ref.dtype)


def _attn_prompt(q, k, v, B, T, tq=ATTN_TILE):
    n = q.shape[0]
    tq = min(tq, T)
    nq = T // tq
    return pl.pallas_call(
        functools.partial(_attn_prompt_kernel, tq=tq),
        grid=(B, MLA_HEADS // 2, nq),
        in_specs=[pl.BlockSpec((tq, 2 * HEAD_SLOT), lambda b, hp, i: (b * nq + i, hp)),
                  pl.BlockSpec((T, 2 * HEAD_SLOT), lambda b, hp, i: (b, hp)),
                  pl.BlockSpec((T, 2 * MLA_VDIM), lambda b, hp, i: (b, hp))],
        out_specs=pl.BlockSpec((tq, 2 * MLA_VDIM), lambda b, hp, i: (b * nq + i, hp)),
        out_shape=jax.ShapeDtypeStruct((n, MLA_HEADS * MLA_VDIM), BF16),
        compiler_params=_params(("parallel", "parallel", "arbitrary")), name="attn_prompt")(q, k, v)


PAGE_GROUP = 16
SEQ_PER_STEP = 4


def _attn_sample_kernel(pt_ref, qlat_ref, q_ref, c_ref, kpe_ref, wuv_ref, ckv_hbm, kpe_hbm, o_ref,
                        cbuf, kbuf, sem, *, T, layer, n_pages):
    G, NS = PAGE_GROUP, SEQ_PER_STEP
    b0 = pl.program_id(0) * NS
    n_groups = n_pages // G
    R = MLA_HEADS * T

    def page_copies(g, slot, first_seq=b0):
        out = []
        for sq in range(NS):
            for i in range(G):
                pg = pt_ref[first_seq + sq, g * G + i]
                out.append(pltpu.make_async_copy(ckv_hbm.at[layer, pg], cbuf.at[slot, sq, i], sem.at[0, slot]))
                out.append(pltpu.make_async_copy(kpe_hbm.at[layer, pg], kbuf.at[slot, sq, i], sem.at[1, slot]))
        return out

    carry_over = n_groups % 2 == 0
    first_step = pl.program_id(0) == 0
    last_step = pl.program_id(0) == pl.num_programs(0) - 1

    def start_first_group():
        for cp in page_copies(0, 0):
            cp.start()

    if carry_over:
        pl.when(first_step)(start_first_group)
    else:
        start_first_group()

    qls, qps = [], []
    for sq in range(NS):
        rows = slice(sq * T, (sq + 1) * T)
        qls.append(jnp.concatenate([qlat_ref[rows, h * MLA_KV_RANK:(h + 1) * MLA_KV_RANK]
                                    for h in range(MLA_HEADS)], axis=0).astype(BF16))
        qs = jnp.concatenate([q_ref[rows, h * HEAD_SLOT:(h + 1) * HEAD_SLOT] for h in range(MLA_HEADS)], axis=0)
        qps.append(qs[:, ROPE_LANE0:ROPE_LANE0 + MLA_ROPE].astype(BF16))

    def update(carry, s, cb):
        m, l, acc = carry
        m_new = jnp.maximum(m, jnp.max(s, axis=-1, keepdims=True))
        pexp = jnp.exp(s - m_new)
        alpha = jnp.exp(m - m_new)
        l = alpha * l + jnp.sum(pexp, axis=-1, keepdims=True)
        acc = alpha * acc + jnp.dot(pexp.astype(BF16), cb, preferred_element_type=F32)
        return m_new, l, acc

    def group(g, carries):
        slot = g & 1

        @pl.when(g + 1 < n_groups)
        def _():
            for cp in page_copies(g + 1, 1 - slot):
                cp.start()
        if carry_over:
            @pl.when((g + 1 == n_groups) & jnp.logical_not(last_step))
            def _():
                for cp in page_copies(0, 0, b0 + NS):
                    cp.start()
        for cp in page_copies(g, slot):
            cp.wait()
        seqs = range(NS)
        cbs = [cbuf[slot, sq].reshape(G * PAGE_SIZE, MLA_KV_RANK).astype(BF16) for sq in seqs]
        kts = [kbuf[slot, sq].astype(BF16) for sq in seqs]
        ss = [_dot_nt(qls[sq], cbs[sq]) for sq in seqs]
        ss = [ss[sq] + jnp.concatenate([jnp.dot(qps[sq], kts[sq][i], preferred_element_type=F32) for i in range(G)], axis=-1)
              for sq in seqs]
        ms = [jnp.maximum(carries[sq][0], jnp.max(ss[sq], axis=-1, keepdims=True)) for sq in seqs]
        ps = [jnp.exp(ss[sq] - ms[sq]) for sq in seqs]
        alphas = [jnp.exp(carries[sq][0] - ms[sq]) for sq in seqs]
        ls = [alphas[sq] * carries[sq][1] + jnp.sum(ps[sq], axis=-1, keepdims=True) for sq in seqs]
        accs = [alphas[sq] * carries[sq][2] + jnp.dot(ps[sq].astype(BF16), cbs[sq], preferred_element_type=F32) for sq in seqs]
        return tuple((ms[sq], ls[sq], accs[sq]) for sq in seqs)

    one = (jnp.full((R, 1), NEG_INF, F32), jnp.zeros((R, 1), F32), jnp.zeros((R, MLA_KV_RANK), F32))
    carries = lax.fori_loop(0, n_groups, group, (one,) * NS)

    pad = PAGE_SIZE - T
    tq = _iota((R, PAGE_SIZE), 0) & (T - 1)
    lane_h = _iota((T, MLA_HEADS * MLA_VDIM), 1) // MLA_VDIM
    for sq in range(NS):
        rows = slice(sq * T, (sq + 1) * T)
        cn = jnp.concatenate([c_ref[rows, :], jnp.zeros((pad, MLA_KV_RANK), F32)], axis=0).astype(BF16)
        kn = jnp.concatenate([kpe_ref[rows, :], jnp.zeros((pad, MLA_ROPE), F32)], axis=0).astype(BF16)
        s = jnp.where(_iota((R, PAGE_SIZE), 1) <= tq, _dot_nt(qls[sq], cn) + _dot_nt(qps[sq], kn), NEG_INF)
        _, l, acc = update(carries[sq], s, cn)
        z = _dot(acc / l, wuv_ref[...])
        out = jnp.zeros((T, MLA_HEADS * MLA_VDIM), F32)
        for h in range(MLA_HEADS):
            out = jnp.where(lane_h == h, z[h * T:(h + 1) * T], out)
        o_ref[rows, :] = out.astype(o_ref.dtype)


def _attn_sample(qlat, q, c, kpe, wuv, cache_ckv, cache_kpe_t, page_table, layer, B, T):
    n = qlat.shape[0]
    n_pages = page_table.shape[1]
    ns = SEQ_PER_STEP
    row = lambda w: pl.BlockSpec((ns * T, w), lambda b, pt: (b, 0))
    new = lambda w: pl.BlockSpec((None, ns * T, w), lambda b, pt: (layer, b, 0))
    hbm = pl.BlockSpec(memory_space=pl.ANY)
    grid_spec = pltpu.PrefetchScalarGridSpec(
        num_scalar_prefetch=1, grid=(B // ns,),
        in_specs=[row(qlat.shape[1]), row(q.shape[1]), new(MLA_KV_RANK), new(MLA_ROPE),
                  pl.BlockSpec(wuv.shape, lambda b, pt: (0, 0)), hbm, hbm],
        out_specs=row(MLA_HEADS * MLA_VDIM),
        scratch_shapes=[pltpu.VMEM((2, ns, PAGE_GROUP, PAGE_SIZE, MLA_KV_RANK), F32),
                        pltpu.VMEM((2, ns, PAGE_GROUP, MLA_ROPE, PAGE_SIZE), F32),
                        pltpu.SemaphoreType.DMA((2, 2))])
    return pl.pallas_call(
        functools.partial(_attn_sample_kernel, T=T, layer=layer, n_pages=n_pages), grid_spec=grid_spec,
        out_shape=jax.ShapeDtypeStruct((n, MLA_HEADS * MLA_VDIM), F32),
        compiler_params=_params(("arbitrary",)), name="attn_sample")(
            page_table, qlat, q, c, kpe, wuv, cache_ckv, cache_kpe_t)


def _rope_rows():
    half = MLA_ROPE // 2
    inv = jnp.exp(-jnp.log(ROPE_THETA) * jnp.arange(half, dtype=F32) / half)
    lane = np.arange(LANES)
    on = (lane >= ROPE_LANE0) & (lane < ROPE_LANE0 + MLA_ROPE)
    inv_l = jnp.concatenate([jnp.zeros(ROPE_LANE0, F32), inv, inv, jnp.zeros(LANES - ROPE_LANE0 - MLA_ROPE, F32)])
    sign = np.where(on, np.where(lane < ROPE_LANE0 + half, -1.0, 1.0), 0.0)
    rows = [inv_l, (lane < ROPE_LANE0).astype(np.float32), sign, on.astype(np.float32)]
    rows += [np.zeros(LANES, np.float32)] * 4
    return jnp.stack([jnp.asarray(r, F32) for r in rows])


def _layer_weights(l, W):
    half = MLA_ROPE // 2
    r2 = lambda a: a.reshape(1, -1)
    win_t = jnp.swapaxes(W["w_in"], 1, 2)
    wqb = W["mla_wqb"][l].reshape(MLA_Q_RANK, MLA_HEADS, MLA_NOPE + MLA_ROPE)
    zq = lambda n: jnp.zeros((MLA_Q_RANK, MLA_HEADS, n), F32)
    rest = HEAD_SLOT - MLA_NOPE - MLA_ROPE
    wq_pad = jnp.concatenate([wqb, zq(rest)], axis=-1)
    wq_rot = jnp.concatenate([zq(MLA_NOPE), wqb[..., MLA_NOPE + half:], wqb[..., MLA_NOPE:MLA_NOPE + half], zq(rest)], axis=-1)
    hw = MLA_HEADS * HEAD_SLOT
    wq = jnp.concatenate([wq_pad.reshape(MLA_Q_RANK, hw), wq_rot.reshape(MLA_Q_RANK, hw)], axis=1)
    wuk = W["mla_wuk"][l]
    wuk_pad = jnp.concatenate([wuk, jnp.zeros((MLA_KV_RANK, MLA_HEADS, HEAD_SLOT - MLA_NOPE), F32)], axis=-1).reshape(MLA_KV_RANK, hw)
    wukt = wuk.transpose(1, 2, 0).reshape(MLA_HEADS * MLA_NOPE, MLA_KV_RANK)
    wuv = W["mla_wuv"][l].reshape(MLA_KV_RANK, MLA_HEADS * MLA_VDIM)
    proj = dict(g=r2(W["norm_mix"][l]), win=win_t, layer=l,
                qn=r2(W["mla_q_norm"][l]), wq=wq.astype(BF16), kvn=r2(W["mla_kv_norm"][l]), rope=_rope_rows(),
                wuk=wuk_pad.astype(BF16), wuv=wuv.astype(BF16), wukt=wukt.astype(BF16))
    zr = lambda n: jnp.zeros((n, RW_WIDTH), F32)
    rwk = dict(mu=r2(W["rw_mu"][l]), w0=r2(W["rw_w0"][l]),
               w2=jnp.concatenate([W["rw_w2"][l], zr(96)], axis=0).astype(BF16),
               a0=r2(W["rw_a0"][l]),
               a2=jnp.concatenate([zr(32), W["rw_a2"][l], zr(64)], axis=0).astype(BF16),
               g2=jnp.concatenate([zr(64), W["rw_g2"][l]], axis=0).astype(BF16),
               kk=r2(W["rw_kk"][l]), ka=r2(W["rw_ka"][l]), rk=r2(W["rw_rk"][l]),
               lnw=r2(W["rw_ln_w"][l]), lnb=r2(W["rw_ln_b"][l]))
    return dict(
        proj=proj, rwkv=rwk,
        ffn1=(r2(W["norm_ffn1"][l]), W["ffn1_wi"], W["ffn1_wo"], l),
        ffn2=(r2(W["norm_ffn2"][l]), W["ffn2_wi"], W["ffn2_wo"], l),
        wout=W["w_out"], hgn=r2(jnp.tile(W["hg_norm"][l], HG_HEADS)))


def _trunk(x, B, T, pos0, LW, W, sample, st=None):
    n = B * T
    x = x.reshape(n, D_MODEL)
    hg_out, rw_out, sh_out = [], [], []
    prev = None
    for l in range(DEPTH):
        lw = LW[l]
        ffn = _ffn_stream if sample else _ffn
        x = ffn(x, *lw["ffn1"])
        if sample:
            p_hg, p_rw, q, qlat, c, kpe = _proj(x, lw["proj"], T, pos0, True, prev)
            o_hg, hg_s = _hgrn_sample(p_hg, W["hg_lb_logits"], lw["hgn"], st["hg"], l, B, T)
            shift_rows = jnp.repeat(st["sh"][l], T, axis=0)
            o_rw, rw_s = _rwkv_sample(p_rw, shift_rows, st["rw"].reshape(DEPTH, B, MIX_W, HEAD_D), lw["rwkv"], l, B, T)
            o_mla = _attn_sample(qlat, q, c, kpe, lw["proj"]["wuv"], st["ckv"], st["kpe"], st["pt"], l, B, T)
        else:
            p_hg, p_rw, q, k, v, c, kpe = _proj(x, lw["proj"], T, pos0, False, prev)
            o_hg, hg_s, o_rw, rw_s = _mixers_prompt(p_hg, p_rw, W["hg_lb_logits"], lw["hgn"], lw["rwkv"], l, B, T)
            o_mla = _attn_prompt(q, k, v, B, T)
        gf = W["norm_final"].reshape(1, -1) if l == DEPTH - 1 else None
        x = ffn(x, *lw["ffn2"], mix=(o_hg, o_rw, o_mla), wout=lw["wout"], gf=gf)
        hg_out.append(hg_s)
        rw_out.append(rw_s.reshape(B, RW_HEADS, RW_DH, RW_DH))
        sh_out.append(p_rw.reshape(B, T, RW_COLS)[:, -1])
        prev = (c, kpe)
    st_ = jnp.stack
    return (x.reshape(B, T, D_MODEL), st_(hg_out), st_(rw_out), st_(sh_out),
            c.reshape(DEPTH, B, T, MLA_KV_RANK), kpe.reshape(DEPTH, B, T, MLA_ROPE))


def kernel(x_prompt, x_sample, cache_mla_ckv, cache_mla_kpe, state_hgrn, state_rwkv, state_rwkv_shift, page_table, norm_ffn1, ffn1_wi, ffn1_wo, norm_mix, w_in, w_out, hg_lb_logits, hg_norm, rw_mu, rw_w0, rw_w2, rw_a0, rw_a2, rw_g2, rw_kk, rw_ka, rw_rk, rw_ln_w, rw_ln_b, mla_q_norm, mla_wqb, mla_kv_norm, mla_wuk, mla_wuv, norm_ffn2, ffn2_wi, ffn2_wo, norm_final):
    W = dict(norm_ffn1=norm_ffn1, ffn1_wi=ffn1_wi, ffn1_wo=ffn1_wo, norm_mix=norm_mix, w_in=w_in, w_out=w_out,
             hg_lb_logits=hg_lb_logits, hg_norm=hg_norm, rw_mu=rw_mu, rw_w0=rw_w0, rw_w2=rw_w2, rw_a0=rw_a0,
             rw_a2=rw_a2, rw_g2=rw_g2, rw_kk=rw_kk, rw_ka=rw_ka, rw_rk=rw_rk, rw_ln_w=rw_ln_w, rw_ln_b=rw_ln_b,
             mla_q_norm=mla_q_norm, mla_wqb=mla_wqb, mla_kv_norm=mla_kv_norm, mla_wuk=mla_wuk, mla_wuv=mla_wuv,
             norm_ffn2=norm_ffn2, ffn2_wi=ffn2_wi, ffn2_wo=ffn2_wo, norm_final=norm_final)
    LW = [_layer_weights(l, W) for l in range(DEPTH)]
    Bp, Tp, _ = x_prompt.shape
    y_p, hg_p, rw_p, sh_p, ckv_p, kpe_p = _trunk(x_prompt, Bp, Tp, 0, LW, W, False)
    Bs, Ts, _ = x_sample.shape
    past_len = page_table.shape[1] * cache_mla_ckv.shape[2]
    st = dict(hg=state_hgrn, rw=state_rwkv, sh=state_rwkv_shift, ckv=cache_mla_ckv,
              kpe=jnp.swapaxes(cache_mla_kpe, 2, 3), pt=page_table)
    y_s, hg_s, rw_s, sh_s, ckv_s, kpe_s = _trunk(x_sample, Bs, Ts, past_len, LW, W, True, st)
    return (y_p, y_s, hg_p, hg_s, rw_p, rw_s, sh_p, sh_s, ckv_p, ckv_s, kpe_p, kpe_s)
```

```python
import functools
import math

import jax
import jax.numpy as jnp
import numpy as np
from jax import lax
from jax.experimental import pallas as pl
from jax.experimental.pallas import tpu as pltpu

F32 = jnp.float32
BF16 = jnp.bfloat16

D_MODEL = 1024
DEPTH = 2
PAGE_SIZE = 128
EPS = 1e-6
D_FF = 2816
HG_HEADS = 4
HG_DK = 64
HG_KW = 256
HG_MIN_F = 1e-30
RW_HEADS = 4
RW_DH = 64
RW_WIDTH = 256
RW_COLS = 896
RW_GN_EPS = 64e-5
MLA_HEADS = 8
MLA_NOPE = 64
MLA_ROPE = 32
MLA_VDIM = 64
MLA_Q_RANK = 384
MLA_KV_RANK = 256
MLA_SCALE = (MLA_NOPE + MLA_ROPE) ** -0.5
ROPE_THETA = 10000.0

LANES = 128
HEAD_SLOT = 128
ROPE_LANE0 = 64
VMEM_LIMIT = 56 * 1024 * 1024
NEG_INF = -1e30

ROW_TILE = 512
MIX_BLOCK = 256
MIX_SEQS = 2
HG_CHUNK = 16
RW_CHUNK = 64
ATTN_TILE = 1024
ATTN_HEADS = 2
HG_COLS = 2 * HG_KW + 2 * HG_HEADS * HG_DK
MLA_COL0 = HG_COLS + RW_COLS


def _dot(a, b):
    return jnp.dot(a.astype(BF16), b.astype(BF16), preferred_element_type=F32)


def _dot_nt(a, b):
    return lax.dot_general(a.astype(BF16), b.astype(BF16), (((1,), (1,)), ((), ())), preferred_element_type=F32)


def _dot_xl(m, b, pieces=2):
    out = None
    rem = b
    for _ in range(pieces):
        part = rem.astype(BF16)
        term = jnp.dot(m, part, preferred_element_type=F32)
        out = term if out is None else out + term
        rem = rem - part.astype(F32)
    return out


def _dot_xr(a, m, pieces=2):
    out = None
    rem = a
    for _ in range(pieces):
        part = rem.astype(BF16)
        term = jnp.dot(part, m, preferred_element_type=F32)
        out = term if out is None else out + term
        rem = rem - part.astype(F32)
    return out


def _rms(x, g, eps=EPS):
    return x * lax.rsqrt(jnp.mean(x * x, axis=-1, keepdims=True) + eps) * g


def _sigmoid(x):
    return 1.0 / (1.0 + jnp.exp(-x))


def _iota(shape, dim):
    return lax.broadcasted_iota(jnp.int32, shape, dim)


def _head_block_ones(n, hd):
    return (_iota((n, n), 0) // hd == _iota((n, n), 1) // hd).astype(BF16)


def _full(shape):
    return pl.BlockSpec(shape, lambda *_: (0,) * len(shape), pipeline_mode=pl.Buffered(1))


def _layer_block(shape, layer):
    return pl.BlockSpec((None,) + tuple(shape[1:]), lambda *_: (layer,) + (0,) * (len(shape) - 1),
                        pipeline_mode=pl.Buffered(1))


def _params(sem):
    return pltpu.CompilerParams(dimension_semantics=sem, vmem_limit_bytes=VMEM_LIMIT)


FF_CHUNK = 256


def _ffn_kernel(*refs, has_mix, final):
    it = iter(refs)
    x_ref = next(it)
    if has_mix:
        ohg_ref, orw_ref, omla_ref, wout_ref = next(it), next(it), next(it), next(it)
    g_ref, wi_ref, wo_ref = next(it), next(it), next(it)
    gf_ref = next(it) if final else None
    o_ref = next(it)

    x = x_ref[...]
    if has_mix:
        o = jnp.concatenate([r[...].astype(BF16) for r in (ohg_ref, orw_ref, omla_ref)], axis=-1)
        x = x + jnp.dot(o, wout_ref[...].astype(BF16), preferred_element_type=F32)
    xn = _rms(x, g_ref[...]).astype(BF16)
    acc = jnp.zeros(x.shape, F32)
    for c in range(D_FF // FF_CHUNK):
        lo = c * FF_CHUNK
        a = jnp.dot(xn, wi_ref[:, lo:lo + FF_CHUNK].astype(BF16), preferred_element_type=F32)
        b = jnp.dot(xn, wi_ref[:, D_FF + lo:D_FF + lo + FF_CHUNK].astype(BF16), preferred_element_type=F32)
        h = (a * _sigmoid(a) * b).astype(BF16)
        acc = acc + jnp.dot(h, wo_ref[lo:lo + FF_CHUNK, :].astype(BF16), preferred_element_type=F32)
    y = x + 0.5 * acc
    if final:
        y = _rms(y, gf_ref[...])
    o_ref[...] = y


def _ffn(x, g, wi, wo, layer, mix=None, wout=None, gf=None, tm=ROW_TILE):
    n = x.shape[0]
    tm = min(tm, n)
    row = lambda w: pl.BlockSpec((tm, w), lambda i: (i, 0))
    args, specs = [x], [row(D_MODEL)]
    if mix is not None:
        for m in mix:
            args.append(m)
            specs.append(row(m.shape[1]))
        args.append(wout)
        specs.append(_layer_block(wout.shape, layer))
    args += [g, wi, wo]
    specs += [_full(g.shape), _layer_block(wi.shape, layer), _layer_block(wo.shape, layer)]
    if gf is not None:
        args.append(gf)
        specs.append(_full(gf.shape))
    return pl.pallas_call(
        functools.partial(_ffn_kernel, has_mix=mix is not None, final=gf is not None),
        grid=(n // tm,), in_specs=specs, out_specs=row(D_MODEL),
        out_shape=jax.ShapeDtypeStruct((n, D_MODEL), F32),
        compiler_params=_params(("parallel",)), name="ffn")(*args)


def _ffn_stream_kernel(*refs, has_mix, final):
    it = iter(refs)
    x_ref = next(it)
    if has_mix:
        ohg_ref, orw_ref, omla_ref, wout_ref = next(it), next(it), next(it), next(it)
    g_ref, wia_ref, wib_ref, wo_ref = next(it), next(it), next(it), next(it)
    gf_ref = next(it) if final else None
    o_ref = next(it)
    x_scr, xn_scr, acc_scr = next(it), next(it), next(it)
    c = pl.program_id(0)

    @pl.when(c == 0)
    def _():
        x = x_ref[...]
        if has_mix:
            o = jnp.concatenate([r[...].astype(BF16) for r in (ohg_ref, orw_ref, omla_ref)], axis=-1)
            x = x + jnp.dot(o, wout_ref[...].astype(BF16), preferred_element_type=F32)
        x_scr[...] = x
        xn_scr[...] = _rms(x, g_ref[...]).astype(BF16)
        acc_scr[...] = jnp.zeros(acc_scr.shape, F32)

    xn = xn_scr[...]
    a = jnp.dot(xn, wia_ref[...].astype(BF16), preferred_element_type=F32)
    b = jnp.dot(xn, wib_ref[...].astype(BF16), preferred_element_type=F32)
    h = (a * _sigmoid(a) * b).astype(BF16)
    acc_scr[...] += jnp.dot(h, wo_ref[...].astype(BF16), preferred_element_type=F32)

    @pl.when(c == pl.num_programs(0) - 1)
    def _():
        y = x_scr[...] + 0.5 * acc_scr[...]
        if final:
            y = _rms(y, gf_ref[...])
        o_ref[...] = y


def _ffn_stream(x, g, wi, wo, layer, mix=None, wout=None, gf=None):
    n = x.shape[0]
    n_chunks = D_FF // FF_CHUNK
    const = lambda a: pl.BlockSpec(a.shape, lambda c: (0,) * a.ndim)
    args, specs = [x], [const(x)]
    if mix is not None:
        args += list(mix) + [wout]
        specs += [const(m) for m in mix] + [_layer_block(wout.shape, layer)]
    args += [g, wi, wi, wo]
    specs += [const(g),
              pl.BlockSpec((None, D_MODEL, FF_CHUNK), lambda c: (layer, 0, c)),
              pl.BlockSpec((None, D_MODEL, FF_CHUNK), lambda c: (layer, 0, n_chunks + c)),
              pl.BlockSpec((None, FF_CHUNK, D_MODEL), lambda c: (layer, c, 0))]
    if gf is not None:
        args.append(gf)
        specs.append(const(gf))
    return pl.pallas_call(
        functools.partial(_ffn_stream_kernel, has_mix=mix is not None, final=gf is not None),
        grid=(n_chunks,), in_specs=specs, out_specs=pl.BlockSpec((n, D_MODEL), lambda c: (0, 0)),
        out_shape=jax.ShapeDtypeStruct((n, D_MODEL), F32),
        scratch_shapes=[pltpu.VMEM((n, D_MODEL), F32), pltpu.VMEM((n, D_MODEL), BF16), pltpu.VMEM((n, D_MODEL), F32)],
        compiler_params=_params(("arbitrary",)), name="ffn_stream")(*args)


def _proj_kernel(*refs, seq_len, pos0, sample, layer):
    it = iter(refs)
    x_ref, g_ref, win_ref = (next(it) for _ in range(3))
    qn_ref, wq_ref, kvn_ref, rope_ref = (next(it) for _ in range(4))
    cprev_ref, kprev_ref = (next(it), next(it)) if layer else (None, None)
    if sample:
        wukt_ref = next(it)
    else:
        wuk_ref, wuv_ref = next(it), next(it)
    phg_ref, prw_ref, q_ref = next(it), next(it), next(it)
    if sample:
        qlat_ref = next(it)
    else:
        k_ref, v_ref = next(it), next(it)
    c_ref, kpe_ref = next(it), next(it)

    tm = x_ref.shape[0]
    xn = _rms(x_ref[...], g_ref[...]).astype(BF16)
    mla_lat = MLA_Q_RANK + MLA_KV_RANK
    phg_ref[...] = _dot_nt(xn, win_ref[0:HG_COLS, :])
    prw_ref[...] = _dot_nt(xn, win_ref[HG_COLS:MLA_COL0, :])
    pm = _dot_nt(xn, win_ref[MLA_COL0:MLA_COL0 + mla_lat, :])
    wk = win_ref[MLA_COL0 + mla_lat:MLA_COL0 + mla_lat + MLA_ROPE, :]
    zrow = lambda n_: jnp.zeros((n_, D_MODEL), F32)
    tail = HEAD_SLOT - ROPE_LANE0 - MLA_ROPE
    wk2 = jnp.concatenate([zrow(ROPE_LANE0), wk, zrow(tail),
                           zrow(ROPE_LANE0), wk[MLA_ROPE // 2:], wk[:MLA_ROPE // 2], zrow(tail)], axis=0)
    pk = _dot_nt(xn, wk2)

    row = pl.program_id(0) * tm + _iota((tm, 1), 0)
    pos = (pos0 + (row & (seq_len - 1))).astype(F32)
    rope = rope_ref[...]
    ang = pos * rope[0:1, :]
    cos_s = rope[1:2, :] + rope[3:4, :] * jnp.cos(ang)
    sin_s = rope[2:3, :] * jnp.sin(ang)

    c = _rms(pm[:, MLA_Q_RANK:MLA_Q_RANK + MLA_KV_RANK], kvn_ref[...])
    if layer:
        c_ref[0:layer] = cprev_ref[...]
        kpe_ref[0:layer] = kprev_ref[...]
    c_ref[layer] = c
    k_slot = pk[:, :HEAD_SLOT] * cos_s + pk[:, HEAD_SLOT:] * sin_s
    kpe_ref[layer] = k_slot[:, ROPE_LANE0:ROPE_LANE0 + MLA_ROPE]

    qn = _rms(pm[:, :MLA_Q_RANK], qn_ref[...]).astype(BF16)
    q2 = jnp.dot(qn, wq_ref[...], preferred_element_type=F32)
    hw = MLA_HEADS * HEAD_SLOT
    q_scale = MLA_SCALE if sample else MLA_SCALE * math.log2(math.e)
    cos_q = jnp.concatenate([cos_s * q_scale] * MLA_HEADS, axis=-1)
    sin_q = jnp.concatenate([sin_s * q_scale] * MLA_HEADS, axis=-1)
    q = q2[:, :hw] * cos_q + q2[:, hw:] * sin_q
    if sample:
        q_ref[...] = q
        for h in range(MLA_HEADS):
            qh = q[:, h * HEAD_SLOT:h * HEAD_SLOT + MLA_NOPE]
            qlat_ref[:, h * MLA_KV_RANK:(h + 1) * MLA_KV_RANK] = _dot(qh, wukt_ref[h * MLA_NOPE:(h + 1) * MLA_NOPE, :])
    else:
        q_ref[...] = q.astype(BF16)
        cb = c.astype(BF16)
        kn = jnp.dot(cb, wuk_ref[...], preferred_element_type=F32)
        k_ref[...] = (kn + jnp.concatenate([k_slot] * MLA_HEADS, axis=-1)).astype(BF16)
        v_ref[...] = jnp.dot(cb, wuv_ref[...], preferred_element_type=F32).astype(BF16)


def _proj(x, pw, seq_len, pos0, sample, prev=None, tm=ROW_TILE):
    n = x.shape[0]
    tm = min(tm, n)
    row = lambda w: pl.BlockSpec((tm, w), lambda i: (i, 0))
    stack = lambda depth, w: pl.BlockSpec((depth, tm, w), lambda i: (0, i, 0))
    hw = MLA_HEADS * HEAD_SLOT
    layer = pw["layer"]
    args = [x, pw["g"], pw["win"], pw["qn"], pw["wq"], pw["kvn"], pw["rope"]]
    specs = [row(D_MODEL)] + [_full(a.shape) for a in args[1:]]
    specs[2] = _layer_block(pw["win"].shape, layer)
    if layer:
        args += list(prev)
        specs += [stack(layer, MLA_KV_RANK), stack(layer, MLA_ROPE)]
    tail = [pw["wukt"]] if sample else [pw["wuk"], pw["wuv"]]
    args += tail
    specs += [_full(a.shape) for a in tail]
    qdt = F32 if sample else BF16
    outs = [((n, HG_COLS), F32), ((n, RW_COLS), F32), ((n, hw), qdt)]
    outs += [((n, MLA_HEADS * MLA_KV_RANK), F32)] if sample else [((n, hw), BF16), ((n, MLA_HEADS * MLA_VDIM), BF16)]
    out_specs = [row(s[1]) for s, _ in outs] + [stack(layer + 1, MLA_KV_RANK), stack(layer + 1, MLA_ROPE)]
    outs += [((layer + 1, n, MLA_KV_RANK), F32), ((layer + 1, n, MLA_ROPE), F32)]
    return pl.pallas_call(
        functools.partial(_proj_kernel, seq_len=seq_len, pos0=pos0, sample=sample, layer=layer),
        grid=(n // tm,), in_specs=specs, out_specs=out_specs,
        out_shape=[jax.ShapeDtypeStruct(s, d) for s, d in outs],
        compiler_params=_params(("parallel",)), name="proj")(*args)


MIX_W = 256
HEAD_D = 64


def _chunk_masks(L, C):
    r, c = _iota((L, L), 0), _iota((L, L), 1)
    same = (r // C) == (c // C)
    return same, same & (c <= r), same & (c < r)


def _chunk_last(a, nck, C):
    a3 = a.reshape(nck, C, a.shape[-1])
    return jnp.broadcast_to(a3[:, C - 1:C, :], a3.shape).reshape(a.shape)


def _expand_state(s_kv):
    t = jnp.concatenate([s_kv] * (MIX_W // HEAD_D), axis=0)
    bd = (_iota((MIX_W, MIX_W), 0) // HEAD_D) == (_iota((MIX_W, MIX_W), 1) // HEAD_D)
    return jnp.where(bd, t, 0.0)


def _fold_state(hbd):
    out = hbd[0:HEAD_D]
    for h in range(1, MIX_W // HEAD_D):
        out = out + hbd[h * HEAD_D:(h + 1) * HEAD_D]
    return out


def _hgrn_steps(*refs, L, C, layer, chained):
    it = iter(refs)
    p_ref, lbl_ref, gn_ref = next(it), next(it), next(it)
    s0_ref = None if chained else next(it)
    o_ref, sout_ref = next(it), next(it)
    h_scr = next(it) if chained else None

    nck = L // C
    p = p_ref[...]
    q, fz, vi, gate = (p[:, i * MIX_W:(i + 1) * MIX_W] for i in range(4))

    lg = lbl_ref[...]
    e = jnp.exp(lg - jnp.max(lg, axis=0, keepdims=True))
    pr = e / jnp.sum(e, axis=0, keepdims=True)
    cs = pr[0:1]
    for i in range(1, layer + 1):
        cs = cs + pr[i:i + 1]
    lb = jnp.clip(cs - pr[0:1], 0.0, 1.0)

    f = lb + (1.0 - lb) * _sigmoid(fz)
    log_f = jnp.log(jnp.maximum(f, HG_MIN_F))
    k = (1.0 - lb) * _sigmoid(-fz)

    same, incl, _ = _chunk_masks(L, C)
    b = _dot_xl(incl.astype(BF16), log_f, pieces=3)
    b_last = _chunk_last(b, nck, C)
    ones_h = _head_block_ones(MIX_W, HEAD_D)

    b3, q3, k3, v3 = (a.reshape(nck, C, MIX_W) for a in (b * math.log2(math.e), q, k, vi))
    SUB = 8
    tiles = [jnp.zeros((nck, min(SUB, C), MIX_W), F32) for _ in range(max(C // SUB, 1))]
    for s in range(C):
        t0 = (s // SUB) * SUB
        bt, qt = b3[:, t0:, :], q3[:, t0:, :]
        rows = C - t0
        w = jnp.exp2(bt - b3[:, s:s + 1, :])
        x = jnp.where(t0 + _iota((nck, rows, MIX_W), 1) >= s, qt * w * k3[:, s:s + 1, :], 0.0)
        a_s = _dot(x.reshape(nck * rows, MIX_W), ones_h).reshape(nck, rows, MIX_W)
        upd = a_s * v3[:, s:s + 1, :]
        for ti in range(t0 // SUB, len(tiles)):
            lo = ti * SUB - t0
            tiles[ti] = tiles[ti] + upd[:, lo:lo + SUB, :]
        yield
    o_intra = jnp.concatenate(tiles, axis=1).reshape(L, MIX_W)

    qe = q * jnp.exp(b)
    kd_t = (k * jnp.exp(b_last - b)).T
    e_t = jnp.exp(b_last).T
    bd = (_iota((MIX_W, MIX_W), 0) // HEAD_D) == (_iota((MIX_W, MIX_W), 1) // HEAD_D)
    col = _iota((1, L), 1)

    if chained:
        hbd = h_scr[...]
    gks = []
    for ck in range(nck):
        gks.append(jnp.where(bd, _dot(jnp.where((col // C) == ck, kd_t, 0.0), vi), 0.0))
        yield
    dcols = [jnp.sum(jnp.where(col == ck * C, e_t, 0.0), axis=1, keepdims=True) for ck in range(nck)]
    o_parts = []
    for ck in range(nck):
        if not chained:
            s_kv = jnp.concatenate([s0_ref[ck, h] for h in range(HG_HEADS)], axis=-1)
            hbd = _expand_state(s_kv)
        o_parts.append(_dot(qe[ck * C:(ck + 1) * C], hbd))
        hbd = dcols[ck] * hbd + gks[ck]
        if not chained:
            s_new = _fold_state(hbd)
            for h in range(HG_HEADS):
                sout_ref[ck, h] = s_new[:, h * HEAD_D:(h + 1) * HEAD_D]
    if chained:
        h_scr[...] = hbd
        s_new = _fold_state(hbd)
        for h in range(HG_HEADS):
            sout_ref[h] = s_new[:, h * HEAD_D:(h + 1) * HEAD_D]

    o = o_intra + jnp.concatenate(o_parts, axis=0)
    ms = _dot_xr(o * o, ones_h) * (1.0 / HEAD_D)
    on = o * lax.rsqrt(ms + EPS) * gn_ref[...]
    o_ref[...] = (on * (gate * _sigmoid(gate))).astype(o_ref.dtype)


def _drain(*gens):
    alive = list(gens)
    while alive:
        for g in list(alive):
            try:
                next(g)
            except StopIteration:
                alive.remove(g)


def _hgrn_kernel(*refs, **kw):
    _drain(_hgrn_steps(*refs, **kw))


def _hgrn_sample(p_hg, lb_logits, gn, s0, layer, B, T):
    n = p_hg.shape[0]
    return pl.pallas_call(
        functools.partial(_hgrn_kernel, L=n, C=T, layer=layer, chained=False),
        grid=(1,),
        in_specs=[_full((n, HG_COLS)), _full(lb_logits.shape), _full(gn.shape), _layer_block(s0.shape, layer)],
        out_specs=[_full((n, MIX_W)), _full((B, HG_HEADS, HEAD_D, HEAD_D))],
        out_shape=[jax.ShapeDtypeStruct((n, MIX_W), BF16),
                   jax.ShapeDtypeStruct((B, HG_HEADS, HEAD_D, HEAD_D), F32)],
        compiler_params=_params(("arbitrary",)), name="hgrn")(p_hg, lb_logits, gn, s0)


def _rwkv_steps(*refs, L, C, chained):
    it = iter(refs)
    p_ref = next(it)
    if not chained:
        shrow_ref, s0_ref = next(it), next(it)
    (mu_ref, w0_ref, w2_ref, a0_ref, a2_ref, g2_ref, kkw_ref, ka_ref, rk_ref, lnw_ref, lnb_ref) = (next(it) for _ in range(11))
    y_ref, sout_ref = next(it), next(it)
    if chained:
        h_scr, carry_scr = next(it), next(it)

    nck = L // C
    p = p_ref[...]
    rowi = _iota((L, 1), 0)
    rolled = pltpu.roll(p, 1, 0)
    if chained:
        prev = jnp.where(rowi == 0, carry_scr[0:1, :], rolled)
        carry_scr[0:1, :] = p[L - 1:L, :]
    else:
        prev = jnp.where((rowi & (C - 1)) == 0, shrow_ref[...], rolled)
    ps = p + (prev - p) * mu_ref[...]
    r, k, v, tail = ps[:, 0:MIX_W], ps[:, MIX_W:2 * MIX_W], ps[:, 2 * MIX_W:3 * MIX_W], ps[:, 3 * MIX_W:RW_COLS]

    w_raw = -jax.nn.softplus(-(w0_ref[...] + _dot(jnp.tanh(tail), w2_ref[...]))) - 0.5
    lw = -jnp.exp(w_raw)
    a = _sigmoid(a0_ref[...] + _dot(tail, a2_ref[...]))
    g = _dot(_sigmoid(tail), g2_ref[...])

    ones_h = _head_block_ones(MIX_W, HEAD_D)
    kk = k * kkw_ref[...]
    kk = kk / jnp.maximum(jnp.sqrt(_dot_xr(kk * kk, ones_h)), 1e-12)
    k2 = k * (1.0 + (a - 1.0) * ka_ref[...])
    bb = kk * a

    same, incl, strict = _chunk_masks(L, C)
    gc = _dot_xl(incl.astype(BF16), lw)
    g_last = _chunk_last(gc, nck, C)
    eg, eng = jnp.exp(gc), jnp.exp(-gc)
    rg, kg, bg, kkg = r * eg, k2 * eng, bb * eng, kk * jnp.exp(gc - lw)
    to_end = jnp.exp(g_last - gc)
    kd_t, bd_t, e_t = (k2 * to_end).T, (bb * to_end).T, jnp.exp(g_last).T

    lane_h = _iota((1, MIX_W), 1) // HEAD_D
    eye = (_iota((L, L), 0) == _iota((L, L), 1)).astype(F32)
    u0 = jnp.zeros((L, MIX_W), F32)
    kkt = jnp.zeros((L, MIX_W), F32)
    qp = rg
    o0 = jnp.zeros((L, MIX_W), F32)
    heads = range(RW_HEADS)
    mhs = [lane_h == h for h in heads]
    kkg_hs = [jnp.where(mh, kkg, 0.0) for mh in mhs]
    rg_hs = [jnp.where(mh, rg, 0.0) for mh in mhs]
    a_kb, a_kk, a_rk, a_rb = [], [], [], []
    for h in heads:
        a_kb.append(jnp.where(strict, _dot_nt(kkg_hs[h], bg), 0.0))
        a_kk.append(jnp.where(strict, _dot_nt(kkg_hs[h], kg), 0.0))
        yield
    for h in heads:
        a_rk.append(jnp.where(incl, _dot_nt(rg_hs[h], kg), 0.0))
        a_rb.append(jnp.where(incl, _dot_nt(rg_hs[h], bg), 0.0))
        yield
    npow = [-a for a in a_kb]
    tinv = [eye + n_ for n_ in npow]
    for _ in range(int(math.log2(C)) - 1):
        npow = [_dot(n_, n_) for n_ in npow]
        yield
        tinv = [t_ + _dot(t_, n_) for t_, n_ in zip(tinv, npow)]
        yield
    x1 = [_dot(a_kk[h], v) for h in heads]
    yield
    u0_hs = [_dot(tinv[h], x1[h]) for h in heads]
    yield
    kkt_hs = [_dot(tinv[h], kkg) for h in heads]
    yield
    for h in heads:
        u0 = jnp.where(mhs[h], u0_hs[h], u0)
        kkt = jnp.where(mhs[h], kkt_hs[h], kkt)
    for h in heads:
        qp = qp - jnp.where(mhs[h], _dot(a_rb[h], kkt_hs[h]), 0.0)
        o0 = o0 + jnp.where(mhs[h], _dot(a_rk[h], v) - _dot(a_rb[h], u0_hs[h]), 0.0)
        yield

    bd = (_iota((MIX_W, MIX_W), 0) // HEAD_D) == (_iota((MIX_W, MIX_W), 1) // HEAD_D)
    col = _iota((1, L), 1)
    if chained:
        hbd = h_scr[...]
    bd_cs = [jnp.where((col // C) == ck, bd_t, 0.0) for ck in range(nck)]
    mbs, gbs = [], []
    for ck in range(nck):
        mbs.append(jnp.where(bd, _dot(bd_cs[ck], kkt), 0.0))
        gbs.append(jnp.where(bd, _dot(jnp.where((col // C) == ck, kd_t, 0.0), v) - _dot(bd_cs[ck], u0), 0.0))
        yield
    dcols = [jnp.sum(jnp.where(col == ck * C, e_t, 0.0), axis=1, keepdims=True) for ck in range(nck)]
    y_parts = []
    for ck in range(nck):
        if not chained:
            hbd = _expand_state(s0_ref[ck].T)
        y_parts.append(_dot(qp[ck * C:(ck + 1) * C], hbd))
        hbd = dcols[ck] * hbd - _dot(mbs[ck], hbd) + gbs[ck]
        yield
        if not chained:
            sout_ref[ck] = _fold_state(hbd).T
    if chained:
        h_scr[...] = hbd
        sout_ref[...] = _fold_state(hbd).T

    y = o0 + jnp.concatenate(y_parts, axis=0)
    mean = _dot_xr(y, ones_h) * (1.0 / HEAD_D)
    yc = y - mean
    var = _dot_xr(yc * yc, ones_h) * (1.0 / HEAD_D)
    yn = yc * lax.rsqrt(var + RW_GN_EPS) * lnw_ref[...] + lnb_ref[...]
    bonus = _dot_xr(r * k2 * rk_ref[...], ones_h) * v
    y_ref[...] = ((yn + bonus) * g).astype(y_ref.dtype)


def _rwkv_kernel(*refs, **kw):
    _drain(_rwkv_steps(*refs, **kw))


RWKV_PARAMS = ("mu", "w0", "w2", "a0", "a2", "g2", "kk", "ka", "rk", "lnw", "lnb")


def _rwkv_sample(p_rw, shift_rows, s0, rw, layer, B, T):
    n = p_rw.shape[0]
    wargs = [rw[k] for k in RWKV_PARAMS]
    return pl.pallas_call(
        functools.partial(_rwkv_kernel, L=n, C=T, chained=False),
        grid=(1,),
        in_specs=[_full((n, RW_COLS)), _full(shift_rows.shape), _layer_block(s0.shape, layer)] + [_full(a.shape) for a in wargs],
        out_specs=[_full((n, MIX_W)), _full((B, MIX_W, HEAD_D))],
        out_shape=[jax.ShapeDtypeStruct((n, MIX_W), BF16), jax.ShapeDtypeStruct((B, MIX_W, HEAD_D), F32)],
        compiler_params=_params(("arbitrary",)), name="rwkv")(p_rw, shift_rows, s0, *wargs)


def _mixers_kernel(*refs, L, hg_c, rw_c, layer):
    p_hg_ref, lbl_ref, gn_ref, p_rw_ref = refs[0:4]
    rw_params = refs[4:15]
    o_hg_ref, hg_s_ref, o_rw_ref, rw_s_ref = refs[15:19]
    hg_scr, rw_scr, carry_scr = refs[19:22]

    @pl.when(pl.program_id(1) == 0)
    def _():
        for scr in (hg_scr, rw_scr, carry_scr):
            scr[...] = jnp.zeros(scr.shape, F32)
    gens = []
    for sq in range(MIX_SEQS):
        gens.append(_rwkv_steps(p_rw_ref.at[sq], *rw_params, o_rw_ref.at[sq], rw_s_ref.at[sq],
                                rw_scr.at[sq], carry_scr.at[sq], L=L, C=rw_c, chained=True))
        gens.append(_hgrn_steps(p_hg_ref.at[sq], lbl_ref, gn_ref, o_hg_ref.at[sq], hg_s_ref.at[sq],
                                hg_scr.at[sq], L=L, C=hg_c, layer=layer, chained=True))
    _drain(*gens)


def _mixers_prompt(p_hg, p_rw, lb_logits, gn, rw, layer, B, T):
    n = p_hg.shape[0]
    wargs = [rw[k] for k in RWKV_PARAMS]
    L = min(MIX_BLOCK, T)
    nblk = T // L
    S = MIX_SEQS
    view = lambda a: a.reshape(B // S, S, nblk, L, a.shape[-1])
    blocks = lambda w: pl.BlockSpec((None, S, None, L, w), lambda b, j: (b, 0, j, 0, 0))
    state = lambda *dims: pl.BlockSpec((None, S) + dims, lambda b, j: (b, 0) + (0,) * len(dims))
    o_hg, hg_s, o_rw, rw_s = pl.pallas_call(
        functools.partial(_mixers_kernel, L=L, hg_c=min(HG_CHUNK, T), rw_c=min(RW_CHUNK, T), layer=layer),
        grid=(B // S, nblk),
        in_specs=[blocks(HG_COLS), _full(lb_logits.shape), _full(gn.shape), blocks(RW_COLS)] + [_full(a.shape) for a in wargs],
        out_specs=[blocks(MIX_W), state(HG_HEADS, HEAD_D, HEAD_D), blocks(MIX_W), state(MIX_W, HEAD_D)],
        out_shape=[jax.ShapeDtypeStruct((B // S, S, nblk, L, MIX_W), BF16),
                   jax.ShapeDtypeStruct((B // S, S, HG_HEADS, HEAD_D, HEAD_D), F32),
                   jax.ShapeDtypeStruct((B // S, S, nblk, L, MIX_W), BF16),
                   jax.ShapeDtypeStruct((B // S, S, MIX_W, HEAD_D), F32)],
        scratch_shapes=[pltpu.VMEM((S, MIX_W, MIX_W), F32), pltpu.VMEM((S, MIX_W, MIX_W), F32),
                        pltpu.VMEM((S, 8, RW_COLS), F32)],
        compiler_params=_params(("parallel", "arbitrary")), name="mixers")(view(p_hg), lb_logits, gn, view(p_rw), *wargs)
    return (o_hg.reshape(n, MIX_W), hg_s.reshape(B, HG_HEADS, HEAD_D, HEAD_D),
            o_rw.reshape(n, MIX_W), rw_s.reshape(B, MIX_W, HEAD_D))


def _attn_prompt_kernel(q_ref, k_ref, v_ref, o_ref, *, tq):
    i = pl.program_id(2)
    nh = ATTN_HEADS
    qs = [q_ref[:, hh * HEAD_SLOT:(hh + 1) * HEAD_SLOT] for hh in range(nh)]

    def step(j, carry, diagonal):
        start = pl.multiple_of(j * tq, tq)
        vb = v_ref[pl.ds(start, tq), :]
        new = []
        for hh in range(nh):
            m, l, acc = carry[hh]
            kb = k_ref[pl.ds(start, tq), hh * HEAD_SLOT:(hh + 1) * HEAD_SLOT]
            s = lax.dot_general(qs[hh], kb, (((1,), (1,)), ((), ())), preferred_element_type=F32)
            if diagonal:
                s = jnp.where(_iota((tq, tq), 1) <= _iota((tq, tq), 0), s, NEG_INF)
            m_new = jnp.maximum(m, jnp.max(s, axis=-1, keepdims=True))
            pexp = jnp.exp2(s - m_new)
            alpha = jnp.exp2(m - m_new)
            l = alpha * l + jnp.sum(pexp, axis=-1, keepdims=True)
            acc = alpha * acc + jnp.dot(pexp.astype(BF16), vb, preferred_element_type=F32)
            new.append((m_new, l, acc))
        return tuple(new)

    one = (jnp.full((tq, 1), NEG_INF, F32), jnp.zeros((tq, 1), F32), jnp.zeros((tq, nh * MLA_VDIM), F32))
    carry = lax.fori_loop(0, i, lambda j, c: step(j, c, False), (one,) * nh)
    final = step(i, carry, True)
    lane_head = _iota((tq, nh * MLA_VDIM), 1) // MLA_VDIM
    out = final[0][2] / final[0][1]
    for hh in range(1, nh):
        out = jnp.where(lane_head == hh, final[hh][2] / final[hh][1], out)
    o_ref[...] = out.astype(o_ref.dtype)


def _attn_prompt(q, k, v, B, T, tq=ATTN_TILE):
    n = q.shape[0]
    tq = min(tq, T)
    nq = T // tq
    return pl.pallas_call(
        functools.partial(_attn_prompt_kernel, tq=tq),
        grid=(B, MLA_HEADS // ATTN_HEADS, nq),
        in_specs=[pl.BlockSpec((tq, ATTN_HEADS * HEAD_SLOT), lambda b, hp, i: (b * nq + i, hp)),
                  pl.BlockSpec((T, ATTN_HEADS * HEAD_SLOT), lambda b, hp, i: (b, hp)),
                  pl.BlockSpec((T, ATTN_HEADS * MLA_VDIM), lambda b, hp, i: (b, hp))],
        out_specs=pl.BlockSpec((tq, ATTN_HEADS * MLA_VDIM), lambda b, hp, i: (b * nq + i, hp)),
        out_shape=jax.ShapeDtypeStruct((n, MLA_HEADS * MLA_VDIM), BF16),
        compiler_params=_params(("parallel", "parallel", "arbitrary")), name="attn_prompt")(q, k, v)


PAGE_GROUP = 16
SEQ_PER_STEP = 4


def _attn_sample_kernel(pt_ref, qlat_ref, q_ref, c_ref, kpe_ref, wuv_ref, ckv_hbm, kpe_hbm, o_ref,
                        cbuf, kbuf, sem, *, T, layer, n_pages):
    G, NS = PAGE_GROUP, SEQ_PER_STEP
    b0 = pl.program_id(0) * NS
    n_groups = n_pages // G
    R = MLA_HEADS * T

    def page_copies(g, slot, first_seq=b0):
        out = []
        for sq in range(NS):
            for i in range(G):
                pg = pt_ref[first_seq + sq, g * G + i]
                out.append(pltpu.make_async_copy(ckv_hbm.at[layer, pg], cbuf.at[slot, sq, i], sem.at[0, slot]))
                out.append(pltpu.make_async_copy(kpe_hbm.at[layer, pg], kbuf.at[slot, sq, i], sem.at[1, slot]))
        return out

    carry_over = n_groups % 2 == 0
    first_step = pl.program_id(0) == 0
    last_step = pl.program_id(0) == pl.num_programs(0) - 1

    def start_first_group():
        for cp in page_copies(0, 0):
            cp.start()

    if carry_over:
        pl.when(first_step)(start_first_group)
    else:
        start_first_group()

    qls, qps = [], []
    for sq in range(NS):
        rows = slice(sq * T, (sq + 1) * T)
        qls.append(jnp.concatenate([qlat_ref[rows, h * MLA_KV_RANK:(h + 1) * MLA_KV_RANK]
                                    for h in range(MLA_HEADS)], axis=0).astype(BF16))
        qs = jnp.concatenate([q_ref[rows, h * HEAD_SLOT:(h + 1) * HEAD_SLOT] for h in range(MLA_HEADS)], axis=0)
        qps.append(qs[:, ROPE_LANE0:ROPE_LANE0 + MLA_ROPE].astype(BF16))

    def update(carry, s, cb):
        m, l, acc = carry
        m_new = jnp.maximum(m, jnp.max(s, axis=-1, keepdims=True))
        pexp = jnp.exp(s - m_new)
        alpha = jnp.exp(m - m_new)
        l = alpha * l + jnp.sum(pexp, axis=-1, keepdims=True)
        acc = alpha * acc + jnp.dot(pexp.astype(BF16), cb, preferred_element_type=F32)
        return m_new, l, acc

    def group(g, carries):
        slot = g & 1

        @pl.when(g + 1 < n_groups)
        def _():
            for cp in page_copies(g + 1, 1 - slot):
                cp.start()
        if carry_over:
            @pl.when((g + 1 == n_groups) & jnp.logical_not(last_step))
            def _():
                for cp in page_copies(0, 0, b0 + NS):
                    cp.start()
        for cp in page_copies(g, slot):
            cp.wait()
        seqs = range(NS)
        cbs = [cbuf[slot, sq].reshape(G * PAGE_SIZE, MLA_KV_RANK).astype(BF16) for sq in seqs]
        kts = [kbuf[slot, sq].astype(BF16) for sq in seqs]
        ss = [_dot_nt(qls[sq], cbs[sq]) for sq in seqs]
        ss = [ss[sq] + jnp.concatenate([jnp.dot(qps[sq], kts[sq][i], preferred_element_type=F32) for i in range(G)], axis=-1)
              for sq in seqs]
        ms = [jnp.maximum(carries[sq][0], jnp.max(ss[sq], axis=-1, keepdims=True)) for sq in seqs]
        ps = [jnp.exp(ss[sq] - ms[sq]) for sq in seqs]
        alphas = [jnp.exp(carries[sq][0] - ms[sq]) for sq in seqs]
        ls = [alphas[sq] * carries[sq][1] + jnp.sum(ps[sq], axis=-1, keepdims=True) for sq in seqs]
        accs = [alphas[sq] * carries[sq][2] + jnp.dot(ps[sq].astype(BF16), cbs[sq], preferred_element_type=F32) for sq in seqs]
        return tuple((ms[sq], ls[sq], accs[sq]) for sq in seqs)

    one = (jnp.full((R, 1), NEG_INF, F32), jnp.zeros((R, 1), F32), jnp.zeros((R, MLA_KV_RANK), F32))
    carries = lax.fori_loop(0, n_groups, group, (one,) * NS)

    pad = PAGE_SIZE - T
    tq = _iota((R, PAGE_SIZE), 0) & (T - 1)
    lane_h = _iota((T, MLA_HEADS * MLA_VDIM), 1) // MLA_VDIM
    for sq in range(NS):
        rows = slice(sq * T, (sq + 1) * T)
        cn = jnp.concatenate([c_ref[rows, :], jnp.zeros((pad, MLA_KV_RANK), F32)], axis=0).astype(BF16)
        kn = jnp.concatenate([kpe_ref[rows, :], jnp.zeros((pad, MLA_ROPE), F32)], axis=0).astype(BF16)
        s = jnp.where(_iota((R, PAGE_SIZE), 1) <= tq, _dot_nt(qls[sq], cn) + _dot_nt(qps[sq], kn), NEG_INF)
        _, l, acc = update(carries[sq], s, cn)
        z = _dot(acc / l, wuv_ref[...])
        out = jnp.zeros((T, MLA_HEADS * MLA_VDIM), F32)
        for h in range(MLA_HEADS):
            out = jnp.where(lane_h == h, z[h * T:(h + 1) * T], out)
        o_ref[rows, :] = out.astype(o_ref.dtype)


def _attn_sample(qlat, q, c, kpe, wuv, cache_ckv, cache_kpe_t, page_table, layer, B, T):
    n = qlat.shape[0]
    n_pages = page_table.shape[1]
    ns = SEQ_PER_STEP
    row = lambda w: pl.BlockSpec((ns * T, w), lambda b, pt: (b, 0))
    new = lambda w: pl.BlockSpec((None, ns * T, w), lambda b, pt: (layer, b, 0))
    hbm = pl.BlockSpec(memory_space=pl.ANY)
    grid_spec = pltpu.PrefetchScalarGridSpec(
        num_scalar_prefetch=1, grid=(B // ns,),
        in_specs=[row(qlat.shape[1]), row(q.shape[1]), new(MLA_KV_RANK), new(MLA_ROPE),
                  pl.BlockSpec(wuv.shape, lambda b, pt: (0, 0)), hbm, hbm],
        out_specs=row(MLA_HEADS * MLA_VDIM),
        scratch_shapes=[pltpu.VMEM((2, ns, PAGE_GROUP, PAGE_SIZE, MLA_KV_RANK), F32),
                        pltpu.VMEM((2, ns, PAGE_GROUP, MLA_ROPE, PAGE_SIZE), F32),
                        pltpu.SemaphoreType.DMA((2, 2))])
    return pl.pallas_call(
        functools.partial(_attn_sample_kernel, T=T, layer=layer, n_pages=n_pages), grid_spec=grid_spec,
        out_shape=jax.ShapeDtypeStruct((n, MLA_HEADS * MLA_VDIM), F32),
        compiler_params=_params(("arbitrary",)), name="attn_sample")(
            page_table, qlat, q, c, kpe, wuv, cache_ckv, cache_kpe_t)


def _rope_rows():
    half = MLA_ROPE // 2
    inv = jnp.exp(-jnp.log(ROPE_THETA) * jnp.arange(half, dtype=F32) / half)
    lane = np.arange(LANES)
    on = (lane >= ROPE_LANE0) & (lane < ROPE_LANE0 + MLA_ROPE)
    inv_l = jnp.concatenate([jnp.zeros(ROPE_LANE0, F32), inv, inv, jnp.zeros(LANES - ROPE_LANE0 - MLA_ROPE, F32)])
    sign = np.where(on, np.where(lane < ROPE_LANE0 + half, -1.0, 1.0), 0.0)
    rows = [inv_l, (lane < ROPE_LANE0).astype(np.float32), sign, on.astype(np.float32)]
    rows += [np.zeros(LANES, np.float32)] * 4
    return jnp.stack([jnp.asarray(r, F32) for r in rows])


def _layer_weights(l, W):
    half = MLA_ROPE // 2
    r2 = lambda a: a.reshape(1, -1)
    win_t = jnp.swapaxes(W["w_in"], 1, 2)
    wqb = W["mla_wqb"][l].reshape(MLA_Q_RANK, MLA_HEADS, MLA_NOPE + MLA_ROPE)
    zq = lambda n: jnp.zeros((MLA_Q_RANK, MLA_HEADS, n), F32)
    rest = HEAD_SLOT - MLA_NOPE - MLA_ROPE
    wq_pad = jnp.concatenate([wqb, zq(rest)], axis=-1)
    wq_rot = jnp.concatenate([zq(MLA_NOPE), wqb[..., MLA_NOPE + half:], wqb[..., MLA_NOPE:MLA_NOPE + half], zq(rest)], axis=-1)
    hw = MLA_HEADS * HEAD_SLOT
    wq = jnp.concatenate([wq_pad.reshape(MLA_Q_RANK, hw), wq_rot.reshape(MLA_Q_RANK, hw)], axis=1)
    wuk = W["mla_wuk"][l]
    wuk_pad = jnp.concatenate([wuk, jnp.zeros((MLA_KV_RANK, MLA_HEADS, HEAD_SLOT - MLA_NOPE), F32)], axis=-1).reshape(MLA_KV_RANK, hw)
    wukt = wuk.transpose(1, 2, 0).reshape(MLA_HEADS * MLA_NOPE, MLA_KV_RANK)
    wuv = W["mla_wuv"][l].reshape(MLA_KV_RANK, MLA_HEADS * MLA_VDIM)
    proj = dict(g=r2(W["norm_mix"][l]), win=win_t, layer=l,
                qn=r2(W["mla_q_norm"][l]), wq=wq.astype(BF16), kvn=r2(W["mla_kv_norm"][l]), rope=_rope_rows(),
                wuk=wuk_pad.astype(BF16), wuv=wuv.astype(BF16), wukt=wukt.astype(BF16))
    zr = lambda n: jnp.zeros((n, RW_WIDTH), F32)
    rwk = dict(mu=r2(W["rw_mu"][l]), w0=r2(W["rw_w0"][l]),
               w2=jnp.concatenate([W["rw_w2"][l], zr(96)], axis=0).astype(BF16),
               a0=r2(W["rw_a0"][l]),
               a2=jnp.concatenate([zr(32), W["rw_a2"][l], zr(64)], axis=0).astype(BF16),
               g2=jnp.concatenate([zr(64), W["rw_g2"][l]], axis=0).astype(BF16),
               kk=r2(W["rw_kk"][l]), ka=r2(W["rw_ka"][l]), rk=r2(W["rw_rk"][l]),
               lnw=r2(W["rw_ln_w"][l]), lnb=r2(W["rw_ln_b"][l]))
    return dict(
        proj=proj, rwkv=rwk,
        ffn1=(r2(W["norm_ffn1"][l]), W["ffn1_wi"], W["ffn1_wo"], l),
        ffn2=(r2(W["norm_ffn2"][l]), W["ffn2_wi"], W["ffn2_wo"], l),
        wout=W["w_out"], hgn=r2(jnp.tile(W["hg_norm"][l], HG_HEADS)))


def _trunk(x, B, T, pos0, LW, W, sample, st=None):
    n = B * T
    x = x.reshape(n, D_MODEL)
    hg_out, rw_out, sh_out = [], [], []
    prev = None
    for l in range(DEPTH):
        lw = LW[l]
        ffn = _ffn_stream if sample else _ffn
        x = ffn(x, *lw["ffn1"])
        if sample:
            p_hg, p_rw, q, qlat, c, kpe = _proj(x, lw["proj"], T, pos0, True, prev)
            o_hg, hg_s = _hgrn_sample(p_hg, W["hg_lb_logits"], lw["hgn"], st["hg"], l, B, T)
            shift_rows = jnp.repeat(st["sh"][l], T, axis=0)
            o_rw, rw_s = _rwkv_sample(p_rw, shift_rows, st["rw"].reshape(DEPTH, B, MIX_W, HEAD_D), lw["rwkv"], l, B, T)
            o_mla = _attn_sample(qlat, q, c, kpe, lw["proj"]["wuv"], st["ckv"], st["kpe"], st["pt"], l, B, T)
        else:
            p_hg, p_rw, q, k, v, c, kpe = _proj(x, lw["proj"], T, pos0, False, prev)
            o_hg, hg_s, o_rw, rw_s = _mixers_prompt(p_hg, p_rw, W["hg_lb_logits"], lw["hgn"], lw["rwkv"], l, B, T)
            o_mla = _attn_prompt(q, k, v, B, T)
        gf = W["norm_final"].reshape(1, -1) if l == DEPTH - 1 else None
        x = ffn(x, *lw["ffn2"], mix=(o_hg, o_rw, o_mla), wout=lw["wout"], gf=gf)
        hg_out.append(hg_s)
        rw_out.append(rw_s.reshape(B, RW_HEADS, RW_DH, RW_DH))
        sh_out.append(p_rw.reshape(B, T, RW_COLS)[:, -1])
        prev = (c, kpe)
    st_ = jnp.stack
    return (x.reshape(B, T, D_MODEL), st_(hg_out), st_(rw_out), st_(sh_out),
            c.reshape(DEPTH, B, T, MLA_KV_RANK), kpe.reshape(DEPTH, B, T, MLA_ROPE))


def kernel(x_prompt, x_sample, cache_mla_ckv, cache_mla_kpe, state_hgrn, state_rwkv, state_rwkv_shift, page_table, norm_ffn1, ffn1_wi, ffn1_wo, norm_mix, w_in, w_out, hg_lb_logits, hg_norm, rw_mu, rw_w0, rw_w2, rw_a0, rw_a2, rw_g2, rw_kk, rw_ka, rw_rk, rw_ln_w, rw_ln_b, mla_q_norm, mla_wqb, mla_kv_norm, mla_wuk, mla_wuv, norm_ffn2, ffn2_wi, ffn2_wo, norm_final):
    W = dict(norm_ffn1=norm_ffn1, ffn1_wi=ffn1_wi, ffn1_wo=ffn1_wo, norm_mix=norm_mix, w_in=w_in, w_out=w_out,
             hg_lb_logits=hg_lb_logits, hg_norm=hg_norm, rw_mu=rw_mu, rw_w0=rw_w0, rw_w2=rw_w2, rw_a0=rw_a0,
             rw_a2=rw_a2, rw_g2=rw_g2, rw_kk=rw_kk, rw_ka=rw_ka, rw_rk=rw_rk, rw_ln_w=rw_ln_w, rw_ln_b=rw_ln_b,
             mla_q_norm=mla_q_norm, mla_wqb=mla_wqb, mla_kv_norm=mla_kv_norm, mla_wuk=mla_wuk, mla_wuv=mla_wuv,
             norm_ffn2=norm_ffn2, ffn2_wi=ffn2_wi, ffn2_wo=ffn2_wo, norm_final=norm_final)
    LW = [_layer_weights(l, W) for l in range(DEPTH)]
    Bp, Tp, _ = x_prompt.shape
    y_p, hg_p, rw_p, sh_p, ckv_p, kpe_p = _trunk(x_prompt, Bp, Tp, 0, LW, W, False)
    Bs, Ts, _ = x_sample.shape
    past_len = page_table.shape[1] * cache_mla_ckv.shape[2]
    st = dict(hg=state_hgrn, rw=state_rwkv, sh=state_rwkv_shift, ckv=cache_mla_ckv,
              kpe=jnp.swapaxes(cache_mla_kpe, 2, 3), pt=page_table)
    y_s, hg_s, rw_s, sh_s, ckv_s, kpe_s = _trunk(x_sample, Bs, Ts, past_len, LW, W, True, st)
    return (y_p, y_s, hg_p, hg_s, rw_p, rw_s, sh_p, sh_s, ckv_p, ckv_s, kpe_p, kpe_s)
```

```python
import functools
import math

import jax
import jax.numpy as jnp
import numpy as np
from jax import lax
from jax.experimental import pallas as pl
from jax.experimental.pallas import tpu as pltpu

F32 = jnp.float32
BF16 = jnp.bfloat16

D_MODEL = 1024
DEPTH = 2
PAGE_SIZE = 128
EPS = 1e-6
D_FF = 2816
HG_HEADS = 4
HG_DK = 64
HG_KW = 256
HG_MIN_F = 1e-30
RW_HEADS = 4
RW_DH = 64
RW_WIDTH = 256
RW_COLS = 896
RW_GN_EPS = 64e-5
MLA_HEADS = 8
MLA_NOPE = 64
MLA_ROPE = 32
MLA_VDIM = 64
MLA_Q_RANK = 384
MLA_KV_RANK = 256
MLA_SCALE = (MLA_NOPE + MLA_ROPE) ** -0.5
ROPE_THETA = 10000.0

LANES = 128
HEAD_SLOT = 128
ROPE_LANE0 = 64
VMEM_LIMIT = 56 * 1024 * 1024
NEG_INF = -1e30

ROW_TILE = 512
MIX_BLOCK = 256
MIX_SEQS = 2
HG_CHUNK = 16
RW_CHUNK = 64
ATTN_TILE = 1024
ATTN_HEADS = 2
HG_COLS = 2 * HG_KW + 2 * HG_HEADS * HG_DK
MLA_COL0 = HG_COLS + RW_COLS


def _dot(a, b):
    return jnp.dot(a.astype(BF16), b.astype(BF16), preferred_element_type=F32)


def _dot_nt(a, b):
    return lax.dot_general(a.astype(BF16), b.astype(BF16), (((1,), (1,)), ((), ())), preferred_element_type=F32)


def _dot_xl(m, b, pieces=2):
    out = None
    rem = b
    for _ in range(pieces):
        part = rem.astype(BF16)
        term = jnp.dot(m, part, preferred_element_type=F32)
        out = term if out is None else out + term
        rem = rem - part.astype(F32)
    return out


def _dot_xr(a, m, pieces=2):
    out = None
    rem = a
    for _ in range(pieces):
        part = rem.astype(BF16)
        term = jnp.dot(part, m, preferred_element_type=F32)
        out = term if out is None else out + term
        rem = rem - part.astype(F32)
    return out


def _rms(x, g, eps=EPS):
    return x * lax.rsqrt(jnp.mean(x * x, axis=-1, keepdims=True) + eps) * g


def _sigmoid(x):
    return 1.0 / (1.0 + jnp.exp(-x))


def _iota(shape, dim):
    return lax.broadcasted_iota(jnp.int32, shape, dim)


def _head_block_ones(n, hd):
    return (_iota((n, n), 0) // hd == _iota((n, n), 1) // hd).astype(BF16)


def _full(shape):
    return pl.BlockSpec(shape, lambda *_: (0,) * len(shape), pipeline_mode=pl.Buffered(1))


def _layer_block(shape, layer):
    return pl.BlockSpec((None,) + tuple(shape[1:]), lambda *_: (layer,) + (0,) * (len(shape) - 1),
                        pipeline_mode=pl.Buffered(1))


def _params(sem):
    return pltpu.CompilerParams(dimension_semantics=sem, vmem_limit_bytes=VMEM_LIMIT)


FF_CHUNK = 256


def _ffn_kernel(*refs, has_mix, final):
    it = iter(refs)
    x_ref = next(it)
    if has_mix:
        ohg_ref, orw_ref, omla_ref, wout_ref = next(it), next(it), next(it), next(it)
    g_ref, wi_ref, wo_ref = next(it), next(it), next(it)
    gf_ref = next(it) if final else None
    o_ref = next(it)

    x = x_ref[...]
    if has_mix:
        o = jnp.concatenate([r[...].astype(BF16) for r in (ohg_ref, orw_ref, omla_ref)], axis=-1)
        x = x + jnp.dot(o, wout_ref[...].astype(BF16), preferred_element_type=F32)
    xn = _rms(x, g_ref[...]).astype(BF16)
    acc = jnp.zeros(x.shape, F32)
    for c in range(D_FF // FF_CHUNK):
        lo = c * FF_CHUNK
        a = jnp.dot(xn, wi_ref[:, lo:lo + FF_CHUNK].astype(BF16), preferred_element_type=F32)
        b = jnp.dot(xn, wi_ref[:, D_FF + lo:D_FF + lo + FF_CHUNK].astype(BF16), preferred_element_type=F32)
        h = (a * _sigmoid(a) * b).astype(BF16)
        acc = acc + jnp.dot(h, wo_ref[lo:lo + FF_CHUNK, :].astype(BF16), preferred_element_type=F32)
    y = x + 0.5 * acc
    if final:
        y = _rms(y, gf_ref[...])
    o_ref[...] = y


def _ffn(x, g, wi, wo, layer, mix=None, wout=None, gf=None, tm=ROW_TILE):
    n = x.shape[0]
    tm = min(tm, n)
    row = lambda w: pl.BlockSpec((tm, w), lambda i: (i, 0))
    args, specs = [x], [row(D_MODEL)]
    if mix is not None:
        for m in mix:
            args.append(m)
            specs.append(row(m.shape[1]))
        args.append(wout)
        specs.append(_layer_block(wout.shape, layer))
    args += [g, wi, wo]
    specs += [_full(g.shape), _layer_block(wi.shape, layer), _layer_block(wo.shape, layer)]
    if gf is not None:
        args.append(gf)
        specs.append(_full(gf.shape))
    return pl.pallas_call(
        functools.partial(_ffn_kernel, has_mix=mix is not None, final=gf is not None),
        grid=(n // tm,), in_specs=specs, out_specs=row(D_MODEL),
        out_shape=jax.ShapeDtypeStruct((n, D_MODEL), F32),
        compiler_params=_params(("parallel",)), name="ffn")(*args)


def _ffn_stream_kernel(*refs, has_mix, final):
    it = iter(refs)
    x_ref = next(it)
    if has_mix:
        ohg_ref, orw_ref, omla_ref, wout_ref = next(it), next(it), next(it), next(it)
    g_ref, wia_ref, wib_ref, wo_ref = next(it), next(it), next(it), next(it)
    gf_ref = next(it) if final else None
    o_ref = next(it)
    x_scr, xn_scr, acc_scr = next(it), next(it), next(it)
    c = pl.program_id(0)

    @pl.when(c == 0)
    def _():
        x = x_ref[...]
        if has_mix:
            o = jnp.concatenate([r[...].astype(BF16) for r in (ohg_ref, orw_ref, omla_ref)], axis=-1)
            x = x + jnp.dot(o, wout_ref[...].astype(BF16), preferred_element_type=F32)
        x_scr[...] = x
        xn_scr[...] = _rms(x, g_ref[...]).astype(BF16)
        acc_scr[...] = jnp.zeros(acc_scr.shape, F32)

    xn = xn_scr[...]
    a = jnp.dot(xn, wia_ref[...].astype(BF16), preferred_element_type=F32)
    b = jnp.dot(xn, wib_ref[...].astype(BF16), preferred_element_type=F32)
    h = (a * _sigmoid(a) * b).astype(BF16)
    acc_scr[...] += jnp.dot(h, wo_ref[...].astype(BF16), preferred_element_type=F32)

    @pl.when(c == pl.num_programs(0) - 1)
    def _():
        y = x_scr[...] + 0.5 * acc_scr[...]
        if final:
            y = _rms(y, gf_ref[...])
        o_ref[...] = y


def _ffn_stream(x, g, wi, wo, layer, mix=None, wout=None, gf=None):
    n = x.shape[0]
    n_chunks = D_FF // FF_CHUNK
    const = lambda a: pl.BlockSpec(a.shape, lambda c: (0,) * a.ndim)
    args, specs = [x], [const(x)]
    if mix is not None:
        args += list(mix) + [wout]
        specs += [const(m) for m in mix] + [_layer_block(wout.shape, layer)]
    args += [g, wi, wi, wo]
    specs += [const(g),
              pl.BlockSpec((None, D_MODEL, FF_CHUNK), lambda c: (layer, 0, c)),
              pl.BlockSpec((None, D_MODEL, FF_CHUNK), lambda c: (layer, 0, n_chunks + c)),
              pl.BlockSpec((None, FF_CHUNK, D_MODEL), lambda c: (layer, c, 0))]
    if gf is not None:
        args.append(gf)
        specs.append(const(gf))
    return pl.pallas_call(
        functools.partial(_ffn_stream_kernel, has_mix=mix is not None, final=gf is not None),
        grid=(n_chunks,), in_specs=specs, out_specs=pl.BlockSpec((n, D_MODEL), lambda c: (0, 0)),
        out_shape=jax.ShapeDtypeStruct((n, D_MODEL), F32),
        scratch_shapes=[pltpu.VMEM((n, D_MODEL), F32), pltpu.VMEM((n, D_MODEL), BF16), pltpu.VMEM((n, D_MODEL), F32)],
        compiler_params=_params(("arbitrary",)), name="ffn_stream")(*args)


def _proj_kernel(*refs, seq_len, pos0, sample, layer):
    it = iter(refs)
    x_ref, g_ref, win_ref = (next(it) for _ in range(3))
    qn_ref, wq_ref, kvn_ref, rope_ref = (next(it) for _ in range(4))
    cprev_ref, kprev_ref = (next(it), next(it)) if layer else (None, None)
    if sample:
        wukt_ref = next(it)
    else:
        wuk_ref, wuv_ref = next(it), next(it)
    phg_ref, prw_ref, q_ref = next(it), next(it), next(it)
    if sample:
        qlat_ref = next(it)
    else:
        k_ref, v_ref = next(it), next(it)
    c_ref, kpe_ref = next(it), next(it)

    tm = x_ref.shape[0]
    xn = _rms(x_ref[...], g_ref[...]).astype(BF16)
    mla_lat = MLA_Q_RANK + MLA_KV_RANK
    phg_ref[...] = _dot_nt(xn, win_ref[0:HG_COLS, :])
    prw_ref[...] = _dot_nt(xn, win_ref[HG_COLS:MLA_COL0, :])
    pm = _dot_nt(xn, win_ref[MLA_COL0:MLA_COL0 + mla_lat, :])
    wk = win_ref[MLA_COL0 + mla_lat:MLA_COL0 + mla_lat + MLA_ROPE, :]
    zrow = lambda n_: jnp.zeros((n_, D_MODEL), F32)
    tail = HEAD_SLOT - ROPE_LANE0 - MLA_ROPE
    wk2 = jnp.concatenate([zrow(ROPE_LANE0), wk, zrow(tail),
                           zrow(ROPE_LANE0), wk[MLA_ROPE // 2:], wk[:MLA_ROPE // 2], zrow(tail)], axis=0)
    pk = _dot_nt(xn, wk2)

    row = pl.program_id(0) * tm + _iota((tm, 1), 0)
    pos = (pos0 + (row & (seq_len - 1))).astype(F32)
    rope = rope_ref[...]
    ang = pos * rope[0:1, :]
    cos_s = rope[1:2, :] + rope[3:4, :] * jnp.cos(ang)
    sin_s = rope[2:3, :] * jnp.sin(ang)

    c = _rms(pm[:, MLA_Q_RANK:MLA_Q_RANK + MLA_KV_RANK], kvn_ref[...])
    if layer:
        c_ref[0:layer] = cprev_ref[...]
        kpe_ref[0:layer] = kprev_ref[...]
    c_ref[layer] = c
    k_slot = pk[:, :HEAD_SLOT] * cos_s + pk[:, HEAD_SLOT:] * sin_s
    kpe_ref[layer] = k_slot[:, ROPE_LANE0:ROPE_LANE0 + MLA_ROPE]

    qn = _rms(pm[:, :MLA_Q_RANK], qn_ref[...]).astype(BF16)
    q2 = jnp.dot(qn, wq_ref[...], preferred_element_type=F32)
    hw = MLA_HEADS * HEAD_SLOT
    q_scale = MLA_SCALE if sample else MLA_SCALE * math.log2(math.e)
    cos_q = jnp.concatenate([cos_s * q_scale] * MLA_HEADS, axis=-1)
    sin_q = jnp.concatenate([sin_s * q_scale] * MLA_HEADS, axis=-1)
    q = q2[:, :hw] * cos_q + q2[:, hw:] * sin_q
    if sample:
        q_ref[...] = q
        for h in range(MLA_HEADS):
            qh = q[:, h * HEAD_SLOT:h * HEAD_SLOT + MLA_NOPE]
            qlat_ref[:, h * MLA_KV_RANK:(h + 1) * MLA_KV_RANK] = _dot(qh, wukt_ref[h * MLA_NOPE:(h + 1) * MLA_NOPE, :])
    else:
        q_ref[...] = q.astype(BF16)
        cb = c.astype(BF16)
        kn = jnp.dot(cb, wuk_ref[...], preferred_element_type=F32)
        k_ref[...] = (kn + jnp.concatenate([k_slot] * MLA_HEADS, axis=-1)).astype(BF16)
        v_ref[...] = jnp.dot(cb, wuv_ref[...], preferred_element_type=F32).astype(BF16)


def _proj(x, pw, seq_len, pos0, sample, prev=None, tm=ROW_TILE):
    n = x.shape[0]
    tm = min(tm, n)
    row = lambda w: pl.BlockSpec((tm, w), lambda i: (i, 0))
    stack = lambda depth, w: pl.BlockSpec((depth, tm, w), lambda i: (0, i, 0))
    hw = MLA_HEADS * HEAD_SLOT
    layer = pw["layer"]
    args = [x, pw["g"], pw["win"], pw["qn"], pw["wq"], pw["kvn"], pw["rope"]]
    specs = [row(D_MODEL)] + [_full(a.shape) for a in args[1:]]
    specs[2] = _layer_block(pw["win"].shape, layer)
    if layer:
        args += list(prev)
        specs += [stack(layer, MLA_KV_RANK), stack(layer, MLA_ROPE)]
    tail = [pw["wukt"]] if sample else [pw["wuk"], pw["wuv"]]
    args += tail
    specs += [_full(a.shape) for a in tail]
    qdt = F32 if sample else BF16
    outs = [((n, HG_COLS), F32), ((n, RW_COLS), F32), ((n, hw), qdt)]
    outs += [((n, MLA_HEADS * MLA_KV_RANK), F32)] if sample else [((n, hw), BF16), ((n, MLA_HEADS * MLA_VDIM), BF16)]
    out_specs = [row(s[1]) for s, _ in outs] + [stack(layer + 1, MLA_KV_RANK), stack(layer + 1, MLA_ROPE)]
    outs += [((layer + 1, n, MLA_KV_RANK), F32), ((layer + 1, n, MLA_ROPE), F32)]
    return pl.pallas_call(
        functools.partial(_proj_kernel, seq_len=seq_len, pos0=pos0, sample=sample, layer=layer),
        grid=(n // tm,), in_specs=specs, out_specs=out_specs,
        out_shape=[jax.ShapeDtypeStruct(s, d) for s, d in outs],
        compiler_params=_params(("parallel",)), name="proj")(*args)


MIX_W = 256
HEAD_D = 64


def _chunk_masks(L, C):
    r, c = _iota((L, L), 0), _iota((L, L), 1)
    same = (r // C) == (c // C)
    return same, same & (c <= r), same & (c < r)


def _chunk_last(a, nck, C):
    a3 = a.reshape(nck, C, a.shape[-1])
    return jnp.broadcast_to(a3[:, C - 1:C, :], a3.shape).reshape(a.shape)


def _expand_state(s_kv):
    t = jnp.concatenate([s_kv] * (MIX_W // HEAD_D), axis=0)
    bd = (_iota((MIX_W, MIX_W), 0) // HEAD_D) == (_iota((MIX_W, MIX_W), 1) // HEAD_D)
    return jnp.where(bd, t, 0.0)


def _fold_state(hbd):
    out = hbd[0:HEAD_D]
    for h in range(1, MIX_W // HEAD_D):
        out = out + hbd[h * HEAD_D:(h + 1) * HEAD_D]
    return out


def _hgrn_steps(*refs, L, C, layer, chained):
    it = iter(refs)
    p_ref, lbl_ref, gn_ref = next(it), next(it), next(it)
    s0_ref = None if chained else next(it)
    o_ref, sout_ref = next(it), next(it)
    h_scr = next(it) if chained else None

    nck = L // C
    p = p_ref[...]
    q, fz, vi, gate = (p[:, i * MIX_W:(i + 1) * MIX_W] for i in range(4))

    lg = lbl_ref[...]
    e = jnp.exp(lg - jnp.max(lg, axis=0, keepdims=True))
    pr = e / jnp.sum(e, axis=0, keepdims=True)
    cs = pr[0:1]
    for i in range(1, layer + 1):
        cs = cs + pr[i:i + 1]
    lb = jnp.clip(cs - pr[0:1], 0.0, 1.0)

    f = lb + (1.0 - lb) * _sigmoid(fz)
    log_f = jnp.log(jnp.maximum(f, HG_MIN_F))
    k = (1.0 - lb) * _sigmoid(-fz)

    same, incl, _ = _chunk_masks(L, C)
    b = _dot_xl(incl.astype(BF16), log_f, pieces=3)
    b_last = _chunk_last(b, nck, C)
    ones_h = _head_block_ones(MIX_W, HEAD_D)

    b3, q3, k3, v3 = (a.reshape(nck, C, MIX_W) for a in (b * math.log2(math.e), q, k, vi))
    SUB = 8
    tiles = [jnp.zeros((nck, min(SUB, C), MIX_W), F32) for _ in range(max(C // SUB, 1))]
    for s in range(C):
        t0 = (s // SUB) * SUB
        bt, qt = b3[:, t0:, :], q3[:, t0:, :]
        rows = C - t0
        w = jnp.exp2(bt - b3[:, s:s + 1, :])
        x = jnp.where(t0 + _iota((nck, rows, MIX_W), 1) >= s, qt * w * k3[:, s:s + 1, :], 0.0)
        a_s = _dot(x.reshape(nck * rows, MIX_W), ones_h).reshape(nck, rows, MIX_W)
        upd = a_s * v3[:, s:s + 1, :]
        for ti in range(t0 // SUB, len(tiles)):
            lo = ti * SUB - t0
            tiles[ti] = tiles[ti] + upd[:, lo:lo + SUB, :]
        yield
    o_intra = jnp.concatenate(tiles, axis=1).reshape(L, MIX_W)

    qe = q * jnp.exp(b)
    kd_t = (k * jnp.exp(b_last - b)).T
    e_t = jnp.exp(b_last).T
    bd = (_iota((MIX_W, MIX_W), 0) // HEAD_D) == (_iota((MIX_W, MIX_W), 1) // HEAD_D)
    col = _iota((1, L), 1)

    if chained:
        hbd = h_scr[...]
    gks = []
    for ck in range(nck):
        gks.append(jnp.where(bd, _dot(jnp.where((col // C) == ck, kd_t, 0.0), vi), 0.0))
        yield
    dcols = [jnp.sum(jnp.where(col == ck * C, e_t, 0.0), axis=1, keepdims=True) for ck in range(nck)]
    o_parts = []
    for ck in range(nck):
        if not chained:
            s_kv = jnp.concatenate([s0_ref[ck, h] for h in range(HG_HEADS)], axis=-1)
            hbd = _expand_state(s_kv)
        o_parts.append(_dot(qe[ck * C:(ck + 1) * C], hbd))
        hbd = dcols[ck] * hbd + gks[ck]
        if not chained:
            s_new = _fold_state(hbd)
            for h in range(HG_HEADS):
                sout_ref[ck, h] = s_new[:, h * HEAD_D:(h + 1) * HEAD_D]
    if chained:
        h_scr[...] = hbd
        s_new = _fold_state(hbd)
        for h in range(HG_HEADS):
            sout_ref[h] = s_new[:, h * HEAD_D:(h + 1) * HEAD_D]

    o = o_intra + jnp.concatenate(o_parts, axis=0)
    ms = _dot_xr(o * o, ones_h) * (1.0 / HEAD_D)
    on = o * lax.rsqrt(ms + EPS) * gn_ref[...]
    o_ref[...] = (on * (gate * _sigmoid(gate))).astype(o_ref.dtype)


def _drain(*gens):
    alive = list(gens)
    while alive:
        for g in list(alive):
            try:
                next(g)
            except StopIteration:
                alive.remove(g)


def _hgrn_kernel(*refs, **kw):
    _drain(_hgrn_steps(*refs, **kw))


def _hgrn_sample(p_hg, lb_logits, gn, s0, layer, B, T):
    n = p_hg.shape[0]
    return pl.pallas_call(
        functools.partial(_hgrn_kernel, L=n, C=T, layer=layer, chained=False),
        grid=(1,),
        in_specs=[_full((n, HG_COLS)), _full(lb_logits.shape), _full(gn.shape), _layer_block(s0.shape, layer)],
        out_specs=[_full((n, MIX_W)), _full((B, HG_HEADS, HEAD_D, HEAD_D))],
        out_shape=[jax.ShapeDtypeStruct((n, MIX_W), BF16),
                   jax.ShapeDtypeStruct((B, HG_HEADS, HEAD_D, HEAD_D), F32)],
        compiler_params=_params(("arbitrary",)), name="hgrn")(p_hg, lb_logits, gn, s0)


def _rwkv_steps(*refs, L, C, chained):
    it = iter(refs)
    p_ref = next(it)
    if not chained:
        shrow_ref, s0_ref = next(it), next(it)
    (mu_ref, w0_ref, w2_ref, a0_ref, a2_ref, g2_ref, kkw_ref, ka_ref, rk_ref, lnw_ref, lnb_ref) = (next(it) for _ in range(11))
    y_ref, sout_ref = next(it), next(it)
    if chained:
        h_scr, carry_scr = next(it), next(it)

    nck = L // C
    p = p_ref[...]
    rowi = _iota((L, 1), 0)
    rolled = pltpu.roll(p, 1, 0)
    if chained:
        prev = jnp.where(rowi == 0, carry_scr[0:1, :], rolled)
        carry_scr[0:1, :] = p[L - 1:L, :]
    else:
        prev = jnp.where((rowi & (C - 1)) == 0, shrow_ref[...], rolled)
    ps = p + (prev - p) * mu_ref[...]
    r, k, v, tail = ps[:, 0:MIX_W], ps[:, MIX_W:2 * MIX_W], ps[:, 2 * MIX_W:3 * MIX_W], ps[:, 3 * MIX_W:RW_COLS]

    w_raw = -jax.nn.softplus(-(w0_ref[...] + _dot(jnp.tanh(tail), w2_ref[...]))) - 0.5
    lw = -jnp.exp(w_raw)
    a = _sigmoid(a0_ref[...] + _dot(tail, a2_ref[...]))
    g = _dot(_sigmoid(tail), g2_ref[...])

    ones_h = _head_block_ones(MIX_W, HEAD_D)
    kk = k * kkw_ref[...]
    kk = kk / jnp.maximum(jnp.sqrt(_dot_xr(kk * kk, ones_h)), 1e-12)
    k2 = k * (1.0 + (a - 1.0) * ka_ref[...])
    bb = kk * a

    same, incl, strict = _chunk_masks(L, C)
    gc = _dot_xl(incl.astype(BF16), lw)
    g_last = _chunk_last(gc, nck, C)
    eg, eng = jnp.exp(gc), jnp.exp(-gc)
    rg, kg, bg, kkg = r * eg, k2 * eng, bb * eng, kk * jnp.exp(gc - lw)
    to_end = jnp.exp(g_last - gc)
    kd_t, bd_t, e_t = (k2 * to_end).T, (bb * to_end).T, jnp.exp(g_last).T

    lane_h = _iota((1, MIX_W), 1) // HEAD_D
    eye = (_iota((L, L), 0) == _iota((L, L), 1)).astype(F32)
    u0 = jnp.zeros((L, MIX_W), F32)
    kkt = jnp.zeros((L, MIX_W), F32)
    qp = rg
    o0 = jnp.zeros((L, MIX_W), F32)
    heads = range(RW_HEADS)
    mhs = [lane_h == h for h in heads]
    kkg_hs = [jnp.where(mh, kkg, 0.0) for mh in mhs]
    rg_hs = [jnp.where(mh, rg, 0.0) for mh in mhs]
    a_kb, a_kk, a_rk, a_rb = [], [], [], []
    for h in heads:
        a_kb.append(jnp.where(strict, _dot_nt(kkg_hs[h], bg), 0.0))
        a_kk.append(jnp.where(strict, _dot_nt(kkg_hs[h], kg), 0.0))
        yield
    for h in heads:
        a_rk.append(jnp.where(incl, _dot_nt(rg_hs[h], kg), 0.0))
        a_rb.append(jnp.where(incl, _dot_nt(rg_hs[h], bg), 0.0))
        yield
    npow = [-a for a in a_kb]
    tinv = [eye + n_ for n_ in npow]
    for _ in range(int(math.log2(C)) - 1):
        npow = [_dot(n_, n_) for n_ in npow]
        yield
        tinv = [t_ + _dot(t_, n_) for t_, n_ in zip(tinv, npow)]
        yield
    x1 = [_dot(a_kk[h], v) for h in heads]
    yield
    u0_hs = [_dot(tinv[h], x1[h]) for h in heads]
    yield
    kkt_hs = [_dot(tinv[h], kkg) for h in heads]
    yield
    for h in heads:
        u0 = jnp.where(mhs[h], u0_hs[h], u0)
        kkt = jnp.where(mhs[h], kkt_hs[h], kkt)
    for h in heads:
        qp = qp - jnp.where(mhs[h], _dot(a_rb[h], kkt_hs[h]), 0.0)
        o0 = o0 + jnp.where(mhs[h], _dot(a_rk[h], v) - _dot(a_rb[h], u0_hs[h]), 0.0)
        yield

    bd = (_iota((MIX_W, MIX_W), 0) // HEAD_D) == (_iota((MIX_W, MIX_W), 1) // HEAD_D)
    col = _iota((1, L), 1)
    if chained:
        hbd = h_scr[...]
    bd_cs = [jnp.where((col // C) == ck, bd_t, 0.0) for ck in range(nck)]
    mbs, gbs = [], []
    for ck in range(nck):
        mbs.append(jnp.where(bd, _dot(bd_cs[ck], kkt), 0.0))
        gbs.append(jnp.where(bd, _dot(jnp.where((col // C) == ck, kd_t, 0.0), v) - _dot(bd_cs[ck], u0), 0.0))
        yield
    dcols = [jnp.sum(jnp.where(col == ck * C, e_t, 0.0), axis=1, keepdims=True) for ck in range(nck)]
    y_parts = []
    for ck in range(nck):
        if not chained:
            hbd = _expand_state(s0_ref[ck].T)
        y_parts.append(_dot(qp[ck * C:(ck + 1) * C], hbd))
        hbd = dcols[ck] * hbd - _dot(mbs[ck], hbd) + gbs[ck]
        yield
        if not chained:
            sout_ref[ck] = _fold_state(hbd).T
    if chained:
        h_scr[...] = hbd
        sout_ref[...] = _fold_state(hbd).T

    y = o0 + jnp.concatenate(y_parts, axis=0)
    mean = _dot_xr(y, ones_h) * (1.0 / HEAD_D)
    yc = y - mean
    var = _dot_xr(yc * yc, ones_h) * (1.0 / HEAD_D)
    yn = yc * lax.rsqrt(var + RW_GN_EPS) * lnw_ref[...] + lnb_ref[...]
    bonus = _dot_xr(r * k2 * rk_ref[...], ones_h) * v
    y_ref[...] = ((yn + bonus) * g).astype(y_ref.dtype)


def _rwkv_kernel(*refs, **kw):
    _drain(_rwkv_steps(*refs, **kw))


RWKV_PARAMS = ("mu", "w0", "w2", "a0", "a2", "g2", "kk", "ka", "rk", "lnw", "lnb")


def _rwkv_sample(p_rw, shift_rows, s0, rw, layer, B, T):
    n = p_rw.shape[0]
    wargs = [rw[k] for k in RWKV_PARAMS]
    return pl.pallas_call(
        functools.partial(_rwkv_kernel, L=n, C=T, chained=False),
        grid=(1,),
        in_specs=[_full((n, RW_COLS)), _full(shift_rows.shape), _layer_block(s0.shape, layer)] + [_full(a.shape) for a in wargs],
        out_specs=[_full((n, MIX_W)), _full((B, MIX_W, HEAD_D))],
        out_shape=[jax.ShapeDtypeStruct((n, MIX_W), BF16), jax.ShapeDtypeStruct((B, MIX_W, HEAD_D), F32)],
        compiler_params=_params(("arbitrary",)), name="rwkv")(p_rw, shift_rows, s0, *wargs)


def _mixers_kernel(*refs, L, hg_c, rw_c, layer):
    p_hg_ref, lbl_ref, gn_ref, p_rw_ref = refs[0:4]
    rw_params = refs[4:15]
    o_hg_ref, hg_s_ref, o_rw_ref, rw_s_ref = refs[15:19]
    hg_scr, rw_scr, carry_scr = refs[19:22]

    @pl.when(pl.program_id(1) == 0)
    def _():
        for scr in (hg_scr, rw_scr, carry_scr):
            scr[...] = jnp.zeros(scr.shape, F32)
    gens = []
    for sq in range(MIX_SEQS):
        gens.append(_rwkv_steps(p_rw_ref.at[sq], *rw_params, o_rw_ref.at[sq], rw_s_ref.at[sq],
                                rw_scr.at[sq], carry_scr.at[sq], L=L, C=rw_c, chained=True))
        gens.append(_hgrn_steps(p_hg_ref.at[sq], lbl_ref, gn_ref, o_hg_ref.at[sq], hg_s_ref.at[sq],
                                hg_scr.at[sq], L=L, C=hg_c, layer=layer, chained=True))
    _drain(*gens)


def _mixers_prompt(p_hg, p_rw, lb_logits, gn, rw, layer, B, T):
    n = p_hg.shape[0]
    wargs = [rw[k] for k in RWKV_PARAMS]
    L = min(MIX_BLOCK, T)
    nblk = T // L
    S = MIX_SEQS
    view = lambda a: a.reshape(B // S, S, nblk, L, a.shape[-1])
    blocks = lambda w: pl.BlockSpec((None, S, None, L, w), lambda b, j: (b, 0, j, 0, 0))
    state = lambda *dims: pl.BlockSpec((None, S) + dims, lambda b, j: (b, 0) + (0,) * len(dims))
    o_hg, hg_s, o_rw, rw_s = pl.pallas_call(
        functools.partial(_mixers_kernel, L=L, hg_c=min(HG_CHUNK, T), rw_c=min(RW_CHUNK, T), layer=layer),
        grid=(B // S, nblk),
        in_specs=[blocks(HG_COLS), _full(lb_logits.shape), _full(gn.shape), blocks(RW_COLS)] + [_full(a.shape) for a in wargs],
        out_specs=[blocks(MIX_W), state(HG_HEADS, HEAD_D, HEAD_D), blocks(MIX_W), state(MIX_W, HEAD_D)],
        out_shape=[jax.ShapeDtypeStruct((B // S, S, nblk, L, MIX_W), BF16),
                   jax.ShapeDtypeStruct((B // S, S, HG_HEADS, HEAD_D, HEAD_D), F32),
                   jax.ShapeDtypeStruct((B // S, S, nblk, L, MIX_W), BF16),
                   jax.ShapeDtypeStruct((B // S, S, MIX_W, HEAD_D), F32)],
        scratch_shapes=[pltpu.VMEM((S, MIX_W, MIX_W), F32), pltpu.VMEM((S, MIX_W, MIX_W), F32),
                        pltpu.VMEM((S, 8, RW_COLS), F32)],
        compiler_params=_params(("parallel", "arbitrary")), name="mixers")(view(p_hg), lb_logits, gn, view(p_rw), *wargs)
    return (o_hg.reshape(n, MIX_W), hg_s.reshape(B, HG_HEADS, HEAD_D, HEAD_D),
            o_rw.reshape(n, MIX_W), rw_s.reshape(B, MIX_W, HEAD_D))


def _attn_prompt_kernel(q_ref, k_ref, v_ref, o_ref, *, tq):
    i = pl.program_id(2)
    nh = ATTN_HEADS
    qs = [q_ref[:, hh * HEAD_SLOT:(hh + 1) * HEAD_SLOT] for hh in range(nh)]

    def step(j, carry, diagonal):
        start = pl.multiple_of(j * tq, tq)
        vb = v_ref[pl.ds(start, tq), :]
        new = []
        for hh in range(nh):
            m, l, acc = carry[hh]
            kb = k_ref[pl.ds(start, tq), hh * HEAD_SLOT:(hh + 1) * HEAD_SLOT]
            s = lax.dot_general(qs[hh], kb, (((1,), (1,)), ((), ())), preferred_element_type=F32)
            if diagonal:
                s = jnp.where(_iota((tq, tq), 1) <= _iota((tq, tq), 0), s, NEG_INF)
            m_new = jnp.maximum(m, jnp.max(s, axis=-1, keepdims=True))
            pexp = jnp.exp2(s - m_new)
            alpha = jnp.exp2(m - m_new)
            l = alpha * l + jnp.sum(pexp, axis=-1, keepdims=True)
            acc = alpha * acc + jnp.dot(pexp.astype(BF16), vb, preferred_element_type=F32)
            new.append((m_new, l, acc))
        return tuple(new)

    one = (jnp.full((tq, 1), NEG_INF, F32), jnp.zeros((tq, 1), F32), jnp.zeros((tq, nh * MLA_VDIM), F32))
    carry = lax.fori_loop(0, i, lambda j, c: step(j, c, False), (one,) * nh)
    final = step(i, carry, True)
    lane_head = _iota((tq, nh * MLA_VDIM), 1) // MLA_VDIM
    out = final[0][2] / final[0][1]
    for hh in range(1, nh):
        out = jnp.where(lane_head == hh, final[hh][2] / final[hh][1], out)
    o_ref[...] = out.astype(o_ref.dtype)


def _attn_prompt(q, k, v, B, T, tq=ATTN_TILE):
    n = q.shape[0]
    tq = min(tq, T)
    nq = T // tq
    return pl.pallas_call(
        functools.partial(_attn_prompt_kernel, tq=tq),
        grid=(B, MLA_HEADS // ATTN_HEADS, nq),
        in_specs=[pl.BlockSpec((tq, ATTN_HEADS * HEAD_SLOT), lambda b, hp, i: (b * nq + i, hp)),
                  pl.BlockSpec((T, ATTN_HEADS * HEAD_SLOT), lambda b, hp, i: (b, hp)),
                  pl.BlockSpec((T, ATTN_HEADS * MLA_VDIM), lambda b, hp, i: (b, hp))],
        out_specs=pl.BlockSpec((tq, ATTN_HEADS * MLA_VDIM), lambda b, hp, i: (b * nq + i, hp)),
        out_shape=jax.ShapeDtypeStruct((n, MLA_HEADS * MLA_VDIM), BF16),
        compiler_params=_params(("parallel", "parallel", "arbitrary")), name="attn_prompt")(q, k, v)


PAGE_GROUP = 16
SEQ_PER_STEP = 4


def _attn_sample_kernel(pt_ref, qlat_ref, q_ref, c_ref, kpe_ref, wuv_ref, ckv_hbm, kpe_hbm, o_ref,
                        cbuf, kbuf, sem, *, T, layer, n_pages):
    G, NS = PAGE_GROUP, SEQ_PER_STEP
    b0 = pl.program_id(0) * NS
    n_groups = n_pages // G
    R = MLA_HEADS * T

    def page_copies(g, slot, first_seq=b0):
        out = []
        for sq in range(NS):
            for i in range(G):
                pg = pt_ref[first_seq + sq, g * G + i]
                out.append(pltpu.make_async_copy(ckv_hbm.at[layer, pg], cbuf.at[slot, sq, i], sem.at[0, slot]))
                out.append(pltpu.make_async_copy(kpe_hbm.at[layer, pg], kbuf.at[slot, sq, i], sem.at[1, slot]))
        return out

    carry_over = n_groups % 2 == 0
    first_step = pl.program_id(0) == 0
    last_step = pl.program_id(0) == pl.num_programs(0) - 1

    def start_all(copies):
        for n_, cp in enumerate(copies):
            cp.start(priority=(n_ // 2) % 2)

    def start_first_group():
        start_all(page_copies(0, 0))

    if carry_over:
        pl.when(first_step)(start_first_group)
    else:
        start_first_group()

    qls, qps = [], []
    for sq in range(NS):
        rows = slice(sq * T, (sq + 1) * T)
        qls.append(jnp.concatenate([qlat_ref[rows, h * MLA_KV_RANK:(h + 1) * MLA_KV_RANK]
                                    for h in range(MLA_HEADS)], axis=0).astype(BF16))
        qs = jnp.concatenate([q_ref[rows, h * HEAD_SLOT:(h + 1) * HEAD_SLOT] for h in range(MLA_HEADS)], axis=0)
        qps.append(qs[:, ROPE_LANE0:ROPE_LANE0 + MLA_ROPE].astype(BF16))

    def update(carry, s, cb):
        m, l, acc = carry
        m_new = jnp.maximum(m, jnp.max(s, axis=-1, keepdims=True))
        pexp = jnp.exp(s - m_new)
        alpha = jnp.exp(m - m_new)
        l = alpha * l + jnp.sum(pexp, axis=-1, keepdims=True)
        acc = alpha * acc + jnp.dot(pexp.astype(BF16), cb, preferred_element_type=F32)
        return m_new, l, acc

    def group(g, carries):
        slot = g & 1

        @pl.when(g + 1 < n_groups)
        def _():
            start_all(page_copies(g + 1, 1 - slot))
        if carry_over:
            @pl.when((g + 1 == n_groups) & jnp.logical_not(last_step))
            def _():
                start_all(page_copies(0, 0, b0 + NS))
        for cp in page_copies(g, slot):
            cp.wait()
        seqs = range(NS)
        cbs = [cbuf[slot, sq].reshape(G * PAGE_SIZE, MLA_KV_RANK).astype(BF16) for sq in seqs]
        kts = [kbuf[slot, sq].astype(BF16) for sq in seqs]
        ss = [_dot_nt(qls[sq], cbs[sq]) for sq in seqs]
        ss = [ss[sq] + jnp.concatenate([jnp.dot(qps[sq], kts[sq][i], preferred_element_type=F32) for i in range(G)], axis=-1)
              for sq in seqs]
        ms = [jnp.maximum(carries[sq][0], jnp.max(ss[sq], axis=-1, keepdims=True)) for sq in seqs]
        ps = [jnp.exp(ss[sq] - ms[sq]) for sq in seqs]
        alphas = [jnp.exp(carries[sq][0] - ms[sq]) for sq in seqs]
        ls = [alphas[sq] * carries[sq][1] + jnp.sum(ps[sq], axis=-1, keepdims=True) for sq in seqs]
        accs = [alphas[sq] * carries[sq][2] + jnp.dot(ps[sq].astype(BF16), cbs[sq], preferred_element_type=F32) for sq in seqs]
        return tuple((ms[sq], ls[sq], accs[sq]) for sq in seqs)

    one = (jnp.full((R, 1), NEG_INF, F32), jnp.zeros((R, 1), F32), jnp.zeros((R, MLA_KV_RANK), F32))
    carries = lax.fori_loop(0, n_groups, group, (one,) * NS)

    pad = PAGE_SIZE - T
    tq = _iota((R, PAGE_SIZE), 0) & (T - 1)
    lane_h = _iota((T, MLA_HEADS * MLA_VDIM), 1) // MLA_VDIM
    for sq in range(NS):
        rows = slice(sq * T, (sq + 1) * T)
        cn = jnp.concatenate([c_ref[rows, :], jnp.zeros((pad, MLA_KV_RANK), F32)], axis=0).astype(BF16)
        kn = jnp.concatenate([kpe_ref[rows, :], jnp.zeros((pad, MLA_ROPE), F32)], axis=0).astype(BF16)
        s = jnp.where(_iota((R, PAGE_SIZE), 1) <= tq, _dot_nt(qls[sq], cn) + _dot_nt(qps[sq], kn), NEG_INF)
        _, l, acc = update(carries[sq], s, cn)
        z = _dot(acc / l, wuv_ref[...])
        out = jnp.zeros((T, MLA_HEADS * MLA_VDIM), F32)
        for h in range(MLA_HEADS):
            out = jnp.where(lane_h == h, z[h * T:(h + 1) * T], out)
        o_ref[rows, :] = out.astype(o_ref.dtype)


def _attn_sample(qlat, q, c, kpe, wuv, cache_ckv, cache_kpe_t, page_table, layer, B, T):
    n = qlat.shape[0]
    n_pages = page_table.shape[1]
    ns = SEQ_PER_STEP
    row = lambda w: pl.BlockSpec((ns * T, w), lambda b, pt: (b, 0))
    new = lambda w: pl.BlockSpec((None, ns * T, w), lambda b, pt: (layer, b, 0))
    hbm = pl.BlockSpec(memory_space=pl.ANY)
    grid_spec = pltpu.PrefetchScalarGridSpec(
        num_scalar_prefetch=1, grid=(B // ns,),
        in_specs=[row(qlat.shape[1]), row(q.shape[1]), new(MLA_KV_RANK), new(MLA_ROPE),
                  pl.BlockSpec(wuv.shape, lambda b, pt: (0, 0)), hbm, hbm],
        out_specs=row(MLA_HEADS * MLA_VDIM),
        scratch_shapes=[pltpu.VMEM((2, ns, PAGE_GROUP, PAGE_SIZE, MLA_KV_RANK), F32),
                        pltpu.VMEM((2, ns, PAGE_GROUP, MLA_ROPE, PAGE_SIZE), F32),
                        pltpu.SemaphoreType.DMA((2, 2))])
    return pl.pallas_call(
        functools.partial(_attn_sample_kernel, T=T, layer=layer, n_pages=n_pages), grid_spec=grid_spec,
        out_shape=jax.ShapeDtypeStruct((n, MLA_HEADS * MLA_VDIM), F32),
        compiler_params=_params(("arbitrary",)), name="attn_sample")(
            page_table, qlat, q, c, kpe, wuv, cache_ckv, cache_kpe_t)


def _rope_rows():
    half = MLA_ROPE // 2
    inv = jnp.exp(-jnp.log(ROPE_THETA) * jnp.arange(half, dtype=F32) / half)
    lane = np.arange(LANES)
    on = (lane >= ROPE_LANE0) & (lane < ROPE_LANE0 + MLA_ROPE)
    inv_l = jnp.concatenate([jnp.zeros(ROPE_LANE0, F32), inv, inv, jnp.zeros(LANES - ROPE_LANE0 - MLA_ROPE, F32)])
    sign = np.where(on, np.where(lane < ROPE_LANE0 + half, -1.0, 1.0), 0.0)
    rows = [inv_l, (lane < ROPE_LANE0).astype(np.float32), sign, on.astype(np.float32)]
    rows += [np.zeros(LANES, np.float32)] * 4
    return jnp.stack([jnp.asarray(r, F32) for r in rows])


def _layer_weights(l, W):
    half = MLA_ROPE // 2
    r2 = lambda a: a.reshape(1, -1)
    win_t = jnp.swapaxes(W["w_in"], 1, 2)
    wqb = W["mla_wqb"][l].reshape(MLA_Q_RANK, MLA_HEADS, MLA_NOPE + MLA_ROPE)
    zq = lambda n: jnp.zeros((MLA_Q_RANK, MLA_HEADS, n), F32)
    rest = HEAD_SLOT - MLA_NOPE - MLA_ROPE
    wq_pad = jnp.concatenate([wqb, zq(rest)], axis=-1)
    wq_rot = jnp.concatenate([zq(MLA_NOPE), wqb[..., MLA_NOPE + half:], wqb[..., MLA_NOPE:MLA_NOPE + half], zq(rest)], axis=-1)
    hw = MLA_HEADS * HEAD_SLOT
    wq = jnp.concatenate([wq_pad.reshape(MLA_Q_RANK, hw), wq_rot.reshape(MLA_Q_RANK, hw)], axis=1)
    wuk = W["mla_wuk"][l]
    wuk_pad = jnp.concatenate([wuk, jnp.zeros((MLA_KV_RANK, MLA_HEADS, HEAD_SLOT - MLA_NOPE), F32)], axis=-1).reshape(MLA_KV_RANK, hw)
    wukt = wuk.transpose(1, 2, 0).reshape(MLA_HEADS * MLA_NOPE, MLA_KV_RANK)
    wuv = W["mla_wuv"][l].reshape(MLA_KV_RANK, MLA_HEADS * MLA_VDIM)
    proj = dict(g=r2(W["norm_mix"][l]), win=win_t, layer=l,
                qn=r2(W["mla_q_norm"][l]), wq=wq.astype(BF16), kvn=r2(W["mla_kv_norm"][l]), rope=_rope_rows(),
                wuk=wuk_pad.astype(BF16), wuv=wuv.astype(BF16), wukt=wukt.astype(BF16))
    zr = lambda n: jnp.zeros((n, RW_WIDTH), F32)
    rwk = dict(mu=r2(W["rw_mu"][l]), w0=r2(W["rw_w0"][l]),
               w2=jnp.concatenate([W["rw_w2"][l], zr(96)], axis=0).astype(BF16),
               a0=r2(W["rw_a0"][l]),
               a2=jnp.concatenate([zr(32), W["rw_a2"][l], zr(64)], axis=0).astype(BF16),
               g2=jnp.concatenate([zr(64), W["rw_g2"][l]], axis=0).astype(BF16),
               kk=r2(W["rw_kk"][l]), ka=r2(W["rw_ka"][l]), rk=r2(W["rw_rk"][l]),
               lnw=r2(W["rw_ln_w"][l]), lnb=r2(W["rw_ln_b"][l]))
    return dict(
        proj=proj, rwkv=rwk,
        ffn1=(r2(W["norm_ffn1"][l]), W["ffn1_wi"], W["ffn1_wo"], l),
        ffn2=(r2(W["norm_ffn2"][l]), W["ffn2_wi"], W["ffn2_wo"], l),
        wout=W["w_out"], hgn=r2(jnp.tile(W["hg_norm"][l], HG_HEADS)))


def _trunk(x, B, T, pos0, LW, W, sample, st=None):
    n = B * T
    x = x.reshape(n, D_MODEL)
    hg_out, rw_out, sh_out = [], [], []
    prev = None
    for l in range(DEPTH):
        lw = LW[l]
        ffn = _ffn_stream if sample else _ffn
        x = ffn(x, *lw["ffn1"])
        if sample:
            p_hg, p_rw, q, qlat, c, kpe = _proj(x, lw["proj"], T, pos0, True, prev)
            o_hg, hg_s = _hgrn_sample(p_hg, W["hg_lb_logits"], lw["hgn"], st["hg"], l, B, T)
            shift_rows = jnp.repeat(st["sh"][l], T, axis=0)
            o_rw, rw_s = _rwkv_sample(p_rw, shift_rows, st["rw"].reshape(DEPTH, B, MIX_W, HEAD_D), lw["rwkv"], l, B, T)
            o_mla = _attn_sample(qlat, q, c, kpe, lw["proj"]["wuv"], st["ckv"], st["kpe"], st["pt"], l, B, T)
        else:
            p_hg, p_rw, q, k, v, c, kpe = _proj(x, lw["proj"], T, pos0, False, prev)
            o_hg, hg_s, o_rw, rw_s = _mixers_prompt(p_hg, p_rw, W["hg_lb_logits"], lw["hgn"], lw["rwkv"], l, B, T)
            o_mla = _attn_prompt(q, k, v, B, T)
        gf = W["norm_final"].reshape(1, -1) if l == DEPTH - 1 else None
        x = ffn(x, *lw["ffn2"], mix=(o_hg, o_rw, o_mla), wout=lw["wout"], gf=gf)
        hg_out.append(hg_s)
        rw_out.append(rw_s.reshape(B, RW_HEADS, RW_DH, RW_DH))
        sh_out.append(p_rw.reshape(B, T, RW_COLS)[:, -1])
        prev = (c, kpe)
    st_ = jnp.stack
    return (x.reshape(B, T, D_MODEL), st_(hg_out), st_(rw_out), st_(sh_out),
            c.reshape(DEPTH, B, T, MLA_KV_RANK), kpe.reshape(DEPTH, B, T, MLA_ROPE))


def kernel(x_prompt, x_sample, cache_mla_ckv, cache_mla_kpe, state_hgrn, state_rwkv, state_rwkv_shift, page_table, norm_ffn1, ffn1_wi, ffn1_wo, norm_mix, w_in, w_out, hg_lb_logits, hg_norm, rw_mu, rw_w0, rw_w2, rw_a0, rw_a2, rw_g2, rw_kk, rw_ka, rw_rk, rw_ln_w, rw_ln_b, mla_q_norm, mla_wqb, mla_kv_norm, mla_wuk, mla_wuv, norm_ffn2, ffn2_wi, ffn2_wo, norm_final):
    W = dict(norm_ffn1=norm_ffn1, ffn1_wi=ffn1_wi, ffn1_wo=ffn1_wo, norm_mix=norm_mix, w_in=w_in, w_out=w_out,
             hg_lb_logits=hg_lb_logits, hg_norm=hg_norm, rw_mu=rw_mu, rw_w0=rw_w0, rw_w2=rw_w2, rw_a0=rw_a0,
             rw_a2=rw_a2, rw_g2=rw_g2, rw_kk=rw_kk, rw_ka=rw_ka, rw_rk=rw_rk, rw_ln_w=rw_ln_w, rw_ln_b=rw_ln_b,
             mla_q_norm=mla_q_norm, mla_wqb=mla_wqb, mla_kv_norm=mla_kv_norm, mla_wuk=mla_wuk, mla_wuv=mla_wuv,
             norm_ffn2=norm_ffn2, ffn2_wi=ffn2_wi, ffn2_wo=ffn2_wo, norm_final=norm_final)
    LW = [_layer_weights(l, W) for l in range(DEPTH)]
    Bp, Tp, _ = x_prompt.shape
    y_p, hg_p, rw_p, sh_p, ckv_p, kpe_p = _trunk(x_prompt, Bp, Tp, 0, LW, W, False)
    Bs, Ts, _ = x_sample.shape
    past_len = page_table.shape[1] * cache_mla_ckv.shape[2]
    st = dict(hg=state_hgrn, rw=state_rwkv, sh=state_rwkv_shift, ckv=cache_mla_ckv,
              kpe=jnp.swapaxes(cache_mla_kpe, 2, 3), pt=page_table)
    y_s, hg_s, rw_s, sh_s, ckv_s, kpe_s = _trunk(x_sample, Bs, Ts, past_len, LW, W, True, st)
    return (y_p, y_s, hg_p, hg_s, rw_p, rw_s, sh_p, sh_s, ckv_p, ckv_s, kpe_p, kpe_s)
```
